```python
import math
import jax, jax.numpy as jnp
from jax import lax
import numpy as np

D_MODEL = 2048
BATCH = 4
SEQ = 4096
DEPTH = 2

EPS = 1e-6
N_MOD = 6
SSM_WIDTH = D_MODEL // 2
SSM_GROUP = 16
SSM_GROUPS = SSM_WIDTH // SSM_GROUP
SSM_STATE = 64
HEAD_DIM = 128
ATTN_WIDTH = D_MODEL // 2
ATTN_HEADS = ATTN_WIDTH // HEAD_DIM
DILATED_PATTERNS = ((128, 1), (512, 4), (2048, 16))
BLOCK = 128
IN_SIZES = (SSM_WIDTH, ATTN_WIDTH, ATTN_WIDTH, ATTN_WIDTH, D_MODEL, D_MODEL)
IN_COLS = sum(IN_SIZES)
IN_SPLITS = tuple(int(v) for v in np.cumsum(IN_SIZES)[:-1])
PEER_HEADS = 8
PEER_NKEYS = 128
PEER_EXPERTS = PEER_NKEYS * PEER_NKEYS
PEER_TOPK = 16
PEER_DQ = 256
PEER_CHUNK = 128

kernel_name = 'hybrid_s5_dilated_attn_peer_adaln'


def _rmsnorm(x, gain):
    x32 = x.astype(jnp.float32)
    y = x32 * lax.rsqrt(jnp.mean(x32 * x32, axis=-1, keepdims=True) + EPS)
    return (y * gain.astype(jnp.float32)).astype(x.dtype)


def _complex_affine_combine(e1, e2):
    a1r, a1i, b1r, b1i = e1
    a2r, a2i, b2r, b2i = e2
    ar = a2r * a1r - a2i * a1i
    ai = a2r * a1i + a2i * a1r
    br = a2r * b1r - a2i * b1i + b2r
    bi = a2r * b1i + a2i * b1r + b2i
    return (ar, ai, br, bi)


def _s5_branch(u, lam_re, lam_im, log_dt, b_re, b_im, c_re, c_im, d_skip, w_glu):
    bsz, seq, _ = u.shape
    f32 = jnp.float32
    u32 = u.astype(f32)
    ug = u32.reshape(bsz, seq, SSM_GROUPS, SSM_GROUP)
    dt = jnp.exp(log_dt.astype(f32))[:, None]
    lr, li = lam_re.astype(f32), lam_im.astype(f32)
    mag = jnp.exp(lr * dt)
    a_re = mag * jnp.cos(li * dt)
    a_im = mag * jnp.sin(li * dt)
    inv = 1.0 / (lr * lr + li * li)
    coef_re = ((a_re - 1.0) * lr + a_im * li) * inv
    coef_im = (a_im * lr - (a_re - 1.0) * li) * inv
    br, bi = b_re.astype(f32), b_im.astype(f32)
    bbar_re = coef_re[..., None] * br - coef_im[..., None] * bi
    bbar_im = coef_re[..., None] * bi + coef_im[..., None] * br
    bu_re = jnp.einsum('bsgc,gpc->bsgp', ug, bbar_re)
    bu_im = jnp.einsum('bsgc,gpc->bsgp', ug, bbar_im)
    a_re_t = jnp.broadcast_to(a_re, (1, seq) + a_re.shape)
    a_im_t = jnp.broadcast_to(a_im, (1, seq) + a_im.shape)
    _, _, h_re, h_im = lax.associative_scan(
        _complex_affine_combine, (a_re_t, a_im_t, bu_re, bu_im), axis=1)
    y = (jnp.einsum('bsgp,gcp->bsgc', h_re, c_re.astype(f32))
         - jnp.einsum('bsgp,gcp->bsgc', h_im, c_im.astype(f32)))
    y = y.reshape(bsz, seq, SSM_WIDTH) + d_skip.astype(f32) * u32
    y = jax.nn.gelu(y, approximate=False)
    y = y * jax.nn.sigmoid(y @ w_glu.astype(f32))
    return y.astype(u.dtype)


def _dilated_window(q, k, v, window, dilation):
    bsz, seq, nh, hd = q.shape
    f32 = jnp.float32
    n_back = window // dilation
    sub_len = seq // dilation
    n_blk = -(-sub_len // BLOCK)
    pad = n_blk * BLOCK - sub_len

    def to_sub(t):
        t = t.reshape(bsz, sub_len, dilation, nh, hd).transpose(0, 2, 3, 1, 4)
        t = jnp.pad(t, ((0, 0), (0, 0), (0, 0), (0, pad), (0, 0)))
        return t.reshape(bsz, dilation, nh, n_blk, BLOCK, hd)

    def with_prev(t):
        prev = jnp.pad(t[:, :, :, :-1], ((0, 0), (0, 0), (0, 0), (1, 0), (0, 0), (0, 0)))
        return jnp.concatenate([prev, t], axis=4)

    qb = to_sub(q)
    kk = with_prev(to_sub(k))
    vv = with_prev(to_sub(v))
    s = jnp.einsum('bdhnqe,bdhnke->bdhnqk', qb, kk).astype(f32) * (hd ** -0.5)
    blk = jnp.arange(n_blk)[:, None, None]
    qi = jnp.arange(BLOCK)[None, :, None]
    kj = jnp.arange(2 * BLOCK)[None, None, :]
    dist = BLOCK + qi - kj
    key_pos = (blk - 1) * BLOCK + kj
    valid = (dist >= 0) & (dist <= n_back) & (key_pos >= 0)
    s = jnp.where(valid, s, -jnp.inf)
    m = jnp.max(s, axis=-1, keepdims=True)
    p = jnp.exp(s - m)
    den = jnp.sum(p, axis=-1, keepdims=True)
    o = jnp.einsum('bdhnqk,bdhnke->bdhnqe', p, vv.astype(f32)) / den
    lse = (m + jnp.log(den))[..., 0]

    def from_sub(t):
        t = t.reshape((bsz, dilation, nh, n_blk * BLOCK) + t.shape[5:])[:, :, :, :sub_len]
        t = jnp.moveaxis(t, 3, 1)
        return t.reshape((bsz, seq, nh) + t.shape[4:])

    return from_sub(o), from_sub(lse)


def _dilated_attention(q, k, v):
    bsz, seq, _ = q.shape
    shp = (bsz, seq, ATTN_HEADS, HEAD_DIM)
    qh, kh, vh = q.reshape(shp), k.reshape(shp), v.reshape(shp)
    outs, lses = [], []
    for window, dilation in DILATED_PATTERNS:
        o, lse = _dilated_window(qh, kh, vh, window, dilation)
        outs.append(o)
        lses.append(lse)
    weights = jax.nn.softmax(jnp.stack(lses), axis=0)
    out = jnp.einsum('pbsh,pbshe->bshe', weights, jnp.stack(outs))
    return out.reshape(bsz, seq, ATTN_WIDTH).astype(q.dtype)


def _peer(h, wq, subkeys, u_tab, v_tab):
    bsz, seq, d = h.shape
    f32 = jnp.float32
    q = (h @ wq).astype(f32).reshape(bsz, seq, PEER_HEADS, 2, PEER_DQ // 2)
    sk = subkeys.astype(f32)
    s1 = jnp.einsum('bshe,ke->bshk', q[..., 0, :], sk[0])
    s2 = jnp.einsum('bshe,ke->bshk', q[..., 1, :], sk[1])
    v1, i1 = lax.top_k(s1, PEER_TOPK)
    v2, i2 = lax.top_k(s2, PEER_TOPK)
    cand = (v1[..., :, None] + v2[..., None, :]).reshape(
        bsz, seq, PEER_HEADS, PEER_TOPK * PEER_TOPK)
    top_s, top_pos = lax.top_k(cand, PEER_TOPK)
    e1 = jnp.take_along_axis(i1, top_pos // PEER_TOPK, axis=-1)
    e2 = jnp.take_along_axis(i2, top_pos % PEER_TOPK, axis=-1)
    idx = e1 * PEER_NKEYS + e2
    gate = jax.nn.softmax(top_s, axis=-1)
    n_tok = bsz * seq
    n_chunk = n_tok // PEER_CHUNK
    xs = h.reshape(n_chunk, PEER_CHUNK, d)
    ids = idx.reshape(n_chunk, PEER_CHUNK, PEER_HEADS, PEER_TOPK)
    gs = gate.reshape(n_chunk, PEER_CHUNK, PEER_HEADS, PEER_TOPK)

    def chunk(args):
        xc, ic, gc = args
        u_sel = jnp.take(u_tab, ic, axis=0)
        act = jax.nn.gelu(jnp.einsum('thkd,td->thk', u_sel, xc).astype(f32), approximate=False)
        v_sel = jnp.take(v_tab, ic, axis=0)
        return jnp.einsum('thk,thkd->td', (gc * act).astype(v_tab.dtype), v_sel)

    out = lax.map(chunk, (xs, ids, gs))
    return out.reshape(bsz, seq, d).astype(h.dtype)


def setup_inputs(seed: int = 0) -> dict:
    key = jax.random.key(seed)
    ks = jax.random.split(key, 26)
    f32 = jnp.float32

    def nrm(k, shape, scale):
        return jax.random.normal(k, shape, f32) * scale

    L, D = DEPTH, D_MODEL
    G, P, C = SSM_GROUPS, SSM_STATE, SSM_GROUP
    n_idx = jnp.arange(P, dtype=f32)
    return {
        'x': nrm(ks[0], (BATCH, SEQ, D), 1.0),
        'c': nrm(ks[1], (BATCH, D), 1.0),
        'w_ada': nrm(ks[2], (L, D, N_MOD * D), 0.5 * D ** -0.5),
        'b_ada': nrm(ks[3], (L, N_MOD * D), 0.02),
        'norm_mix': 1.0 + nrm(ks[4], (L, D), 0.02),
        'norm_ffn': 1.0 + nrm(ks[5], (L, D), 0.02),
        'w_in': nrm(ks[6], (L, D, IN_COLS), D ** -0.5),
        'ssm_lambda_re': -0.5 + nrm(ks[7], (L, G, P), 0.01),
        'ssm_lambda_im': jnp.pi * n_idx + nrm(ks[8], (L, G, P), 0.01),
        'ssm_log_dt': jax.random.uniform(ks[9], (L, G), f32, math.log(1e-3), math.log(1e-1)),
        'ssm_b_re': nrm(ks[10], (L, G, P, C), (2 * C) ** -0.5),
        'ssm_b_im': nrm(ks[11], (L, G, P, C), (2 * C) ** -0.5),
        'ssm_c_re': nrm(ks[12], (L, G, C, P), (2 * P) ** -0.5),
        'ssm_c_im': nrm(ks[13], (L, G, C, P), (2 * P) ** -0.5),
        'ssm_d': nrm(ks[14], (L, SSM_WIDTH), 1.0),
        'w_glu': nrm(ks[15], (L, SSM_WIDTH, SSM_WIDTH), SSM_WIDTH ** -0.5),
        'w_br_ssm': nrm(ks[16], (L, SSM_WIDTH, D), SSM_WIDTH ** -0.5),
        'w_br_attn': nrm(ks[17], (L, ATTN_WIDTH, D), ATTN_WIDTH ** -0.5),
        'w_out': nrm(ks[18], (L, D, D), D ** -0.5),
        'peer_wq': nrm(ks[19], (L, D, PEER_HEADS * PEER_DQ), D ** -0.5),
        'peer_subkeys': nrm(ks[20], (L, 2, PEER_NKEYS, PEER_DQ // 2), (PEER_DQ // 2) ** -0.5),
        'peer_u': nrm(ks[21], (L, PEER_EXPERTS, D), D ** -0.5),
        'peer_v': nrm(ks[22], (L, PEER_EXPERTS, D), PEER_HEADS ** -0.5),
        'norm_final': 1.0 + nrm(ks[23], (D,), 0.02),
    }


def reference(x, c, w_ada, b_ada, norm_mix, norm_ffn, w_in, ssm_lambda_re, ssm_lambda_im,
              ssm_log_dt, ssm_b_re, ssm_b_im, ssm_c_re, ssm_c_im, ssm_d, w_glu, w_br_ssm,
              w_br_attn, w_out, peer_wq, peer_subkeys, peer_u, peer_v, norm_final):
    c_act = jax.nn.silu(c)
    for l in range(DEPTH):
        mod = (c_act @ w_ada[l] + b_ada[l])[:, None, :]
        sh_m, sc_m, g_m, sh_f, sc_f, g_f = jnp.split(mod, N_MOD, axis=-1)
        h = _rmsnorm(x, norm_mix[l]) * (1.0 + sc_m) + sh_m
        z = h @ w_in[l]
        u_ssm, q, k, v, gate_s, gate_a = jnp.split(z, IN_SPLITS, axis=-1)
        y_s = _s5_branch(u_ssm, ssm_lambda_re[l], ssm_lambda_im[l], ssm_log_dt[l],
                         ssm_b_re[l], ssm_b_im[l], ssm_c_re[l], ssm_c_im[l], ssm_d[l], w_glu[l])
        y_a = _dilated_attention(q, k, v)
        merged = (jax.nn.sigmoid(gate_s) * (y_s @ w_br_ssm[l])
                  + jax.nn.sigmoid(gate_a) * (y_a @ w_br_attn[l]))
        x = x + g_m * (merged @ w_out[l])
        h = _rmsnorm(x, norm_ffn[l]) * (1.0 + sc_f) + sh_f
        x = x + g_f * _peer(h, peer_wq[l], peer_subkeys[l], peer_u[l], peer_v[l])
    return _rmsnorm(x, norm_final)
```

```python
import functools
import math

import jax
import jax.numpy as jnp
import numpy as np
from jax import lax
from jax.experimental import pallas as pl
from jax.experimental.pallas import tpu as pltpu

F32 = jnp.float32
BF16 = jnp.bfloat16

EPS = 1e-6
N_MOD = 6
SSM_GROUP = 16
SSM_STATE = 64
HEAD_DIM = 128
ATTN_BLOCK = 128
DILATED_PATTERNS = ((128, 1), (512, 4), (2048, 16))
PEER_HEADS = 8
PEER_NKEYS = 128
PEER_TOPK = 16

LANES = 128
SUBLANES = 8
S5_CHUNK = 256
S5_LANE_BLOCK = 512
S5_CH_BLOCK = 128
VMEM_LIMIT = 56 * 1024 * 1024


def _cparams(*sem, vmem=VMEM_LIMIT):
    return pltpu.CompilerParams(dimension_semantics=sem, vmem_limit_bytes=vmem)


def _gelu(x):
    return 0.5 * x * (1.0 + lax.erf(x * np.float32(math.sqrt(0.5))))


def _split3(x):
    hi = x.astype(BF16)
    r1 = x - hi.astype(F32)
    mid = r1.astype(BF16)
    lo = (r1 - mid.astype(F32)).astype(BF16)
    return hi, mid, lo


def _mod_kernel(c_ref, w_ref, b_ref, o_ref):
    c = c_ref[...]
    ca = c * jax.nn.sigmoid(c)
    w = w_ref[0]
    c_hi = ca.astype(BF16)
    c_lo = (ca - c_hi.astype(F32)).astype(BF16)
    w_hi = w.astype(BF16)
    w_lo = (w - w_hi.astype(F32)).astype(BF16)
    acc = jnp.dot(c_hi, w_hi, preferred_element_type=F32)
    acc += jnp.dot(c_lo, w_hi, preferred_element_type=F32)
    acc += jnp.dot(c_hi, w_lo, preferred_element_type=F32)
    o_ref[0] = acc + b_ref[0]


def _modulation(c_pad, w_ada, b_ada, tn=1024):
    depth, d, cols = w_ada.shape
    return pl.pallas_call(
        _mod_kernel,
        grid=(depth, cols // tn),
        in_specs=[
            pl.BlockSpec((SUBLANES, d), lambda l, j: (0, 0)),
            pl.BlockSpec((1, d, tn), lambda l, j: (l, 0, j)),
            pl.BlockSpec((1, 1, tn), lambda l, j: (l, 0, j)),
        ],
        out_specs=pl.BlockSpec((1, SUBLANES, tn), lambda l, j: (l, 0, j)),
        out_shape=jax.ShapeDtypeStruct((depth, SUBLANES, cols), F32),
        compiler_params=_cparams("arbitrary", "arbitrary"),
        name="adaln_mod",
    )(c_pad, w_ada, b_ada.reshape(depth, 1, cols))


def _norm_mod_kernel(x_ref, gain_ref, sc_ref, sh_ref, o_ref, *, transpose):
    x = x_ref[...]
    ms = jnp.mean(x * x, axis=-1, keepdims=True)
    y = x * lax.rsqrt(ms + EPS) * gain_ref[...]
    h = y * (1.0 + sc_ref[0]) + sh_ref[0]
    if transpose:
        o_ref[...] = h.T.astype(o_ref.dtype)
    else:
        o_ref[...] = h.astype(o_ref.dtype)


def _norm_mod(x2d, gain, sc, sh, seq, *, transpose, tm=512):
    n, d = x2d.shape
    per_batch = seq // tm
    if transpose:
        out_spec = pl.BlockSpec((d, tm), lambda i: (0, i))
        out_shape = jax.ShapeDtypeStruct((d, n), BF16)
    else:
        out_spec = pl.BlockSpec((tm, d), lambda i: (i, 0))
        out_shape = jax.ShapeDtypeStruct((n, d), BF16)
    return pl.pallas_call(
        functools.partial(_norm_mod_kernel, transpose=transpose),
        grid=(n // tm,),
        in_specs=[
            pl.BlockSpec((tm, d), lambda i: (i, 0)),
            pl.BlockSpec((1, d), lambda i: (0, 0)),
            pl.BlockSpec((1, 1, d), lambda i: (i // per_batch, 0, 0)),
            pl.BlockSpec((1, 1, d), lambda i: (i // per_batch, 0, 0)),
        ],
        out_specs=out_spec,
        out_shape=out_shape,
        compiler_params=_cparams("arbitrary"),
        name="norm_mod_t" if transpose else "norm_mod",
    )(x2d, gain.reshape(1, d), sc, sh)


def _final_norm_kernel(x_ref, gain_ref, o_ref):
    x = x_ref[...]
    ms = jnp.mean(x * x, axis=-1, keepdims=True)
    o_ref[...] = x * lax.rsqrt(ms + EPS) * gain_ref[...]


def _final_norm(x2d, gain, tm=512):
    n, d = x2d.shape
    return pl.pallas_call(
        _final_norm_kernel,
        grid=(n // tm,),
        in_specs=[pl.BlockSpec((tm, d), lambda i: (i, 0)),
                  pl.BlockSpec((1, d), lambda i: (0, 0))],
        out_specs=pl.BlockSpec((tm, d), lambda i: (i, 0)),
        out_shape=jax.ShapeDtypeStruct((n, d), F32),
        compiler_params=_cparams("arbitrary"),
        name="final_norm",
    )(x2d, gain.reshape(1, d))


def _mm_kernel(a_ref, w_ref, o_ref, *, act):
    acc = jnp.dot(a_ref[...], w_ref[...], preferred_element_type=F32)
    if act == "sigmoid":
        acc = jax.nn.sigmoid(acc)
    o_ref[...] = acc.astype(o_ref.dtype)


def _matmul(a, w, *, col0, ncols, out_dtype, act=None, tm, tn, name):
    m, k = a.shape
    off = col0 // tn
    return pl.pallas_call(
        functools.partial(_mm_kernel, act=act),
        grid=(m // tm, ncols // tn),
        in_specs=[pl.BlockSpec((tm, k), lambda i, j: (i, 0)),
                  pl.BlockSpec((k, tn), lambda i, j: (0, j + off))],
        out_specs=pl.BlockSpec((tm, tn), lambda i, j: (i, j)),
        out_shape=jax.ShapeDtypeStruct((m, ncols), out_dtype),
        compiler_params=_cparams("arbitrary", "arbitrary"),
        name=name,
    )(a, w)


def _merge_kernel(ys_ref, ya_ref, ws_ref, wa_ref, gs_ref, ga_ref, o_ref):
    ps = jnp.dot(ys_ref[...], ws_ref[...], preferred_element_type=F32)
    pa = jnp.dot(ya_ref[...], wa_ref[...], preferred_element_type=F32)
    o_ref[...] = (gs_ref[...].astype(F32) * ps
                  + ga_ref[...].astype(F32) * pa).astype(o_ref.dtype)


def _merge(ys, ya, ws, wa, gates, tm=1024, tn=1024):
    n, ks = ys.shape
    ka = ya.shape[1]
    d = ws.shape[1]
    goff = d // tn
    return pl.pallas_call(
        _merge_kernel,
        grid=(n // tm, d // tn),
        in_specs=[
            pl.BlockSpec((tm, ks), lambda i, j: (i, 0)),
            pl.BlockSpec((tm, ka), lambda i, j: (i, 0)),
            pl.BlockSpec((ks, tn), lambda i, j: (0, j)),
            pl.BlockSpec((ka, tn), lambda i, j: (0, j)),
            pl.BlockSpec((tm, tn), lambda i, j: (i, j)),
            pl.BlockSpec((tm, tn), lambda i, j: (i, j + goff)),
        ],
        out_specs=pl.BlockSpec((tm, tn), lambda i, j: (i, j)),
        out_shape=jax.ShapeDtypeStruct((n, d), BF16),
        compiler_params=_cparams("arbitrary", "arbitrary"),
        name="branch_merge",
    )(ys, ya, ws, wa, gates, gates)


def _outproj_kernel(a_ref, w_ref, x_ref, g_ref, o_ref):
    acc = jnp.dot(a_ref[...], w_ref[...], preferred_element_type=F32)
    o_ref[...] = x_ref[...] + g_ref[0] * acc


def _outproj(a, w, x2d, g, seq, tm=1024, tn=1024):
    n, k = a.shape
    d = w.shape[1]
    per_batch = seq // tm
    return pl.pallas_call(
        _outproj_kernel,
        grid=(n // tm, d // tn),
        in_specs=[
            pl.BlockSpec((tm, k), lambda i, j: (i, 0)),
            pl.BlockSpec((k, tn), lambda i, j: (0, j)),
            pl.BlockSpec((tm, tn), lambda i, j: (i, j)),
            pl.BlockSpec((1, 1, tn), lambda i, j: (i // per_batch, 0, j)),
        ],
        out_specs=pl.BlockSpec((tm, tn), lambda i, j: (i, j)),
        out_shape=jax.ShapeDtypeStruct((n, d), F32),
        compiler_params=_cparams("arbitrary", "arbitrary"),
        name="out_proj",
    )(a, w, x2d, g)


def _s5_prep_kernel(lr_ref, li_ref, ldt_ref, br_ref, bi_ref,
                    are_ref, aim_ref, bbr_ref, bbi_ref, pwr_ref, pwi_ref, *, ts):
    lr, li = lr_ref[...], li_ref[...]
    dt = jnp.exp(ldt_ref[...])
    mag = jnp.exp(lr * dt)
    a_re = mag * jnp.cos(li * dt)
    a_im = mag * jnp.sin(li * dt)
    inv = 1.0 / (lr * lr + li * li)
    coef_re = ((a_re - 1.0) * lr + a_im * li) * inv
    coef_im = (a_im * lr - (a_re - 1.0) * li) * inv
    br, bi = br_ref[...], bi_ref[...]
    bbr_ref[...] = coef_re * br - coef_im * bi
    bbi_ref[...] = coef_re * bi + coef_im * br
    are_ref[...] = a_re
    aim_ref[...] = a_im
    pr, pi = a_re, a_im
    for j in range(ts):
        pwr_ref[j:j + 1, :] = pr
        pwi_ref[j:j + 1, :] = pi
        pr, pi = pr * a_re - pi * a_im, pr * a_im + pi * a_re


def _s5_prep(lam_re, lam_im, log_dt, b_re, b_im, ts):
    g, p, c = b_re.shape
    ns = g * p
    row = lambda t: t.reshape(1, ns)
    ldt = jnp.broadcast_to(log_dt[:, None], (g, p))
    bt = lambda t: jnp.transpose(t, (2, 0, 1)).reshape(c, ns)
    shapes = (
        jax.ShapeDtypeStruct((1, ns), F32), jax.ShapeDtypeStruct((1, ns), F32),
        jax.ShapeDtypeStruct((c, ns), F32), jax.ShapeDtypeStruct((c, ns), F32),
        jax.ShapeDtypeStruct((ts, ns), F32), jax.ShapeDtypeStruct((ts, ns), F32),
    )
    return pl.pallas_call(
        functools.partial(_s5_prep_kernel, ts=ts),
        out_shape=shapes,
        name="s5_prep",
    )(row(lam_re), row(lam_im), row(ldt), bt(b_re), bt(b_im))


def _s5_block_weights(bbar_re, bbar_im, c_re, c_im):
    c, ns = bbar_re.shape
    gpb = S5_CH_BLOCK // c
    nk = ns // (gpb * SSM_STATE)
    eye = jnp.eye(gpb, dtype=F32)

    def bpart(t):
        t = t.reshape(c, nk, gpb, SSM_STATE)
        t = jnp.einsum("ckjp,ij->kicjp", t, eye)
        return t.reshape(nk, gpb * c, gpb * SSM_STATE)

    def cpart(t):
        t = t.reshape(nk, gpb, c, SSM_STATE)
        t = jnp.einsum("kicp,ij->kjpic", t, eye)
        return t.reshape(nk, gpb * SSM_STATE, gpb * c)

    bblk = jnp.concatenate([bpart(bbar_re), bpart(bbar_im)], axis=2).astype(BF16)
    cblk = jnp.concatenate([cpart(c_re), cpart(-c_im)], axis=1).astype(BF16)
    return bblk, cblk


def _s5_kernel(u_ref, pm_ref, pmt_ref, bblk_ref, are_ref, aim_ref, pwr_ref, pwi_ref,
               cblk_ref, d_ref, wglu_ref, o_ref,
               hr, hi, car_r, car_i, cs_r, cs_i, yp, *, ts, nk):
    lw = S5_LANE_BLOCK
    cw = S5_CH_BLOCK
    t_rows = SUBLANES * ts

    @pl.when(pl.program_id(1) == 0)
    def _():
        car_r[...] = jnp.zeros_like(car_r)
        car_i[...] = jnp.zeros_like(car_i)

    u = u_ref[...]
    up = jnp.dot(pm_ref[...], u.astype(BF16), preferred_element_type=F32).astype(BF16)
    for k in range(nk):
        bu = jnp.dot(up[:, k * cw:(k + 1) * cw], bblk_ref[k], preferred_element_type=F32)
        hr[:, k * lw:(k + 1) * lw] = bu[:, :lw]
        hi[:, k * lw:(k + 1) * lw] = bu[:, lw:]

    for k in range(nk):
        sl = pl.ds(k * lw, lw)
        ar = jnp.broadcast_to(are_ref[:, sl], (SUBLANES, lw))
        ai = jnp.broadcast_to(aim_ref[:, sl], (SUBLANES, lw))

        def local_step(j, carry, sl=sl, ar=ar, ai=ai):
            cr, cim = carry
            rows = pl.ds(pl.multiple_of(j * SUBLANES, SUBLANES), SUBLANES)
            nr = ar * cr - ai * cim + hr[rows, sl]
            ni = ar * cim + ai * cr + hi[rows, sl]
            hr[rows, sl] = nr
            hi[rows, sl] = ni
            return nr, ni

        zero = jnp.zeros((SUBLANES, lw), F32)
        lax.fori_loop(0, ts, local_step, (zero, zero))

    c_r = car_r[...]
    c_i = car_i[...]
    at_r = pwr_ref[ts - 1:ts, :]
    at_i = pwi_ref[ts - 1:ts, :]
    for s in range(SUBLANES):
        cs_r[s:s + 1, :] = c_r
        cs_i[s:s + 1, :] = c_i
        row = t_rows - SUBLANES + s
        l_r = hr[row:row + 1, :]
        l_i = hi[row:row + 1, :]
        c_r, c_i = l_r + at_r * c_r - at_i * c_i, l_i + at_r * c_i + at_i * c_r
    car_r[...] = c_r
    car_i[...] = c_i

    for k in range(nk):
        sl = pl.ds(k * lw, lw)
        csr = cs_r[:, sl]
        csi = cs_i[:, sl]

        def fix_step(j, carry, sl=sl, csr=csr, csi=csi):
            rows = pl.ds(pl.multiple_of(j * SUBLANES, SUBLANES), SUBLANES)
            pr = pwr_ref[pl.ds(j, 1), sl]
            pi = pwi_ref[pl.ds(j, 1), sl]
            hr[rows, sl] = hr[rows, sl] + (pr * csr - pi * csi)
            hi[rows, sl] = hi[rows, sl] + (pr * csi + pi * csr)
            return carry

        lax.fori_loop(0, ts, fix_step, 0)
        hcat = jnp.concatenate([hr[:, sl], hi[:, sl]], axis=1).astype(BF16)
        yp[:, k * cw:(k + 1) * cw] = jnp.dot(hcat, cblk_ref[k], preferred_element_type=F32)

    y1, y2, y3 = _split3(yp[...])
    pmt = pmt_ref[...]
    y = (jnp.dot(pmt, y1, preferred_element_type=F32)
         + jnp.dot(pmt, y2, preferred_element_type=F32)
         + jnp.dot(pmt, y3, preferred_element_type=F32))
    y = y + d_ref[...] * u
    y = _gelu(y)
    gate = jax.nn.sigmoid(jnp.dot(y.astype(BF16), wglu_ref[...], preferred_element_type=F32))
    o_ref[...] = (y * gate).astype(o_ref.dtype)


def _s5_permutation(t_rows, ts):
    r = np.arange(t_rows)
    src = (r % SUBLANES) * ts + r // SUBLANES
    pm = np.zeros((t_rows, t_rows), np.float32)
    pm[r, src] = 1.0
    return jnp.asarray(pm, BF16), jnp.asarray(pm.T, BF16)


def _s5_branch(u, seq, a_re, a_im, pw_re, pw_im, bblk, cblk, d_skip, w_glu):
    n, w = u.shape
    ns = a_re.shape[1]
    t_rows = S5_CHUNK
    ts = t_rows // SUBLANES
    nk = ns // S5_LANE_BLOCK
    nchunk = seq // t_rows
    pm, pmt = _s5_permutation(t_rows, ts)
    full = lambda shape: pl.BlockSpec(shape, lambda b, c: (0,) * len(shape))
    return pl.pallas_call(
        functools.partial(_s5_kernel, ts=ts, nk=nk),
        grid=(n // seq, nchunk),
        in_specs=[
            pl.BlockSpec((t_rows, w), lambda b, c: (b * nchunk + c, 0)),
            full((t_rows, t_rows)), full((t_rows, t_rows)),
            full(bblk.shape),
            full((1, ns)), full((1, ns)), full((ts, ns)), full((ts, ns)),
            full(cblk.shape),
            full((1, w)), full((w, w)),
        ],
        out_specs=pl.BlockSpec((t_rows, w), lambda b, c: (b * nchunk + c, 0)),
        out_shape=jax.ShapeDtypeStruct((n, w), BF16),
        scratch_shapes=[
            pltpu.VMEM((t_rows, ns), F32), pltpu.VMEM((t_rows, ns), F32),
            pltpu.VMEM((1, ns), F32), pltpu.VMEM((1, ns), F32),
            pltpu.VMEM((SUBLANES, ns), F32), pltpu.VMEM((SUBLANES, ns), F32),
            pltpu.VMEM((t_rows, w), F32),
        ],
        compiler_params=_cparams("arbitrary", "arbitrary"),
        name="s5_branch",
    )(u, pm, pmt, bblk, a_re, a_im, pw_re, pw_im, cblk, d_skip.reshape(1, w), w_glu)


def _attn_kernel(q_ref, k_ref, v_ref, o_ref, m_s, l_s, acc_s, *, seq, patterns, scale):
    blk = ATTN_BLOCK
    m_s[...] = jnp.full(m_s.shape, -jnp.inf, F32)
    l_s[...] = jnp.zeros_like(l_s)
    acc_s[...] = jnp.zeros_like(acc_s)
    qi = lax.broadcasted_iota(jnp.int32, (blk, blk), 0)
    kj = lax.broadcasted_iota(jnp.int32, (blk, blk), 1)
    nt = (((1,), (1,)), ((), ()))

    for window, dil in patterns:
        n_back = window // dil
        nblk = seq // dil // blk
        mask_cur = (qi - kj >= 0) & (qi - kj <= n_back)
        mask_prev = (blk + qi - kj) <= n_back

        def block_step(idx, carry, dil=dil, nblk=nblk, mask_cur=mask_cur, mask_prev=mask_prev):
            r = idx // nblk
            nb = idx % nblk
            start = r + nb * (blk * dil)
            pstart = jnp.maximum(start - blk * dil, r)
            if dil == 1:
                rows, prows = pl.ds(start, blk), pl.ds(pstart, blk)
            else:
                rows = pl.ds(start, blk, stride=dil)
                prows = pl.ds(pstart, blk, stride=dil)
            q = q_ref[rows, :].astype(BF16)
            kc = k_ref[rows, :].astype(BF16)
            vc = v_ref[rows, :].astype(BF16)
            kp = k_ref[prows, :].astype(BF16)
            vp = v_ref[prows, :].astype(BF16)
            sc = lax.dot_general(q, kc, nt, preferred_element_type=F32) * scale
            sp = lax.dot_general(q, kp, nt, preferred_element_type=F32) * scale
            sc = jnp.where(mask_cur, sc, -jnp.inf)
            sp = jnp.where(mask_prev & (nb > 0), sp, -jnp.inf)
            mb = jnp.maximum(jnp.max(sc, axis=1, keepdims=True),
                             jnp.max(sp, axis=1, keepdims=True))
            m_old = m_s[rows, :]
            m_new = jnp.maximum(m_old, mb)
            pc = jnp.exp(sc - m_new)
            pp = jnp.exp(sp - m_new)
            alpha = jnp.exp(m_old - m_new)
            l_s[rows, :] = (alpha * l_s[rows, :] + jnp.sum(pc, axis=1, keepdims=True)
                            + jnp.sum(pp, axis=1, keepdims=True))
            acc_s[rows, :] = (alpha * acc_s[rows, :]
                              + jnp.dot(pc.astype(BF16), vc, preferred_element_type=F32)
                              + jnp.dot(pp.astype(BF16), vp, preferred_element_type=F32))
            m_s[rows, :] = m_new
            return carry

        lax.fori_loop(0, dil * nblk, block_step, 0)

    o_ref[...] = (acc_s[...] / l_s[...]).astype(o_ref.dtype)


def _dilated_attention(qkv, seq):
    n = qkv.shape[0]
    nh = qkv.shape[1] // (3 * HEAD_DIM)
    for window, dil in DILATED_PATTERNS:
        assert window // dil <= ATTN_BLOCK and seq % (dil * ATTN_BLOCK) == 0
    blk = (seq, HEAD_DIM)
    return pl.pallas_call(
        functools.partial(_attn_kernel, seq=seq, patterns=DILATED_PATTERNS,
                          scale=np.float32(HEAD_DIM ** -0.5)),
        grid=(n // seq, nh),
        in_specs=[pl.BlockSpec(blk, lambda b, h: (b, h)),
                  pl.BlockSpec(blk, lambda b, h: (b, nh + h)),
                  pl.BlockSpec(blk, lambda b, h: (b, 2 * nh + h))],
        out_specs=pl.BlockSpec(blk, lambda b, h: (b, h)),
        out_shape=jax.ShapeDtypeStruct((n, nh * HEAD_DIM), BF16),
        scratch_shapes=[pltpu.VMEM(blk, F32), pltpu.VMEM(blk, F32), pltpu.VMEM(blk, F32)],
        compiler_params=_cparams("arbitrary", "arbitrary"),
        name="dilated_attn",
    )(qkv, qkv, qkv)


def _top16(s, kid):
    nkeys = s.shape[0]
    rank = jnp.full(s.shape, float(nkeys), F32)
    vals = []
    for it in range(PEER_TOPK):
        m = jnp.max(s, axis=0, keepdims=True)
        idx = jnp.min(jnp.where(s == m, kid, float(nkeys)), axis=0, keepdims=True)
        sel = kid == idx
        rank = jnp.where(sel, float(it), rank)
        s = jnp.where(sel, -jnp.inf, s)
        vals.append(m)
    return rank, jnp.concatenate(vals, axis=0)


def _router_kernel(q_ref, sk_ref, p1_ref, n_ref, r2_ref, p2_ref, *, nsub):
    nk = PEER_NKEYS
    kid = lax.broadcasted_iota(jnp.int32, (nk, LANES), 0).astype(F32)
    ncand = PEER_TOPK * PEER_TOPK
    cid = lax.broadcasted_iota(jnp.int32, (ncand, LANES), 0).astype(F32)

    def sub(st, carry):
        lanes = pl.ds(pl.multiple_of(st * LANES, LANES), LANES)
        q1 = q_ref[0:nk, lanes].astype(BF16)
        q2 = q_ref[nk:2 * nk, lanes].astype(BF16)
        s1 = jnp.dot(sk_ref[0], q1, preferred_element_type=F32)
        s2 = jnp.dot(sk_ref[1], q2, preferred_element_type=F32)
        rank1, v1 = _top16(s1, kid)
        rank2, v2 = _top16(s2, kid)
        work = jnp.concatenate([v1[a:a + 1] + v2 for a in range(PEER_TOPK)], axis=0)
        m0 = v1[0:1] + v2[0:1]
        z = jnp.zeros((1, LANES), F32)
        nkey = jnp.zeros((nk, LANES), F32)
        for _ in range(PEER_TOPK):
            m = jnp.max(work, axis=0, keepdims=True)
            idx = jnp.min(jnp.where(work == m, cid, float(ncand)), axis=0, keepdims=True)
            work = jnp.where(cid == idx, -jnp.inf, work)
            a_sel = jnp.floor(idx * (1.0 / PEER_TOPK))
            nkey = nkey + jnp.where(rank1 == a_sel, 1.0, 0.0)
            z = z + jnp.exp(m - m0)
        p1_ref[0, :, lanes] = jnp.exp(s1 - v1[0:1])
        n_ref[0, :, lanes] = nkey
        r2_ref[0, :, lanes] = rank2
        p2_ref[0, :, lanes] = jnp.exp(s2 - v2[0:1]) / z
        return carry

    lax.fori_loop(0, nsub, sub, 0)


def _router(qt, sk, tt=512):
    dq2, n = qt.shape
    nh = PEER_HEADS
    nk = PEER_NKEYS
    out = jax.ShapeDtypeStruct((nh, nk, n), F32)
    ospec = pl.BlockSpec((1, nk, tt), lambda i, h: (h, 0, i))
    return pl.pallas_call(
        functools.partial(_router_kernel, nsub=tt // LANES),
        grid=(n // tt, nh),
        in_specs=[pl.BlockSpec((2 * nk, tt), lambda i, h: (h, i)),
                  pl.BlockSpec((2, nk, nk), lambda i, h: (0, 0, 0))],
        out_specs=(ospec, ospec, ospec, ospec),
        out_shape=(out, out, out, out),
        compiler_params=_cparams("arbitrary", "arbitrary"),
        name="peer_router",
    )(qt, sk)


def _peer_kernel(ht_ref, u_ref, vt_ref, p1_ref, n_ref, r2_ref, p2_ref, x_ref, g_ref, o_ref,
                 acc, a_s, gbuf, *, ne1, nh):
    j = pl.program_id(1)
    nk = PEER_NKEYS

    @pl.when(j == 0)
    def _():
        acc[...] = jnp.zeros_like(acc)

    a_s[...] = jnp.dot(u_ref[...], ht_ref[...], preferred_element_type=F32)

    def per_e1(e, carry):
        rows = pl.ds(pl.multiple_of(e * nk, nk), nk)
        act = _gelu(a_s[rows, :])
        w = jnp.zeros(act.shape, F32)
        for h in range(nh):
            p1 = p1_ref[h, pl.ds(e, 1), :]
            cnt = n_ref[h, pl.ds(e, 1), :]
            w = w + p1 * jnp.where(r2_ref[h] < cnt, p2_ref[h], 0.0)
        gbuf[rows, :] = (act * w).astype(BF16)
        return carry

    lax.fori_loop(0, ne1, per_e1, 0)
    acc[...] += jnp.dot(vt_ref[...], gbuf[...], preferred_element_type=F32)

    @pl.when(j == pl.num_programs(1) - 1)
    def _():
        o_ref[...] = x_ref[...] + g_ref[0] * acc[...].T


def _peer(ht, u_tab, vt_tab, p1, cnt, r2, p2, x2d, g, seq, tt=512, eb=1024):
    d, n = ht.shape
    ne = u_tab.shape[0]
    nh, nk, _ = p1.shape
    ne1 = eb // nk
    per_batch = seq // tt
    tok = pl.BlockSpec((nh, nk, tt), lambda i, j: (0, 0, i))
    e1b = pl.BlockSpec((nh, ne1, tt), lambda i, j: (0, j, i))
    return pl.pallas_call(
        functools.partial(_peer_kernel, ne1=ne1, nh=nh),
        grid=(n // tt, ne // eb),
        in_specs=[
            pl.BlockSpec((d, tt), lambda i, j: (0, i)),
            pl.BlockSpec((eb, d), lambda i, j: (j, 0)),
            pl.BlockSpec((d, eb), lambda i, j: (0, j)),
            e1b, e1b, tok, tok,
            pl.BlockSpec((tt, d), lambda i, j: (i, 0)),
            pl.BlockSpec((1, 1, d), lambda i, j: (i // per_batch, 0, 0)),
        ],
        out_specs=pl.BlockSpec((tt, d), lambda i, j: (i, 0)),
        out_shape=jax.ShapeDtypeStruct((n, d), F32),
        scratch_shapes=[pltpu.VMEM((d, tt), F32), pltpu.VMEM((eb, tt), F32),
                        pltpu.VMEM((eb, tt), BF16)],
        compiler_params=_cparams("arbitrary", "arbitrary", vmem=60 * 1024 * 1024),
        name="peer_experts",
    )(ht, u_tab, vt_tab, p1, cnt, r2, p2, x2d, g)


def kernel(x, c, w_ada, b_ada, norm_mix, norm_ffn, w_in, ssm_lambda_re, ssm_lambda_im, ssm_log_dt, ssm_b_re, ssm_b_im, ssm_c_re, ssm_c_im, ssm_d, w_glu, w_br_ssm, w_br_attn, w_out, peer_wq, peer_subkeys, peer_u, peer_v, norm_final):
    bsz, seq, d = x.shape
    depth = w_ada.shape[0]
    n = bsz * seq
    ssm_w = ssm_d.shape[1]
    attn_w = w_br_attn.shape[1]
    assert bsz <= SUBLANES and seq % S5_CHUNK == 0

    c_pad = jnp.pad(c, ((0, SUBLANES - bsz), (0, 0)))
    mod_all = _modulation(c_pad, w_ada, b_ada)
    x2d = x.reshape(n, d)
    ts = S5_CHUNK // SUBLANES

    for l in range(depth):
        mods = [mod_all[l, :bsz, i * d:(i + 1) * d].reshape(bsz, 1, d) for i in range(N_MOD)]
        sh_m, sc_m, g_m, sh_f, sc_f, g_f = mods

        h = _norm_mod(x2d, norm_mix[l], sc_m, sh_m, seq, transpose=False)
        w_in_b = w_in[l].astype(BF16)
        u = _matmul(h, w_in_b, col0=0, ncols=ssm_w, out_dtype=F32, tm=1024, tn=1024, name="in_proj_u")
        qkv = _matmul(h, w_in_b, col0=ssm_w, ncols=3 * attn_w, out_dtype=F32, tm=1024, tn=1024,
                      name="in_proj_qkv")
        gates = _matmul(h, w_in_b, col0=ssm_w + 3 * attn_w, ncols=2 * d, out_dtype=BF16,
                        act="sigmoid", tm=1024, tn=1024, name="in_proj_gates")

        a_re, a_im, bbar_re, bbar_im, pw_re, pw_im = _s5_prep(
            ssm_lambda_re[l], ssm_lambda_im[l], ssm_log_dt[l], ssm_b_re[l], ssm_b_im[l], ts)
        bblk, cblk = _s5_block_weights(bbar_re, bbar_im, ssm_c_re[l], ssm_c_im[l])
        y_s = _s5_branch(u, seq, a_re, a_im, pw_re, pw_im, bblk, cblk, ssm_d[l],
                         w_glu[l].astype(BF16))
        y_a = _dilated_attention(qkv, seq)
        merged = _merge(y_s, y_a, w_br_ssm[l].astype(BF16), w_br_attn[l].astype(BF16), gates)
        x2d = _outproj(merged, w_out[l].astype(BF16), x2d, g_m, seq)

        ht = _norm_mod(x2d, norm_ffn[l], sc_f, sh_f, seq, transpose=True)
        qt = _matmul(peer_wq[l].T.astype(BF16), ht, col0=0, ncols=n, out_dtype=F32,
                     tm=1024, tn=1024, name="peer_query")
        p1, cnt, r2, p2 = _router(qt, peer_subkeys[l].astype(BF16))
        x2d = _peer(ht, peer_u[l].astype(BF16), peer_v[l].T.astype(BF16), p1, cnt, r2, p2,
                    x2d, g_f, seq)

    return _final_norm(x2d, norm_final).reshape(bsz, seq, d)
```

```python
import functools
import math

import jax
import jax.numpy as jnp
import numpy as np
from jax import lax
from jax.experimental import pallas as pl
from jax.experimental.pallas import tpu as pltpu

F32 = jnp.float32
BF16 = jnp.bfloat16

EPS = 1e-6
N_MOD = 6
SSM_GROUP = 16
SSM_STATE = 64
HEAD_DIM = 128
ATTN_BLOCK = 128
ATTN_GROUP = 4
DILATED_PATTERNS = ((128, 1), (512, 4), (2048, 16))
PEER_HEADS = 8
PEER_NKEYS = 128
PEER_TOPK = 16

LANES = 128
SUBLANES = 8
S5_CHUNK = 256
S5_LANE_BLOCK = 512
S5_CH_BLOCK = 128
VMEM_LIMIT = 56 * 1024 * 1024


def _cparams(*sem, vmem=VMEM_LIMIT):
    return pltpu.CompilerParams(dimension_semantics=sem, vmem_limit_bytes=vmem)


def _gelu(x):
    return 0.5 * x * (1.0 + lax.erf(x * np.float32(math.sqrt(0.5))))


def _split3(x):
    hi = x.astype(BF16)
    r1 = x - hi.astype(F32)
    mid = r1.astype(BF16)
    lo = (r1 - mid.astype(F32)).astype(BF16)
    return hi, mid, lo


def _mod_kernel(c_ref, w_ref, b_ref, o_ref):
    c = c_ref[...]
    ca = c * jax.nn.sigmoid(c)
    w = w_ref[0]
    c_hi = ca.astype(BF16)
    c_lo = (ca - c_hi.astype(F32)).astype(BF16)
    w_hi = w.astype(BF16)
    w_lo = (w - w_hi.astype(F32)).astype(BF16)
    acc = jnp.dot(c_hi, w_hi, preferred_element_type=F32)
    acc += jnp.dot(c_lo, w_hi, preferred_element_type=F32)
    acc += jnp.dot(c_hi, w_lo, preferred_element_type=F32)
    o_ref[0] = acc + b_ref[0]


def _modulation(c_pad, w_ada, b_ada, tn=1024):
    depth, d, cols = w_ada.shape
    return pl.pallas_call(
        _mod_kernel,
        grid=(depth, cols // tn),
        in_specs=[
            pl.BlockSpec((SUBLANES, d), lambda l, j: (0, 0)),
            pl.BlockSpec((1, d, tn), lambda l, j: (l, 0, j)),
            pl.BlockSpec((1, 1, tn), lambda l, j: (l, 0, j)),
        ],
        out_specs=pl.BlockSpec((1, SUBLANES, tn), lambda l, j: (l, 0, j)),
        out_shape=jax.ShapeDtypeStruct((depth, SUBLANES, cols), F32),
        compiler_params=_cparams("arbitrary", "arbitrary"),
        name="adaln_mod",
    )(c_pad, w_ada, b_ada.reshape(depth, 1, cols))


def _norm_mod_kernel(x_ref, gain_ref, sc_ref, sh_ref, o_ref, *, transpose):
    x = x_ref[...]
    ms = jnp.mean(x * x, axis=-1, keepdims=True)
    y = x * lax.rsqrt(ms + EPS) * gain_ref[...]
    h = y * (1.0 + sc_ref[0]) + sh_ref[0]
    if transpose:
        o_ref[...] = h.T.astype(o_ref.dtype)
    else:
        o_ref[...] = h.astype(o_ref.dtype)


def _norm_mod(x2d, gain, sc, sh, seq, *, transpose, tm=512):
    n, d = x2d.shape
    per_batch = seq // tm
    if transpose:
        out_spec = pl.BlockSpec((d, tm), lambda i: (0, i))
        out_shape = jax.ShapeDtypeStruct((d, n), BF16)
    else:
        out_spec = pl.BlockSpec((tm, d), lambda i: (i, 0))
        out_shape = jax.ShapeDtypeStruct((n, d), BF16)
    return pl.pallas_call(
        functools.partial(_norm_mod_kernel, transpose=transpose),
        grid=(n // tm,),
        in_specs=[
            pl.BlockSpec((tm, d), lambda i: (i, 0)),
            pl.BlockSpec((1, d), lambda i: (0, 0)),
            pl.BlockSpec((1, 1, d), lambda i: (i // per_batch, 0, 0)),
            pl.BlockSpec((1, 1, d), lambda i: (i // per_batch, 0, 0)),
        ],
        out_specs=out_spec,
        out_shape=out_shape,
        compiler_params=_cparams("arbitrary"),
        name="norm_mod_t" if transpose else "norm_mod",
    )(x2d, gain.reshape(1, d), sc, sh)


def _final_norm_kernel(x_ref, gain_ref, o_ref):
    x = x_ref[...]
    ms = jnp.mean(x * x, axis=-1, keepdims=True)
    o_ref[...] = x * lax.rsqrt(ms + EPS) * gain_ref[...]


def _final_norm(x2d, gain, tm=512):
    n, d = x2d.shape
    return pl.pallas_call(
        _final_norm_kernel,
        grid=(n // tm,),
        in_specs=[pl.BlockSpec((tm, d), lambda i: (i, 0)),
                  pl.BlockSpec((1, d), lambda i: (0, 0))],
        out_specs=pl.BlockSpec((tm, d), lambda i: (i, 0)),
        out_shape=jax.ShapeDtypeStruct((n, d), F32),
        compiler_params=_cparams("arbitrary"),
        name="final_norm",
    )(x2d, gain.reshape(1, d))


def _mm_kernel(a_ref, w_ref, o_ref, *, act):
    acc = jnp.dot(a_ref[...], w_ref[...], preferred_element_type=F32)
    if act == "sigmoid":
        acc = jax.nn.sigmoid(acc)
    o_ref[...] = acc.astype(o_ref.dtype)


def _matmul(a, w, *, col0, ncols, out_dtype, act=None, tm, tn, name):
    m, k = a.shape
    off = col0 // tn
    return pl.pallas_call(
        functools.partial(_mm_kernel, act=act),
        grid=(m // tm, ncols // tn),
        in_specs=[pl.BlockSpec((tm, k), lambda i, j: (i, 0)),
                  pl.BlockSpec((k, tn), lambda i, j: (0, j + off))],
        out_specs=pl.BlockSpec((tm, tn), lambda i, j: (i, j)),
        out_shape=jax.ShapeDtypeStruct((m, ncols), out_dtype),
        compiler_params=_cparams("arbitrary", "arbitrary"),
        name=name,
    )(a, w)


def _merge_kernel(ys_ref, ya_ref, ws_ref, wa_ref, gs_ref, ga_ref, o_ref):
    ps = jnp.dot(ys_ref[...], ws_ref[...], preferred_element_type=F32)
    pa = jnp.dot(ya_ref[...], wa_ref[...], preferred_element_type=F32)
    o_ref[...] = (gs_ref[...].astype(F32) * ps
                  + ga_ref[...].astype(F32) * pa).astype(o_ref.dtype)


def _merge(ys, ya, ws, wa, gates, tm=1024, tn=1024):
    n, ks = ys.shape
    ka = ya.shape[1]
    d = ws.shape[1]
    goff = d // tn
    return pl.pallas_call(
        _merge_kernel,
        grid=(n // tm, d // tn),
        in_specs=[
            pl.BlockSpec((tm, ks), lambda i, j: (i, 0)),
            pl.BlockSpec((tm, ka), lambda i, j: (i, 0)),
            pl.BlockSpec((ks, tn), lambda i, j: (0, j)),
            pl.BlockSpec((ka, tn), lambda i, j: (0, j)),
            pl.BlockSpec((tm, tn), lambda i, j: (i, j)),
            pl.BlockSpec((tm, tn), lambda i, j: (i, j + goff)),
        ],
        out_specs=pl.BlockSpec((tm, tn), lambda i, j: (i, j)),
        out_shape=jax.ShapeDtypeStruct((n, d), BF16),
        compiler_params=_cparams("arbitrary", "arbitrary"),
        name="branch_merge",
    )(ys, ya, ws, wa, gates, gates)


def _outproj_kernel(a_ref, w_ref, x_ref, g_ref, o_ref):
    acc = jnp.dot(a_ref[...], w_ref[...], preferred_element_type=F32)
    o_ref[...] = x_ref[...] + g_ref[0] * acc


def _outproj(a, w, x2d, g, seq, tm=1024, tn=1024):
    n, k = a.shape
    d = w.shape[1]
    per_batch = seq // tm
    return pl.pallas_call(
        _outproj_kernel,
        grid=(n // tm, d // tn),
        in_specs=[
            pl.BlockSpec((tm, k), lambda i, j: (i, 0)),
            pl.BlockSpec((k, tn), lambda i, j: (0, j)),
            pl.BlockSpec((tm, tn), lambda i, j: (i, j)),
            pl.BlockSpec((1, 1, tn), lambda i, j: (i // per_batch, 0, j)),
        ],
        out_specs=pl.BlockSpec((tm, tn), lambda i, j: (i, j)),
        out_shape=jax.ShapeDtypeStruct((n, d), F32),
        compiler_params=_cparams("arbitrary", "arbitrary"),
        name="out_proj",
    )(a, w, x2d, g)


def _s5_prep_kernel(lr_ref, li_ref, ldt_ref, br_ref, bi_ref,
                    are_ref, aim_ref, bbr_ref, bbi_ref, pwr_ref, pwi_ref, *, ts):
    lr, li = lr_ref[...], li_ref[...]
    dt = jnp.exp(ldt_ref[...])
    mag = jnp.exp(lr * dt)
    a_re = mag * jnp.cos(li * dt)
    a_im = mag * jnp.sin(li * dt)
    inv = 1.0 / (lr * lr + li * li)
    coef_re = ((a_re - 1.0) * lr + a_im * li) * inv
    coef_im = (a_im * lr - (a_re - 1.0) * li) * inv
    br, bi = br_ref[...], bi_ref[...]
    bbr_ref[...] = coef_re * br - coef_im * bi
    bbi_ref[...] = coef_re * bi + coef_im * br
    are_ref[...] = a_re
    aim_ref[...] = a_im
    pr, pi = a_re, a_im
    for j in range(ts):
        pwr_ref[j:j + 1, :] = pr
        pwi_ref[j:j + 1, :] = pi
        pr, pi = pr * a_re - pi * a_im, pr * a_im + pi * a_re


def _s5_prep(lam_re, lam_im, log_dt, b_re, b_im, ts):
    g, p, c = b_re.shape
    ns = g * p
    row = lambda t: t.reshape(1, ns)
    ldt = jnp.broadcast_to(log_dt[:, None], (g, p))
    bt = lambda t: jnp.transpose(t, (2, 0, 1)).reshape(c, ns)
    shapes = (
        jax.ShapeDtypeStruct((1, ns), F32), jax.ShapeDtypeStruct((1, ns), F32),
        jax.ShapeDtypeStruct((c, ns), F32), jax.ShapeDtypeStruct((c, ns), F32),
        jax.ShapeDtypeStruct((ts, ns), F32), jax.ShapeDtypeStruct((ts, ns), F32),
    )
    return pl.pallas_call(
        functools.partial(_s5_prep_kernel, ts=ts),
        out_shape=shapes,
        name="s5_prep",
    )(row(lam_re), row(lam_im), row(ldt), bt(b_re), bt(b_im))


def _s5_block_weights(bbar_re, bbar_im, c_re, c_im):
    c, ns = bbar_re.shape
    gpb = S5_CH_BLOCK // c
    nk = ns // (gpb * SSM_STATE)
    eye = jnp.eye(gpb, dtype=F32)

    def bpart(t):
        t = t.reshape(c, nk, gpb, SSM_STATE)
        t = jnp.einsum("ckjp,ij->kicjp", t, eye)
        return t.reshape(nk, gpb * c, gpb * SSM_STATE)

    def cpart(t):
        t = t.reshape(nk, gpb, c, SSM_STATE)
        t = jnp.einsum("kicp,ij->kjpic", t, eye)
        return t.reshape(nk, gpb * SSM_STATE, gpb * c)

    bblk = jnp.concatenate([bpart(bbar_re), bpart(bbar_im)], axis=2).astype(BF16)
    cblk = jnp.concatenate([cpart(c_re), cpart(-c_im)], axis=1).astype(BF16)
    return bblk, cblk


def _s5_kernel(u_ref, pm_ref, pmt_ref, bblk_ref, are_ref, aim_ref, pwr_ref, pwi_ref,
               cblk_ref, d_ref, wglu_ref, o_ref,
               hr, hi, car_r, car_i, cs_r, cs_i, yp, *, ts, nk):
    lw = S5_LANE_BLOCK
    cw = S5_CH_BLOCK
    t_rows = SUBLANES * ts

    @pl.when(pl.program_id(1) == 0)
    def _():
        car_r[...] = jnp.zeros_like(car_r)
        car_i[...] = jnp.zeros_like(car_i)

    u = u_ref[...]
    up = jnp.dot(pm_ref[...], u.astype(BF16), preferred_element_type=F32).astype(BF16)
    for k in range(nk):
        bu = jnp.dot(up[:, k * cw:(k + 1) * cw], bblk_ref[k], preferred_element_type=F32)
        hr[:, k * lw:(k + 1) * lw] = bu[:, :lw]
        hi[:, k * lw:(k + 1) * lw] = bu[:, lw:]

    for k in range(nk):
        sl = pl.ds(k * lw, lw)
        ar = jnp.broadcast_to(are_ref[:, sl], (SUBLANES, lw))
        ai = jnp.broadcast_to(aim_ref[:, sl], (SUBLANES, lw))

        def local_step(j, carry, sl=sl, ar=ar, ai=ai):
            cr, cim = carry
            rows = pl.ds(pl.multiple_of(j * SUBLANES, SUBLANES), SUBLANES)
            nr = ar * cr - ai * cim + hr[rows, sl]
            ni = ar * cim + ai * cr + hi[rows, sl]
            hr[rows, sl] = nr
            hi[rows, sl] = ni
            return nr, ni

        zero = jnp.zeros((SUBLANES, lw), F32)
        lax.fori_loop(0, ts, local_step, (zero, zero))

    c_r = car_r[...]
    c_i = car_i[...]
    at_r = pwr_ref[ts - 1:ts, :]
    at_i = pwi_ref[ts - 1:ts, :]
    for s in range(SUBLANES):
        cs_r[s:s + 1, :] = c_r
        cs_i[s:s + 1, :] = c_i
        row = t_rows - SUBLANES + s
        l_r = hr[row:row + 1, :]
        l_i = hi[row:row + 1, :]
        c_r, c_i = l_r + at_r * c_r - at_i * c_i, l_i + at_r * c_i + at_i * c_r
    car_r[...] = c_r
    car_i[...] = c_i

    for k in range(nk):
        sl = pl.ds(k * lw, lw)
        csr = cs_r[:, sl]
        csi = cs_i[:, sl]

        def fix_step(j, carry, sl=sl, csr=csr, csi=csi):
            rows = pl.ds(pl.multiple_of(j * SUBLANES, SUBLANES), SUBLANES)
            pr = pwr_ref[pl.ds(j, 1), sl]
            pi = pwi_ref[pl.ds(j, 1), sl]
            hr[rows, sl] = hr[rows, sl] + (pr * csr - pi * csi)
            hi[rows, sl] = hi[rows, sl] + (pr * csi + pi * csr)
            return carry

        lax.fori_loop(0, ts, fix_step, 0)
        hcat = jnp.concatenate([hr[:, sl], hi[:, sl]], axis=1).astype(BF16)
        yp[:, k * cw:(k + 1) * cw] = jnp.dot(hcat, cblk_ref[k], preferred_element_type=F32)

    y1, y2, y3 = _split3(yp[...])
    pmt = pmt_ref[...]
    y = (jnp.dot(pmt, y1, preferred_element_type=F32)
         + jnp.dot(pmt, y2, preferred_element_type=F32)
         + jnp.dot(pmt, y3, preferred_element_type=F32))
    y = y + d_ref[...] * u
    y = _gelu(y)
    gate = jax.nn.sigmoid(jnp.dot(y.astype(BF16), wglu_ref[...], preferred_element_type=F32))
    o_ref[...] = (y * gate).astype(o_ref.dtype)


def _s5_permutation(t_rows, ts):
    r = np.arange(t_rows)
    src = (r % SUBLANES) * ts + r // SUBLANES
    pm = np.zeros((t_rows, t_rows), np.float32)
    pm[r, src] = 1.0
    return jnp.asarray(pm, BF16), jnp.asarray(pm.T, BF16)


def _s5_branch(u, seq, a_re, a_im, pw_re, pw_im, bblk, cblk, d_skip, w_glu):
    n, w = u.shape
    ns = a_re.shape[1]
    t_rows = S5_CHUNK
    ts = t_rows // SUBLANES
    nk = ns // S5_LANE_BLOCK
    nchunk = seq // t_rows
    pm, pmt = _s5_permutation(t_rows, ts)
    full = lambda shape: pl.BlockSpec(shape, lambda b, c: (0,) * len(shape))
    return pl.pallas_call(
        functools.partial(_s5_kernel, ts=ts, nk=nk),
        grid=(n // seq, nchunk),
        in_specs=[
            pl.BlockSpec((t_rows, w), lambda b, c: (b * nchunk + c, 0)),
            full((t_rows, t_rows)), full((t_rows, t_rows)),
            full(bblk.shape),
            full((1, ns)), full((1, ns)), full((ts, ns)), full((ts, ns)),
            full(cblk.shape),
            full((1, w)), full((w, w)),
        ],
        out_specs=pl.BlockSpec((t_rows, w), lambda b, c: (b * nchunk + c, 0)),
        out_shape=jax.ShapeDtypeStruct((n, w), BF16),
        scratch_shapes=[
            pltpu.VMEM((t_rows, ns), F32), pltpu.VMEM((t_rows, ns), F32),
            pltpu.VMEM((1, ns), F32), pltpu.VMEM((1, ns), F32),
            pltpu.VMEM((SUBLANES, ns), F32), pltpu.VMEM((SUBLANES, ns), F32),
            pltpu.VMEM((t_rows, w), F32),
        ],
        compiler_params=_cparams("arbitrary", "arbitrary"),
        name="s5_branch",
    )(u, pm, pmt, bblk, a_re, a_im, pw_re, pw_im, cblk, d_skip.reshape(1, w), w_glu)


def _attn_kernel(q_ref, k_ref, v_ref, o_ref, o_s, m_s, l_s, *, seq, patterns, scale):
    blk = ATTN_BLOCK
    qi = lax.broadcasted_iota(jnp.int32, (blk, 2 * blk), 0)
    kj = lax.broadcasted_iota(jnp.int32, (blk, 2 * blk), 1)
    dist = blk + qi - kj
    nt = (((1,), (1,)), ((), ()))

    for pi, (window, dil) in enumerate(patterns):
        n_back = window // dil
        nblk = seq // dil // blk
        grp = min(ATTN_GROUP, nblk)
        ngrp = nblk // grp
        band = (dist >= 0) & (dist <= n_back)

        def group_step(idx, carry, pi=pi, dil=dil, grp=grp, ngrp=ngrp, band=band):
            r = idx // ngrp
            n0 = (idx % ngrp) * grp

            def rows_of(nb):
                start = r + nb * (blk * dil)
                return pl.ds(start, blk) if dil == 1 else pl.ds(start, blk, stride=dil)

            rows = [rows_of(jnp.maximum(n0 - 1, 0))] + [rows_of(n0 + gi) for gi in range(grp)]
            ks = [k_ref[rw, :].astype(BF16) for rw in rows]
            scores = []
            for gi in range(grp):
                q = q_ref[rows[gi + 1], :].astype(BF16)
                scores.append(jnp.concatenate(
                    [lax.dot_general(q, ks[gi], nt, preferred_element_type=F32),
                     lax.dot_general(q, ks[gi + 1], nt, preferred_element_type=F32)], axis=1))
            probs = []
            for gi in range(grp):
                mask = band & ((kj >= blk) | (n0 > 0)) if gi == 0 else band
                s = jnp.where(mask, scores[gi] * scale, -jnp.inf)
                m = jnp.max(s, axis=1, keepdims=True)
                p = jnp.exp(s - m)
                l = jnp.sum(p, axis=1, keepdims=True)
                m_s[pi, rows[gi + 1], :] = jnp.broadcast_to(m, (blk, HEAD_DIM))
                l_s[pi, rows[gi + 1], :] = jnp.broadcast_to(l, (blk, HEAD_DIM))
                probs.append(p.astype(BF16))
            vs = [v_ref[rw, :].astype(BF16) for rw in rows]
            for gi in range(grp):
                pb = probs[gi]
                o_s[pi, rows[gi + 1], :] = (
                    jnp.dot(pb[:, :blk], vs[gi], preferred_element_type=F32)
                    + jnp.dot(pb[:, blk:], vs[gi + 1], preferred_element_type=F32))
            return carry

        lax.fori_loop(0, dil * ngrp, group_step, 0)

    npat = len(patterns)

    def merge_step(t, carry):
        rows = pl.ds(pl.multiple_of(t * blk, blk), blk)
        ms = [m_s[pi, rows, :] for pi in range(npat)]
        mx = functools.reduce(jnp.maximum, ms)
        num = jnp.zeros((blk, HEAD_DIM), F32)
        den = jnp.zeros((blk, HEAD_DIM), F32)
        for pi in range(npat):
            w = jnp.exp(ms[pi] - mx)
            num = num + w * o_s[pi, rows, :]
            den = den + w * l_s[pi, rows, :]
        o_ref[rows, :] = (num / den).astype(o_ref.dtype)
        return carry

    lax.fori_loop(0, seq // blk, merge_step, 0)


def _dilated_attention(qkv, seq):
    n = qkv.shape[0]
    nh = qkv.shape[1] // (3 * HEAD_DIM)
    for window, dil in DILATED_PATTERNS:
        assert window // dil <= ATTN_BLOCK and seq % (dil * ATTN_BLOCK) == 0
    blk = (seq, HEAD_DIM)
    return pl.pallas_call(
        functools.partial(_attn_kernel, seq=seq, patterns=DILATED_PATTERNS,
                          scale=np.float32(HEAD_DIM ** -0.5)),
        grid=(n // seq, nh),
        in_specs=[pl.BlockSpec(blk, lambda b, h: (b, h)),
                  pl.BlockSpec(blk, lambda b, h: (b, nh + h)),
                  pl.BlockSpec(blk, lambda b, h: (b, 2 * nh + h))],
        out_specs=pl.BlockSpec(blk, lambda b, h: (b, h)),
        out_shape=jax.ShapeDtypeStruct((n, nh * HEAD_DIM), BF16),
        scratch_shapes=[pltpu.VMEM((len(DILATED_PATTERNS),) + blk, F32)] * 3,
        compiler_params=_cparams("arbitrary", "arbitrary"),
        name="dilated_attn",
    )(qkv, qkv, qkv)


def _top16(s, kid):
    nkeys = s.shape[0]
    rank = jnp.full(s.shape, float(nkeys), F32)
    vals = []
    for it in range(PEER_TOPK):
        m = jnp.max(s, axis=0, keepdims=True)
        idx = jnp.min(jnp.where(s == m, kid, float(nkeys)), axis=0, keepdims=True)
        sel = kid == idx
        rank = jnp.where(sel, float(it), rank)
        s = jnp.where(sel, -jnp.inf, s)
        vals.append(m)
    return rank, jnp.concatenate(vals, axis=0)


def _router_kernel(q_ref, sk_ref, p1_ref, n_ref, r2_ref, p2_ref, *, nsub):
    nk = PEER_NKEYS
    kt = PEER_TOPK
    half = kt // 2
    kid = lax.broadcasted_iota(jnp.int32, (nk, LANES), 0).astype(F32)
    ia = lax.broadcasted_iota(jnp.int32, (kt, LANES), 0).astype(F32)
    ih = ia[0:half]
    cid = jnp.concatenate([ia * kt] + [ih * kt + float(b) for b in range(1, half)] + [ih + float(half)],
                          axis=0)
    ncand = kt * kt

    def sub(st, carry):
        lanes = pl.ds(pl.multiple_of(st * LANES, LANES), LANES)
        q1 = q_ref[0:nk, lanes].astype(BF16)
        q2 = q_ref[nk:2 * nk, lanes].astype(BF16)
        s1 = jnp.dot(sk_ref[0], q1, preferred_element_type=F32)
        s2 = jnp.dot(sk_ref[1], q2, preferred_element_type=F32)
        rank1, v1 = _top16(s1, kid)
        rank2, v2 = _top16(s2, kid)
        work = jnp.concatenate([v1 + v2[0:1]] + [v1[0:half] + v2[b:b + 1] for b in range(1, half)]
                               + [v1[0:1] + v2[half:kt]], axis=0)
        m0 = v1[0:1] + v2[0:1]
        z = jnp.zeros((1, LANES), F32)
        n_a = jnp.zeros((kt, LANES), F32)
        for _ in range(kt):
            m = jnp.max(work, axis=0, keepdims=True)
            idx = jnp.min(jnp.where(work == m, cid, float(ncand)), axis=0, keepdims=True)
            work = jnp.where(cid == idx, -jnp.inf, work)
            a_sel = jnp.floor(idx * (1.0 / kt))
            n_a = n_a + jnp.where(ia == a_sel, 1.0, 0.0)
            z = z + jnp.exp(m - m0)
        nkey = jnp.zeros((nk, LANES), F32)
        for a in range(kt):
            nkey = jnp.where(rank1 == float(a), n_a[a:a + 1], nkey)
        p1_ref[0, :, lanes] = jnp.exp(s1 - v1[0:1])
        n_ref[0, :, lanes] = nkey
        r2_ref[0, :, lanes] = rank2
        p2_ref[0, :, lanes] = jnp.exp(s2 - v2[0:1]) / z
        return carry

    lax.fori_loop(0, nsub, sub, 0)


def _router(qt, sk, tt=512):
    dq2, n = qt.shape
    nh = PEER_HEADS
    nk = PEER_NKEYS
    out = jax.ShapeDtypeStruct((nh, nk, n), F32)
    ospec = pl.BlockSpec((1, nk, tt), lambda i, h: (h, 0, i))
    return pl.pallas_call(
        functools.partial(_router_kernel, nsub=tt // LANES),
        grid=(n // tt, nh),
        in_specs=[pl.BlockSpec((2 * nk, tt), lambda i, h: (h, i)),
                  pl.BlockSpec((2, nk, nk), lambda i, h: (0, 0, 0))],
        out_specs=(ospec, ospec, ospec, ospec),
        out_shape=(out, out, out, out),
        compiler_params=_cparams("arbitrary", "arbitrary"),
        name="peer_router",
    )(qt, sk)


def _peer_kernel(ht_ref, u_ref, vt_ref, p1_ref, n_ref, r2_ref, p2_ref, x_ref, g_ref, o_ref,
                 acc, a_s, gbuf, *, ne1, nh):
    j = pl.program_id(1)
    nk = PEER_NKEYS

    @pl.when(j == 0)
    def _():
        acc[...] = jnp.zeros_like(acc)

    a_s[...] = jnp.dot(u_ref[...], ht_ref[...], preferred_element_type=F32)

    def per_e1(e, carry):
        rows = pl.ds(pl.multiple_of(e * nk, nk), nk)
        act = _gelu(a_s[rows, :])
        w = jnp.zeros(act.shape, F32)
        for h in range(nh):
            p1 = p1_ref[h, pl.ds(e, 1), :]
            cnt = n_ref[h, pl.ds(e, 1), :]
            w = w + p1 * jnp.where(r2_ref[h] < cnt, p2_ref[h], 0.0)
        gbuf[rows, :] = (act * w).astype(BF16)
        return carry

    lax.fori_loop(0, ne1, per_e1, 0)
    acc[...] += jnp.dot(vt_ref[...], gbuf[...], preferred_element_type=F32)

    @pl.when(j == pl.num_programs(1) - 1)
    def _():
        o_ref[...] = x_ref[...] + g_ref[0] * acc[...].T


def _peer(ht, u_tab, vt_tab, p1, cnt, r2, p2, x2d, g, seq, tt=512, eb=1024):
    d, n = ht.shape
    ne = u_tab.shape[0]
    nh, nk, _ = p1.shape
    ne1 = eb // nk
    per_batch = seq // tt
    tok = pl.BlockSpec((nh, nk, tt), lambda i, j: (0, 0, i))
    e1b = pl.BlockSpec((nh, ne1, tt), lambda i, j: (0, j, i))
    return pl.pallas_call(
        functools.partial(_peer_kernel, ne1=ne1, nh=nh),
        grid=(n // tt, ne // eb),
        in_specs=[
            pl.BlockSpec((d, tt), lambda i, j: (0, i)),
            pl.BlockSpec((eb, d), lambda i, j: (j, 0)),
            pl.BlockSpec((d, eb), lambda i, j: (0, j)),
            e1b, e1b, tok, tok,
            pl.BlockSpec((tt, d), lambda i, j: (i, 0)),
            pl.BlockSpec((1, 1, d), lambda i, j: (i // per_batch, 0, 0)),
        ],
        out_specs=pl.BlockSpec((tt, d), lambda i, j: (i, 0)),
        out_shape=jax.ShapeDtypeStruct((n, d), F32),
        scratch_shapes=[pltpu.VMEM((d, tt), F32), pltpu.VMEM((eb, tt), F32),
                        pltpu.VMEM((eb, tt), BF16)],
        compiler_params=_cparams("arbitrary", "arbitrary", vmem=60 * 1024 * 1024),
        name="peer_experts",
    )(ht, u_tab, vt_tab, p1, cnt, r2, p2, x2d, g)


def kernel(x, c, w_ada, b_ada, norm_mix, norm_ffn, w_in, ssm_lambda_re, ssm_lambda_im, ssm_log_dt, ssm_b_re, ssm_b_im, ssm_c_re, ssm_c_im, ssm_d, w_glu, w_br_ssm, w_br_attn, w_out, peer_wq, peer_subkeys, peer_u, peer_v, norm_final):
    bsz, seq, d = x.shape
    depth = w_ada.shape[0]
    n = bsz * seq
    ssm_w = ssm_d.shape[1]
    attn_w = w_br_attn.shape[1]
    assert bsz <= SUBLANES and seq % S5_CHUNK == 0

    c_pad = jnp.pad(c, ((0, SUBLANES - bsz), (0, 0)))
    mod_all = _modulation(c_pad, w_ada, b_ada)
    x2d = x.reshape(n, d)
    ts = S5_CHUNK // SUBLANES

    for l in range(depth):
        mods = [mod_all[l, :bsz, i * d:(i + 1) * d].reshape(bsz, 1, d) for i in range(N_MOD)]
        sh_m, sc_m, g_m, sh_f, sc_f, g_f = mods

        h = _norm_mod(x2d, norm_mix[l], sc_m, sh_m, seq, transpose=False)
        w_in_b = w_in[l].astype(BF16)
        u = _matmul(h, w_in_b, col0=0, ncols=ssm_w, out_dtype=F32, tm=1024, tn=1024, name="in_proj_u")
        qkv = _matmul(h, w_in_b, col0=ssm_w, ncols=3 * attn_w, out_dtype=F32, tm=1024, tn=1024,
                      name="in_proj_qkv")
        gates = _matmul(h, w_in_b, col0=ssm_w + 3 * attn_w, ncols=2 * d, out_dtype=BF16,
                        act="sigmoid", tm=1024, tn=1024, name="in_proj_gates")

        a_re, a_im, bbar_re, bbar_im, pw_re, pw_im = _s5_prep(
            ssm_lambda_re[l], ssm_lambda_im[l], ssm_log_dt[l], ssm_b_re[l], ssm_b_im[l], ts)
        bblk, cblk = _s5_block_weights(bbar_re, bbar_im, ssm_c_re[l], ssm_c_im[l])
        y_s = _s5_branch(u, seq, a_re, a_im, pw_re, pw_im, bblk, cblk, ssm_d[l],
                         w_glu[l].astype(BF16))
        y_a = _dilated_attention(qkv, seq)
        merged = _merge(y_s, y_a, w_br_ssm[l].astype(BF16), w_br_attn[l].astype(BF16), gates)
        x2d = _outproj(merged, w_out[l].astype(BF16), x2d, g_m, seq)

        ht = _norm_mod(x2d, norm_ffn[l], sc_f, sh_f, seq, transpose=True)
        qt = _matmul(peer_wq[l].T.astype(BF16), ht, col0=0, ncols=n, out_dtype=F32,
                     tm=1024, tn=1024, name="peer_query")
        p1, cnt, r2, p2 = _router(qt, peer_subkeys[l].astype(BF16))
        x2d = _peer(ht, peer_u[l].astype(BF16), peer_v[l].T.astype(BF16), p1, cnt, r2, p2,
                    x2d, g_f, seq)

    return _final_norm(x2d, norm_final).reshape(bsz, seq, d)
```

```python
import functools
import math

import jax
import jax.numpy as jnp
import numpy as np
from jax import lax
from jax.experimental import pallas as pl
from jax.experimental.pallas import tpu as pltpu

F32 = jnp.float32
BF16 = jnp.bfloat16

EPS = 1e-6
N_MOD = 6
SSM_GROUP = 16
SSM_STATE = 64
HEAD_DIM = 128
ATTN_BLOCK = 128
ATTN_GROUP = 4
DILATED_PATTERNS = ((128, 1), (512, 4), (2048, 16))
PEER_HEADS = 8
PEER_NKEYS = 128
PEER_TOPK = 16

LANES = 128
SUBLANES = 8
S5_CHUNK = 256
S5_LANE_BLOCK = 512
S5_CH_BLOCK = 128
VMEM_LIMIT = 56 * 1024 * 1024


def _cparams(*sem, vmem=VMEM_LIMIT):
    return pltpu.CompilerParams(dimension_semantics=sem, vmem_limit_bytes=vmem)


def _gelu(x):
    return 0.5 * x * (1.0 + lax.erf(x * np.float32(math.sqrt(0.5))))


def _split3(x):
    hi = x.astype(BF16)
    r1 = x - hi.astype(F32)
    mid = r1.astype(BF16)
    lo = (r1 - mid.astype(F32)).astype(BF16)
    return hi, mid, lo


def _mod_kernel(c_ref, w_ref, b_ref, o_ref):
    c = c_ref[...]
    ca = c * jax.nn.sigmoid(c)
    w = w_ref[0]
    c_hi = ca.astype(BF16)
    c_lo = (ca - c_hi.astype(F32)).astype(BF16)
    w_hi = w.astype(BF16)
    w_lo = (w - w_hi.astype(F32)).astype(BF16)
    acc = jnp.dot(c_hi, w_hi, preferred_element_type=F32)
    acc += jnp.dot(c_lo, w_hi, preferred_element_type=F32)
    acc += jnp.dot(c_hi, w_lo, preferred_element_type=F32)
    o_ref[0] = acc + b_ref[0]


def _modulation(c_pad, w_ada, b_ada, tn=1024):
    depth, d, cols = w_ada.shape
    return pl.pallas_call(
        _mod_kernel,
        grid=(depth, cols // tn),
        in_specs=[
            pl.BlockSpec((SUBLANES, d), lambda l, j: (0, 0)),
            pl.BlockSpec((1, d, tn), lambda l, j: (l, 0, j)),
            pl.BlockSpec((1, 1, tn), lambda l, j: (l, 0, j)),
        ],
        out_specs=pl.BlockSpec((1, SUBLANES, tn), lambda l, j: (l, 0, j)),
        out_shape=jax.ShapeDtypeStruct((depth, SUBLANES, cols), F32),
        compiler_params=_cparams("arbitrary", "arbitrary"),
        name="adaln_mod",
    )(c_pad, w_ada, b_ada.reshape(depth, 1, cols))


def _norm_mod_kernel(x_ref, gain_ref, sc_ref, sh_ref, o_ref, *, transpose):
    x = x_ref[...]
    ms = jnp.mean(x * x, axis=-1, keepdims=True)
    y = x * lax.rsqrt(ms + EPS) * gain_ref[...]
    h = y * (1.0 + sc_ref[0]) + sh_ref[0]
    if transpose:
        o_ref[...] = h.T.astype(o_ref.dtype)
    else:
        o_ref[...] = h.astype(o_ref.dtype)


def _norm_mod(x2d, gain, sc, sh, seq, *, transpose, tm=512):
    n, d = x2d.shape
    per_batch = seq // tm
    if transpose:
        out_spec = pl.BlockSpec((d, tm), lambda i: (0, i))
        out_shape = jax.ShapeDtypeStruct((d, n), BF16)
    else:
        out_spec = pl.BlockSpec((tm, d), lambda i: (i, 0))
        out_shape = jax.ShapeDtypeStruct((n, d), BF16)
    return pl.pallas_call(
        functools.partial(_norm_mod_kernel, transpose=transpose),
        grid=(n // tm,),
        in_specs=[
            pl.BlockSpec((tm, d), lambda i: (i, 0)),
            pl.BlockSpec((1, d), lambda i: (0, 0)),
            pl.BlockSpec((1, 1, d), lambda i: (i // per_batch, 0, 0)),
            pl.BlockSpec((1, 1, d), lambda i: (i // per_batch, 0, 0)),
        ],
        out_specs=out_spec,
        out_shape=out_shape,
        compiler_params=_cparams("arbitrary"),
        name="norm_mod_t" if transpose else "norm_mod",
    )(x2d, gain.reshape(1, d), sc, sh)


def _final_norm_kernel(x_ref, gain_ref, o_ref):
    x = x_ref[...]
    ms = jnp.mean(x * x, axis=-1, keepdims=True)
    o_ref[...] = x * lax.rsqrt(ms + EPS) * gain_ref[...]


def _final_norm(x2d, gain, tm=512):
    n, d = x2d.shape
    return pl.pallas_call(
        _final_norm_kernel,
        grid=(n // tm,),
        in_specs=[pl.BlockSpec((tm, d), lambda i: (i, 0)),
                  pl.BlockSpec((1, d), lambda i: (0, 0))],
        out_specs=pl.BlockSpec((tm, d), lambda i: (i, 0)),
        out_shape=jax.ShapeDtypeStruct((n, d), F32),
        compiler_params=_cparams("arbitrary"),
        name="final_norm",
    )(x2d, gain.reshape(1, d))


def _mm_kernel(a_ref, w_ref, o_ref, *, act):
    acc = jnp.dot(a_ref[...], w_ref[...], preferred_element_type=F32)
    if act == "sigmoid":
        acc = jax.nn.sigmoid(acc)
    o_ref[...] = acc.astype(o_ref.dtype)


def _matmul(a, w, *, col0, ncols, out_dtype, act=None, tm, tn, name, layer=None):
    m, k = a.shape
    off = col0 // tn
    if layer is None:
        w_spec = pl.BlockSpec((k, tn), lambda i, j: (0, j + off))
    else:
        w_spec = pl.BlockSpec((None, k, tn), lambda i, j: (layer, 0, j + off))
    return pl.pallas_call(
        functools.partial(_mm_kernel, act=act),
        grid=(m // tm, ncols // tn),
        in_specs=[pl.BlockSpec((tm, k), lambda i, j: (i, 0)), w_spec],
        out_specs=pl.BlockSpec((tm, tn), lambda i, j: (i, j)),
        out_shape=jax.ShapeDtypeStruct((m, ncols), out_dtype),
        compiler_params=_cparams("arbitrary", "arbitrary"),
        name=name,
    )(a, w)


def _merge_kernel(ys_ref, ya_ref, ws_ref, wa_ref, gs_ref, ga_ref, o_ref):
    ps = jnp.dot(ys_ref[...], ws_ref[...], preferred_element_type=F32)
    pa = jnp.dot(ya_ref[...], wa_ref[...], preferred_element_type=F32)
    o_ref[...] = (gs_ref[...].astype(F32) * ps
                  + ga_ref[...].astype(F32) * pa).astype(o_ref.dtype)


def _merge(ys, ya, ws, wa, gates, tm=1024, tn=1024):
    n, ks = ys.shape
    ka = ya.shape[1]
    d = ws.shape[1]
    goff = d // tn
    return pl.pallas_call(
        _merge_kernel,
        grid=(n // tm, d // tn),
        in_specs=[
            pl.BlockSpec((tm, ks), lambda i, j: (i, 0)),
            pl.BlockSpec((tm, ka), lambda i, j: (i, 0)),
            pl.BlockSpec((ks, tn), lambda i, j: (0, j)),
            pl.BlockSpec((ka, tn), lambda i, j: (0, j)),
            pl.BlockSpec((tm, tn), lambda i, j: (i, j)),
            pl.BlockSpec((tm, tn), lambda i, j: (i, j + goff)),
        ],
        out_specs=pl.BlockSpec((tm, tn), lambda i, j: (i, j)),
        out_shape=jax.ShapeDtypeStruct((n, d), BF16),
        compiler_params=_cparams("arbitrary", "arbitrary"),
        name="branch_merge",
    )(ys, ya, ws, wa, gates, gates)


def _outproj_kernel(a_ref, w_ref, x_ref, g_ref, o_ref):
    acc = jnp.dot(a_ref[...], w_ref[...], preferred_element_type=F32)
    o_ref[...] = x_ref[...] + g_ref[0] * acc


def _outproj(a, w, x2d, g, seq, tm=1024, tn=1024):
    n, k = a.shape
    d = w.shape[1]
    per_batch = seq // tm
    return pl.pallas_call(
        _outproj_kernel,
        grid=(n // tm, d // tn),
        in_specs=[
            pl.BlockSpec((tm, k), lambda i, j: (i, 0)),
            pl.BlockSpec((k, tn), lambda i, j: (0, j)),
            pl.BlockSpec((tm, tn), lambda i, j: (i, j)),
            pl.BlockSpec((1, 1, tn), lambda i, j: (i // per_batch, 0, j)),
        ],
        out_specs=pl.BlockSpec((tm, tn), lambda i, j: (i, j)),
        out_shape=jax.ShapeDtypeStruct((n, d), F32),
        compiler_params=_cparams("arbitrary", "arbitrary"),
        name="out_proj",
    )(a, w, x2d, g)


def _s5_prep_kernel(lr_ref, li_ref, ldt_ref, br_ref, bi_ref,
                    are_ref, aim_ref, bbr_ref, bbi_ref, pwr_ref, pwi_ref, *, ts):
    lr, li = lr_ref[...], li_ref[...]
    dt = jnp.exp(ldt_ref[...])
    mag = jnp.exp(lr * dt)
    a_re = mag * jnp.cos(li * dt)
    a_im = mag * jnp.sin(li * dt)
    inv = 1.0 / (lr * lr + li * li)
    coef_re = ((a_re - 1.0) * lr + a_im * li) * inv
    coef_im = (a_im * lr - (a_re - 1.0) * li) * inv
    br, bi = br_ref[...], bi_ref[...]
    bbr_ref[...] = coef_re * br - coef_im * bi
    bbi_ref[...] = coef_re * bi + coef_im * br
    are_ref[...] = a_re
    aim_ref[...] = a_im
    pr, pi = a_re, a_im
    for j in range(ts):
        pwr_ref[j:j + 1, :] = pr
        pwi_ref[j:j + 1, :] = pi
        pr, pi = pr * a_re - pi * a_im, pr * a_im + pi * a_re


def _s5_prep(lam_re, lam_im, log_dt, b_re, b_im, ts):
    g, p, c = b_re.shape
    ns = g * p
    row = lambda t: t.reshape(1, ns)
    ldt = jnp.broadcast_to(log_dt[:, None], (g, p))
    bt = lambda t: jnp.transpose(t, (2, 0, 1)).reshape(c, ns)
    shapes = (
        jax.ShapeDtypeStruct((1, ns), F32), jax.ShapeDtypeStruct((1, ns), F32),
        jax.ShapeDtypeStruct((c, ns), F32), jax.ShapeDtypeStruct((c, ns), F32),
        jax.ShapeDtypeStruct((ts, ns), F32), jax.ShapeDtypeStruct((ts, ns), F32),
    )
    return pl.pallas_call(
        functools.partial(_s5_prep_kernel, ts=ts),
        out_shape=shapes,
        name="s5_prep",
    )(row(lam_re), row(lam_im), row(ldt), bt(b_re), bt(b_im))


def _s5_block_weights(bbar_re, bbar_im, c_re, c_im):
    c, ns = bbar_re.shape
    gpb = S5_CH_BLOCK // c
    nk = ns // (gpb * SSM_STATE)
    eye = jnp.eye(gpb, dtype=F32)

    def bpart(t):
        t = t.reshape(c, nk, gpb, SSM_STATE)
        t = jnp.einsum("ckjp,ij->kicjp", t, eye)
        return t.reshape(nk, gpb * c, gpb * SSM_STATE)

    def cpart(t):
        t = t.reshape(nk, gpb, c, SSM_STATE)
        t = jnp.einsum("kicp,ij->kjpic", t, eye)
        return t.reshape(nk, gpb * SSM_STATE, gpb * c)

    bblk = jnp.concatenate([bpart(bbar_re), bpart(bbar_im)], axis=2).astype(BF16)
    cblk = jnp.concatenate([cpart(c_re), cpart(-c_im)], axis=1).astype(BF16)
    return bblk, cblk


def _s5_kernel(u_ref, pm_ref, pmt_ref, bblk_ref, are_ref, aim_ref, pwr_ref, pwi_ref,
               cblk_ref, d_ref, wglu_ref, o_ref,
               hr, hi, car_r, car_i, cs_r, cs_i, yp, *, ts, nk):
    lw = S5_LANE_BLOCK
    cw = S5_CH_BLOCK
    t_rows = SUBLANES * ts

    @pl.when(pl.program_id(1) == 0)
    def _():
        car_r[...] = jnp.zeros_like(car_r)
        car_i[...] = jnp.zeros_like(car_i)

    u = u_ref[...]
    up = jnp.dot(pm_ref[...], u.astype(BF16), preferred_element_type=F32).astype(BF16)
    for k in range(nk):
        bu = jnp.dot(up[:, k * cw:(k + 1) * cw], bblk_ref[k], preferred_element_type=F32)
        hr[:, k * lw:(k + 1) * lw] = bu[:, :lw]
        hi[:, k * lw:(k + 1) * lw] = bu[:, lw:]

    for k in range(nk):
        sl = pl.ds(k * lw, lw)
        ar = jnp.broadcast_to(are_ref[:, sl], (SUBLANES, lw))
        ai = jnp.broadcast_to(aim_ref[:, sl], (SUBLANES, lw))

        def local_step(j, carry, sl=sl, ar=ar, ai=ai):
            cr, cim = carry
            rows = pl.ds(pl.multiple_of(j * SUBLANES, SUBLANES), SUBLANES)
            nr = ar * cr - ai * cim + hr[rows, sl]
            ni = ar * cim + ai * cr + hi[rows, sl]
            hr[rows, sl] = nr
            hi[rows, sl] = ni
            return nr, ni

        zero = jnp.zeros((SUBLANES, lw), F32)
        lax.fori_loop(0, ts, local_step, (zero, zero))

    c_r = car_r[...]
    c_i = car_i[...]
    at_r = pwr_ref[ts - 1:ts, :]
    at_i = pwi_ref[ts - 1:ts, :]
    for s in range(SUBLANES):
        cs_r[s:s + 1, :] = c_r
        cs_i[s:s + 1, :] = c_i
        row = t_rows - SUBLANES + s
        l_r = hr[row:row + 1, :]
        l_i = hi[row:row + 1, :]
        c_r, c_i = l_r + at_r * c_r - at_i * c_i, l_i + at_r * c_i + at_i * c_r
    car_r[...] = c_r
    car_i[...] = c_i

    for k in range(nk):
        sl = pl.ds(k * lw, lw)
        csr = cs_r[:, sl]
        csi = cs_i[:, sl]

        def fix_step(j, carry, sl=sl, csr=csr, csi=csi):
            rows = pl.ds(pl.multiple_of(j * SUBLANES, SUBLANES), SUBLANES)
            pr = pwr_ref[pl.ds(j, 1), sl]
            pi = pwi_ref[pl.ds(j, 1), sl]
            hr[rows, sl] = hr[rows, sl] + (pr * csr - pi * csi)
            hi[rows, sl] = hi[rows, sl] + (pr * csi + pi * csr)
            return carry

        lax.fori_loop(0, ts, fix_step, 0)
        hcat = jnp.concatenate([hr[:, sl], hi[:, sl]], axis=1).astype(BF16)
        yp[:, k * cw:(k + 1) * cw] = jnp.dot(hcat, cblk_ref[k], preferred_element_type=F32)

    y1, y2, y3 = _split3(yp[...])
    pmt = pmt_ref[...]
    y = (jnp.dot(pmt, y1, preferred_element_type=F32)
         + jnp.dot(pmt, y2, preferred_element_type=F32)
         + jnp.dot(pmt, y3, preferred_element_type=F32))
    y = y + d_ref[...] * u
    y = _gelu(y)
    gate = jax.nn.sigmoid(jnp.dot(y.astype(BF16), wglu_ref[...], preferred_element_type=F32))
    o_ref[...] = (y * gate).astype(o_ref.dtype)


def _s5_permutation(t_rows, ts):
    r = np.arange(t_rows)
    src = (r % SUBLANES) * ts + r // SUBLANES
    pm = np.zeros((t_rows, t_rows), np.float32)
    pm[r, src] = 1.0
    return jnp.asarray(pm, BF16), jnp.asarray(pm.T, BF16)


def _s5_branch(u, seq, a_re, a_im, pw_re, pw_im, bblk, cblk, d_skip, w_glu):
    n, w = u.shape
    ns = a_re.shape[1]
    t_rows = S5_CHUNK
    ts = t_rows // SUBLANES
    nk = ns // S5_LANE_BLOCK
    nchunk = seq // t_rows
    pm, pmt = _s5_permutation(t_rows, ts)
    full = lambda shape: pl.BlockSpec(shape, lambda b, c: (0,) * len(shape))
    return pl.pallas_call(
        functools.partial(_s5_kernel, ts=ts, nk=nk),
        grid=(n // seq, nchunk),
        in_specs=[
            pl.BlockSpec((t_rows, w), lambda b, c: (b * nchunk + c, 0)),
            full((t_rows, t_rows)), full((t_rows, t_rows)),
            full(bblk.shape),
            full((1, ns)), full((1, ns)), full((ts, ns)), full((ts, ns)),
            full(cblk.shape),
            full((1, w)), full((w, w)),
        ],
        out_specs=pl.BlockSpec((t_rows, w), lambda b, c: (b * nchunk + c, 0)),
        out_shape=jax.ShapeDtypeStruct((n, w), BF16),
        scratch_shapes=[
            pltpu.VMEM((t_rows, ns), F32), pltpu.VMEM((t_rows, ns), F32),
            pltpu.VMEM((1, ns), F32), pltpu.VMEM((1, ns), F32),
            pltpu.VMEM((SUBLANES, ns), F32), pltpu.VMEM((SUBLANES, ns), F32),
            pltpu.VMEM((t_rows, w), F32),
        ],
        compiler_params=_cparams("arbitrary", "arbitrary"),
        name="s5_branch",
    )(u, pm, pmt, bblk, a_re, a_im, pw_re, pw_im, cblk, d_skip.reshape(1, w), w_glu)


def _attn_kernel(q_ref, k_ref, v_ref, o_ref, o_s, m_s, l_s, *, seq, patterns, scale):
    blk = ATTN_BLOCK
    qi = lax.broadcasted_iota(jnp.int32, (blk, 2 * blk), 0)
    kj = lax.broadcasted_iota(jnp.int32, (blk, 2 * blk), 1)
    dist = blk + qi - kj
    nt = (((1,), (1,)), ((), ()))

    for pi, (window, dil) in enumerate(patterns):
        n_back = window // dil
        nblk = seq // dil // blk
        grp = min(ATTN_GROUP, nblk)
        ngrp = nblk // grp
        band = (dist >= 0) & (dist <= n_back)

        def group_step(idx, carry, pi=pi, dil=dil, grp=grp, ngrp=ngrp, band=band):
            r = idx // ngrp
            n0 = (idx % ngrp) * grp

            def rows_of(nb):
                start = r + nb * (blk * dil)
                return pl.ds(start, blk) if dil == 1 else pl.ds(start, blk, stride=dil)

            rows = [rows_of(jnp.maximum(n0 - 1, 0))] + [rows_of(n0 + gi) for gi in range(grp)]
            ks = [k_ref[rw, :].astype(BF16) for rw in rows]
            scores = []
            for gi in range(grp):
                q = q_ref[rows[gi + 1], :].astype(BF16)
                scores.append(jnp.concatenate(
                    [lax.dot_general(q, ks[gi], nt, preferred_element_type=F32),
                     lax.dot_general(q, ks[gi + 1], nt, preferred_element_type=F32)], axis=1))
            probs = []
            for gi in range(grp):
                mask = band & ((kj >= blk) | (n0 > 0)) if gi == 0 else band
                s = jnp.where(mask, scores[gi] * scale, -jnp.inf)
                m = jnp.max(s, axis=1, keepdims=True)
                p = jnp.exp(s - m)
                l = jnp.sum(p, axis=1, keepdims=True)
                m_s[pi, rows[gi + 1], :] = jnp.broadcast_to(m, (blk, HEAD_DIM))
                l_s[pi, rows[gi + 1], :] = jnp.broadcast_to(l, (blk, HEAD_DIM))
                probs.append(p.astype(BF16))
            vs = [v_ref[rw, :].astype(BF16) for rw in rows]
            for gi in range(grp):
                pb = probs[gi]
                o_s[pi, rows[gi + 1], :] = (
                    jnp.dot(pb[:, :blk], vs[gi], preferred_element_type=F32)
                    + jnp.dot(pb[:, blk:], vs[gi + 1], preferred_element_type=F32))
            return carry

        lax.fori_loop(0, dil * ngrp, group_step, 0)

    npat = len(patterns)

    def merge_step(t, carry):
        rows = pl.ds(pl.multiple_of(t * blk, blk), blk)
        ms = [m_s[pi, rows, :] for pi in range(npat)]
        mx = functools.reduce(jnp.maximum, ms)
        num = jnp.zeros((blk, HEAD_DIM), F32)
        den = jnp.zeros((blk, HEAD_DIM), F32)
        for pi in range(npat):
            w = jnp.exp(ms[pi] - mx)
            num = num + w * o_s[pi, rows, :]
            den = den + w * l_s[pi, rows, :]
        o_ref[rows, :] = (num / den).astype(o_ref.dtype)
        return carry

    lax.fori_loop(0, seq // blk, merge_step, 0)


def _dilated_attention(qkv, seq):
    n = qkv.shape[0]
    nh = qkv.shape[1] // (3 * HEAD_DIM)
    for window, dil in DILATED_PATTERNS:
        assert window // dil <= ATTN_BLOCK and seq % (dil * ATTN_BLOCK) == 0
    blk = (seq, HEAD_DIM)
    return pl.pallas_call(
        functools.partial(_attn_kernel, seq=seq, patterns=DILATED_PATTERNS,
                          scale=np.float32(HEAD_DIM ** -0.5)),
        grid=(n // seq, nh),
        in_specs=[pl.BlockSpec(blk, lambda b, h: (b, h)),
                  pl.BlockSpec(blk, lambda b, h: (b, nh + h)),
                  pl.BlockSpec(blk, lambda b, h: (b, 2 * nh + h))],
        out_specs=pl.BlockSpec(blk, lambda b, h: (b, h)),
        out_shape=jax.ShapeDtypeStruct((n, nh * HEAD_DIM), BF16),
        scratch_shapes=[pltpu.VMEM((len(DILATED_PATTERNS),) + blk, F32)] * 3,
        compiler_params=_cparams("arbitrary", "arbitrary"),
        name="dilated_attn",
    )(qkv, qkv, qkv)


def _top16(s, kid):
    nkeys = s.shape[0]
    rank = jnp.full(s.shape, float(nkeys), F32)
    vals = []
    for it in range(PEER_TOPK):
        m = jnp.max(s, axis=0, keepdims=True)
        idx = jnp.min(jnp.where(s == m, kid, float(nkeys)), axis=0, keepdims=True)
        sel = kid == idx
        rank = jnp.where(sel, float(it), rank)
        s = jnp.where(sel, -jnp.inf, s)
        vals.append(m)
    return rank, jnp.concatenate(vals, axis=0)


def _router_kernel(q_ref, sk_ref, p1_ref, n_ref, r2_ref, p2_ref, *, nsub):
    nk = PEER_NKEYS
    kt = PEER_TOPK
    half = kt // 2
    kid = lax.broadcasted_iota(jnp.int32, (nk, LANES), 0).astype(F32)
    ia = lax.broadcasted_iota(jnp.int32, (kt, LANES), 0).astype(F32)
    ih = ia[0:half]
    cid = jnp.concatenate([ia * kt] + [ih * kt + float(b) for b in range(1, half)] + [ih + float(half)],
                          axis=0)
    ncand = kt * kt

    def sub(st, carry):
        lanes = pl.ds(pl.multiple_of(st * LANES, LANES), LANES)
        q1 = q_ref[0:nk, lanes].astype(BF16)
        q2 = q_ref[nk:2 * nk, lanes].astype(BF16)
        s1 = jnp.dot(sk_ref[0], q1, preferred_element_type=F32)
        s2 = jnp.dot(sk_ref[1], q2, preferred_element_type=F32)
        rank1, v1 = _top16(s1, kid)
        rank2, v2 = _top16(s2, kid)
        work = jnp.concatenate([v1 + v2[0:1]] + [v1[0:half] + v2[b:b + 1] for b in range(1, half)]
                               + [v1[0:1] + v2[half:kt]], axis=0)
        m0 = v1[0:1] + v2[0:1]
        z = jnp.zeros((1, LANES), F32)
        n_a = jnp.zeros((kt, LANES), F32)
        for _ in range(kt):
            m = jnp.max(work, axis=0, keepdims=True)
            idx = jnp.min(jnp.where(work == m, cid, float(ncand)), axis=0, keepdims=True)
            work = jnp.where(cid == idx, -jnp.inf, work)
            a_sel = jnp.floor(idx * (1.0 / kt))
            n_a = n_a + jnp.where(ia == a_sel, 1.0, 0.0)
            z = z + jnp.exp(m - m0)
        nkey = jnp.zeros((nk, LANES), F32)
        for a in range(kt):
            nkey = jnp.where(rank1 == float(a), n_a[a:a + 1], nkey)
        p1_ref[0, :, lanes] = jnp.exp(s1 - v1[0:1])
        n_ref[0, :, lanes] = nkey
        r2_ref[0, :, lanes] = rank2.astype(r2_ref.dtype)
        p2_ref[0, :, lanes] = (jnp.exp(s2 - v2[0:1]) / z).astype(p2_ref.dtype)
        return carry

    lax.fori_loop(0, nsub, sub, 0)


def _router(qt, sk, tt=512):
    dq2, n = qt.shape
    nh = PEER_HEADS
    nk = PEER_NKEYS
    out = jax.ShapeDtypeStruct((nh, nk, n), F32)
    out_b = jax.ShapeDtypeStruct((nh, nk, n), BF16)
    ospec = pl.BlockSpec((1, nk, tt), lambda i, h: (h, 0, i))
    return pl.pallas_call(
        functools.partial(_router_kernel, nsub=tt // LANES),
        grid=(n // tt, nh),
        in_specs=[pl.BlockSpec((2 * nk, tt), lambda i, h: (h, i)),
                  pl.BlockSpec((2, nk, nk), lambda i, h: (0, 0, 0))],
        out_specs=(ospec, ospec, ospec, ospec),
        out_shape=(out, out, out_b, out_b),
        compiler_params=_cparams("arbitrary", "arbitrary"),
        name="peer_router",
    )(qt, sk)


def _peer_kernel(ht_ref, u_ref, vt_ref, p1_ref, n_ref, r2_ref, p2_ref, x_ref, g_ref, o_ref,
                 acc, a_s, gbuf, *, ne1, nh):
    j = pl.program_id(1)
    nk = PEER_NKEYS

    @pl.when(j == 0)
    def _():
        acc[...] = jnp.zeros_like(acc)

    a_s[...] = jnp.dot(u_ref[...], ht_ref[...], preferred_element_type=F32)

    def per_e1(e, carry):
        rows = pl.ds(pl.multiple_of(e * nk, nk), nk)
        act = _gelu(a_s[rows, :])
        w = jnp.zeros(act.shape, BF16)
        for h in range(nh):
            p1 = jnp.broadcast_to(p1_ref[h, pl.ds(e, 1), :], act.shape).astype(BF16)
            cnt = jnp.broadcast_to(n_ref[h, pl.ds(e, 1), :], act.shape).astype(BF16)
            w = w + p1 * jnp.where(r2_ref[h] < cnt, p2_ref[h], jnp.zeros((), BF16))
        gbuf[rows, :] = act.astype(BF16) * w
        return carry

    lax.fori_loop(0, ne1, per_e1, 0)
    acc[...] += jnp.dot(vt_ref[...], gbuf[...], preferred_element_type=F32)

    @pl.when(j == pl.num_programs(1) - 1)
    def _():
        o_ref[...] = x_ref[...] + g_ref[0] * acc[...].T


def _peer(ht, u_tab, vt_tab, layer, p1, cnt, r2, p2, x2d, g, seq, tt=512, eb=1024):
    d, n = ht.shape
    ne = u_tab.shape[1]
    nh, nk, _ = p1.shape
    ne1 = eb // nk
    per_batch = seq // tt
    tok = pl.BlockSpec((nh, nk, tt), lambda i, j: (0, 0, i))
    e1b = pl.BlockSpec((nh, ne1, tt), lambda i, j: (0, j, i))
    return pl.pallas_call(
        functools.partial(_peer_kernel, ne1=ne1, nh=nh),
        grid=(n // tt, ne // eb),
        in_specs=[
            pl.BlockSpec((d, tt), lambda i, j: (0, i)),
            pl.BlockSpec((None, eb, d), lambda i, j: (layer, j, 0)),
            pl.BlockSpec((None, d, eb), lambda i, j: (layer, 0, j)),
            e1b, e1b, tok, tok,
            pl.BlockSpec((tt, d), lambda i, j: (i, 0)),
            pl.BlockSpec((1, 1, d), lambda i, j: (i // per_batch, 0, 0)),
        ],
        out_specs=pl.BlockSpec((tt, d), lambda i, j: (i, 0)),
        out_shape=jax.ShapeDtypeStruct((n, d), F32),
        scratch_shapes=[pltpu.VMEM((d, tt), F32), pltpu.VMEM((eb, tt), F32),
                        pltpu.VMEM((eb, tt), BF16)],
        compiler_params=_cparams("arbitrary", "arbitrary", vmem=60 * 1024 * 1024),
        name="peer_experts",
    )(ht, u_tab, vt_tab, p1, cnt, r2, p2, x2d, g)


def kernel(x, c, w_ada, b_ada, norm_mix, norm_ffn, w_in, ssm_lambda_re, ssm_lambda_im, ssm_log_dt, ssm_b_re, ssm_b_im, ssm_c_re, ssm_c_im, ssm_d, w_glu, w_br_ssm, w_br_attn, w_out, peer_wq, peer_subkeys, peer_u, peer_v, norm_final):
    bsz, seq, d = x.shape
    depth = w_ada.shape[0]
    n = bsz * seq
    ssm_w = ssm_d.shape[1]
    attn_w = w_br_attn.shape[1]
    assert bsz <= SUBLANES and seq % S5_CHUNK == 0

    c_pad = jnp.pad(c, ((0, SUBLANES - bsz), (0, 0)))
    mod_all = _modulation(c_pad, w_ada, b_ada)
    x2d = x.reshape(n, d)
    ts = S5_CHUNK // SUBLANES
    w_in_all = w_in.astype(BF16)
    u_all = peer_u.astype(BF16)
    vt_all = jnp.swapaxes(peer_v, 1, 2).astype(BF16)

    for l in range(depth):
        mods = [mod_all[l, :bsz, i * d:(i + 1) * d].reshape(bsz, 1, d) for i in range(N_MOD)]
        sh_m, sc_m, g_m, sh_f, sc_f, g_f = mods

        h = _norm_mod(x2d, norm_mix[l], sc_m, sh_m, seq, transpose=False)
        u = _matmul(h, w_in_all, layer=l, col0=0, ncols=ssm_w, out_dtype=F32, tm=1024, tn=1024,
                    name="in_proj_u")
        qkv = _matmul(h, w_in_all, layer=l, col0=ssm_w, ncols=3 * attn_w, out_dtype=F32,
                      tm=1024, tn=1024, name="in_proj_qkv")
        gates = _matmul(h, w_in_all, layer=l, col0=ssm_w + 3 * attn_w, ncols=2 * d, out_dtype=BF16,
                        act="sigmoid", tm=1024, tn=1024, name="in_proj_gates")

        a_re, a_im, bbar_re, bbar_im, pw_re, pw_im = _s5_prep(
            ssm_lambda_re[l], ssm_lambda_im[l], ssm_log_dt[l], ssm_b_re[l], ssm_b_im[l], ts)
        bblk, cblk = _s5_block_weights(bbar_re, bbar_im, ssm_c_re[l], ssm_c_im[l])
        y_s = _s5_branch(u, seq, a_re, a_im, pw_re, pw_im, bblk, cblk, ssm_d[l],
                         w_glu[l].astype(BF16))
        y_a = _dilated_attention(qkv, seq)
        merged = _merge(y_s, y_a, w_br_ssm[l].astype(BF16), w_br_attn[l].astype(BF16), gates)
        x2d = _outproj(merged, w_out[l].astype(BF16), x2d, g_m, seq)

        ht = _norm_mod(x2d, norm_ffn[l], sc_f, sh_f, seq, transpose=True)
        qt = _matmul(peer_wq[l].T.astype(BF16), ht, col0=0, ncols=n, out_dtype=F32,
                     tm=1024, tn=1024, name="peer_query")
        p1, cnt, r2, p2 = _router(qt, peer_subkeys[l].astype(BF16))
        x2d = _peer(ht, u_all, vt_all, l, p1, cnt, r2, p2, x2d, g_f, seq)

    return _final_norm(x2d, norm_final).reshape(bsz, seq, d)
```

```python
import functools
import math

import jax
import jax.numpy as jnp
import numpy as np
from jax import lax
from jax.experimental import pallas as pl
from jax.experimental.pallas import tpu as pltpu

F32 = jnp.float32
BF16 = jnp.bfloat16

EPS = 1e-6
N_MOD = 6
SSM_GROUP = 16
SSM_STATE = 64
HEAD_DIM = 128
ATTN_BLOCK = 128
ATTN_GROUP = 4
DILATED_PATTERNS = ((128, 1), (512, 4), (2048, 16))
PEER_HEADS = 8
PEER_NKEYS = 128
PEER_TOPK = 16

LANES = 128
SUBLANES = 8
S5_CHUNK = 256
S5_LANE_BLOCK = 512
S5_CH_BLOCK = 128
S5_UNROLL = 4
VMEM_LIMIT = 56 * 1024 * 1024


def _cparams(*sem, vmem=VMEM_LIMIT):
    return pltpu.CompilerParams(dimension_semantics=sem, vmem_limit_bytes=vmem)


def _gelu(x):
    return 0.5 * x * (1.0 + lax.erf(x * np.float32(math.sqrt(0.5))))


def _split3(x):
    hi = x.astype(BF16)
    r1 = x - hi.astype(F32)
    mid = r1.astype(BF16)
    lo = (r1 - mid.astype(F32)).astype(BF16)
    return hi, mid, lo


def _mod_kernel(c_ref, w_ref, b_ref, o_ref):
    c = c_ref[...]
    ca = c * jax.nn.sigmoid(c)
    w = w_ref[0]
    c_hi = ca.astype(BF16)
    c_lo = (ca - c_hi.astype(F32)).astype(BF16)
    w_hi = w.astype(BF16)
    w_lo = (w - w_hi.astype(F32)).astype(BF16)
    acc = jnp.dot(c_hi, w_hi, preferred_element_type=F32)
    acc += jnp.dot(c_lo, w_hi, preferred_element_type=F32)
    acc += jnp.dot(c_hi, w_lo, preferred_element_type=F32)
    o_ref[0] = acc + b_ref[0]


def _modulation(c_pad, w_ada, b_ada, tn=1024):
    depth, d, cols = w_ada.shape
    return pl.pallas_call(
        _mod_kernel,
        grid=(depth, cols // tn),
        in_specs=[
            pl.BlockSpec((SUBLANES, d), lambda l, j: (0, 0)),
            pl.BlockSpec((1, d, tn), lambda l, j: (l, 0, j)),
            pl.BlockSpec((1, 1, tn), lambda l, j: (l, 0, j)),
        ],
        out_specs=pl.BlockSpec((1, SUBLANES, tn), lambda l, j: (l, 0, j)),
        out_shape=jax.ShapeDtypeStruct((depth, SUBLANES, cols), F32),
        compiler_params=_cparams("arbitrary", "arbitrary"),
        name="adaln_mod",
    )(c_pad, w_ada, b_ada.reshape(depth, 1, cols))


def _norm_mod_kernel(x_ref, gain_ref, sc_ref, sh_ref, o_ref, *, transpose):
    x = x_ref[...]
    ms = jnp.mean(x * x, axis=-1, keepdims=True)
    y = x * lax.rsqrt(ms + EPS) * gain_ref[...]
    h = y * (1.0 + sc_ref[0]) + sh_ref[0]
    if transpose:
        o_ref[...] = h.T.astype(o_ref.dtype)
    else:
        o_ref[...] = h.astype(o_ref.dtype)


def _norm_mod(x2d, gain, sc, sh, seq, *, transpose, tm=512):
    n, d = x2d.shape
    per_batch = seq // tm
    if transpose:
        out_spec = pl.BlockSpec((d, tm), lambda i: (0, i))
        out_shape = jax.ShapeDtypeStruct((d, n), BF16)
    else:
        out_spec = pl.BlockSpec((tm, d), lambda i: (i, 0))
        out_shape = jax.ShapeDtypeStruct((n, d), BF16)
    return pl.pallas_call(
        functools.partial(_norm_mod_kernel, transpose=transpose),
        grid=(n // tm,),
        in_specs=[
            pl.BlockSpec((tm, d), lambda i: (i, 0)),
            pl.BlockSpec((1, d), lambda i: (0, 0)),
            pl.BlockSpec((1, 1, d), lambda i: (i // per_batch, 0, 0)),
            pl.BlockSpec((1, 1, d), lambda i: (i // per_batch, 0, 0)),
        ],
        out_specs=out_spec,
        out_shape=out_shape,
        compiler_params=_cparams("arbitrary"),
        name="norm_mod_t" if transpose else "norm_mod",
    )(x2d, gain.reshape(1, d), sc, sh)


def _final_norm_kernel(x_ref, gain_ref, o_ref):
    x = x_ref[...]
    ms = jnp.mean(x * x, axis=-1, keepdims=True)
    o_ref[...] = x * lax.rsqrt(ms + EPS) * gain_ref[...]


def _final_norm(x2d, gain, tm=512):
    n, d = x2d.shape
    return pl.pallas_call(
        _final_norm_kernel,
        grid=(n // tm,),
        in_specs=[pl.BlockSpec((tm, d), lambda i: (i, 0)),
                  pl.BlockSpec((1, d), lambda i: (0, 0))],
        out_specs=pl.BlockSpec((tm, d), lambda i: (i, 0)),
        out_shape=jax.ShapeDtypeStruct((n, d), F32),
        compiler_params=_cparams("arbitrary"),
        name="final_norm",
    )(x2d, gain.reshape(1, d))


def _mm_kernel(a_ref, w_ref, o_ref, *, act):
    acc = jnp.dot(a_ref[...], w_ref[...], preferred_element_type=F32)
    if act == "sigmoid":
        acc = jax.nn.sigmoid(acc)
    o_ref[...] = acc.astype(o_ref.dtype)


def _matmul(a, w, *, col0, ncols, out_dtype, act=None, tm, tn, name, layer=None):
    m, k = a.shape
    off = col0 // tn
    if layer is None:
        w_spec = pl.BlockSpec((k, tn), lambda i, j: (0, j + off))
    else:
        w_spec = pl.BlockSpec((None, k, tn), lambda i, j: (layer, 0, j + off))
    return pl.pallas_call(
        functools.partial(_mm_kernel, act=act),
        grid=(m // tm, ncols // tn),
        in_specs=[pl.BlockSpec((tm, k), lambda i, j: (i, 0)), w_spec],
        out_specs=pl.BlockSpec((tm, tn), lambda i, j: (i, j)),
        out_shape=jax.ShapeDtypeStruct((m, ncols), out_dtype),
        compiler_params=_cparams("arbitrary", "arbitrary"),
        name=name,
    )(a, w)


def _merge_kernel(ys_ref, ya_ref, ws_ref, wa_ref, gs_ref, ga_ref, o_ref):
    ps = jnp.dot(ys_ref[...], ws_ref[...], preferred_element_type=F32)
    pa = jnp.dot(ya_ref[...], wa_ref[...], preferred_element_type=F32)
    o_ref[...] = (gs_ref[...].astype(F32) * ps
                  + ga_ref[...].astype(F32) * pa).astype(o_ref.dtype)


def _merge(ys, ya, ws, wa, gates, tm=1024, tn=1024):
    n, ks = ys.shape
    ka = ya.shape[1]
    d = ws.shape[1]
    goff = d // tn
    return pl.pallas_call(
        _merge_kernel,
        grid=(n // tm, d // tn),
        in_specs=[
            pl.BlockSpec((tm, ks), lambda i, j: (i, 0)),
            pl.BlockSpec((tm, ka), lambda i, j: (i, 0)),
            pl.BlockSpec((ks, tn), lambda i, j: (0, j)),
            pl.BlockSpec((ka, tn), lambda i, j: (0, j)),
            pl.BlockSpec((tm, tn), lambda i, j: (i, j)),
            pl.BlockSpec((tm, tn), lambda i, j: (i, j + goff)),
        ],
        out_specs=pl.BlockSpec((tm, tn), lambda i, j: (i, j)),
        out_shape=jax.ShapeDtypeStruct((n, d), BF16),
        compiler_params=_cparams("arbitrary", "arbitrary"),
        name="branch_merge",
    )(ys, ya, ws, wa, gates, gates)


def _outproj_kernel(a_ref, w_ref, x_ref, g_ref, o_ref):
    acc = jnp.dot(a_ref[...], w_ref[...], preferred_element_type=F32)
    o_ref[...] = x_ref[...] + g_ref[0] * acc


def _outproj(a, w, x2d, g, seq, tm=1024, tn=1024):
    n, k = a.shape
    d = w.shape[1]
    per_batch = seq // tm
    return pl.pallas_call(
        _outproj_kernel,
        grid=(n // tm, d // tn),
        in_specs=[
            pl.BlockSpec((tm, k), lambda i, j: (i, 0)),
            pl.BlockSpec((k, tn), lambda i, j: (0, j)),
            pl.BlockSpec((tm, tn), lambda i, j: (i, j)),
            pl.BlockSpec((1, 1, tn), lambda i, j: (i // per_batch, 0, j)),
        ],
        out_specs=pl.BlockSpec((tm, tn), lambda i, j: (i, j)),
        out_shape=jax.ShapeDtypeStruct((n, d), F32),
        compiler_params=_cparams("arbitrary", "arbitrary"),
        name="out_proj",
    )(a, w, x2d, g)


def _s5_prep_kernel(lr_ref, li_ref, ldt_ref, br_ref, bi_ref,
                    are_ref, aim_ref, bbr_ref, bbi_ref, atr_ref, ati_ref, *, ts):
    lr, li = lr_ref[...], li_ref[...]
    dt = jnp.exp(ldt_ref[...])
    mag = jnp.exp(lr * dt)
    a_re = mag * jnp.cos(li * dt)
    a_im = mag * jnp.sin(li * dt)
    inv = 1.0 / (lr * lr + li * li)
    coef_re = ((a_re - 1.0) * lr + a_im * li) * inv
    coef_im = (a_im * lr - (a_re - 1.0) * li) * inv
    br, bi = br_ref[...], bi_ref[...]
    bbr_ref[...] = coef_re * br - coef_im * bi
    bbi_ref[...] = coef_re * bi + coef_im * br
    are_ref[...] = a_re
    aim_ref[...] = a_im
    pr, pi = a_re, a_im
    for _ in range(ts - 1):
        pr, pi = pr * a_re - pi * a_im, pr * a_im + pi * a_re
    atr_ref[...] = pr
    ati_ref[...] = pi


def _s5_prep(lam_re, lam_im, log_dt, b_re, b_im, ts):
    g, p, c = b_re.shape
    ns = g * p
    row = lambda t: t.reshape(1, ns)
    ldt = jnp.broadcast_to(log_dt[:, None], (g, p))
    bt = lambda t: jnp.transpose(t, (2, 0, 1)).reshape(c, ns)
    shapes = (
        jax.ShapeDtypeStruct((1, ns), F32), jax.ShapeDtypeStruct((1, ns), F32),
        jax.ShapeDtypeStruct((c, ns), F32), jax.ShapeDtypeStruct((c, ns), F32),
        jax.ShapeDtypeStruct((1, ns), F32), jax.ShapeDtypeStruct((1, ns), F32),
    )
    return pl.pallas_call(
        functools.partial(_s5_prep_kernel, ts=ts),
        out_shape=shapes,
        name="s5_prep",
    )(row(lam_re), row(lam_im), row(ldt), bt(b_re), bt(b_im))


def _s5_block_weights(bbar_re, bbar_im, c_re, c_im):
    c, ns = bbar_re.shape
    gpb = S5_CH_BLOCK // c
    nk = ns // (gpb * SSM_STATE)
    eye = jnp.eye(gpb, dtype=F32)

    def bpart(t):
        t = t.reshape(c, nk, gpb, SSM_STATE)
        t = jnp.einsum("ckjp,ij->kicjp", t, eye)
        return t.reshape(nk, gpb * c, gpb * SSM_STATE)

    def cpart(t):
        t = t.reshape(nk, gpb, c, SSM_STATE)
        t = jnp.einsum("kicp,ij->kjpic", t, eye)
        return t.reshape(nk, gpb * SSM_STATE, gpb * c)

    bblk = jnp.concatenate([bpart(bbar_re), bpart(bbar_im)], axis=2).astype(BF16)
    cblk = jnp.concatenate([cpart(c_re), cpart(-c_im)], axis=1).astype(BF16)
    return bblk, cblk


def _s5_kernel(u_ref, pm_ref, pmt_ref, bblk_ref, are_ref, aim_ref, atr_ref, ati_ref,
               cblk_ref, d_ref, wglu_ref, o_ref,
               hr, hi, car_r, car_i, cs_r, cs_i, yp, *, ts, nk):
    lw = S5_LANE_BLOCK
    cw = S5_CH_BLOCK

    @pl.when(pl.program_id(1) == 0)
    def _():
        car_r[...] = jnp.zeros_like(car_r)
        car_i[...] = jnp.zeros_like(car_i)

    u = u_ref[...]
    up = jnp.dot(pm_ref[...], u.astype(BF16), preferred_element_type=F32).astype(BF16)
    for k in range(nk):
        bu = jnp.dot(up[:, k * cw:(k + 1) * cw], bblk_ref[k], preferred_element_type=F32)
        hr[:, k * lw:(k + 1) * lw] = bu[:, :lw]
        hi[:, k * lw:(k + 1) * lw] = bu[:, lw:]

    for k in range(nk):
        sl = pl.ds(k * lw, lw)
        ar = are_ref[:, sl]
        ai = aim_ref[:, sl]

        def end_step(j, carry, sl=sl, ar=ar, ai=ai):
            cr, cim = carry
            rows = pl.ds(pl.multiple_of(j * SUBLANES, SUBLANES), SUBLANES)
            return ar * cr - ai * cim + hr[rows, sl], ar * cim + ai * cr + hi[rows, sl]

        zero = jnp.zeros((SUBLANES, lw), F32)
        end_r, end_i = lax.fori_loop(0, ts, end_step, (zero, zero), unroll=S5_UNROLL)
        cs_r[:, sl] = end_r
        cs_i[:, sl] = end_i

    c_r = car_r[...]
    c_i = car_i[...]
    at_r = atr_ref[...]
    at_i = ati_ref[...]
    for s in range(SUBLANES):
        l_r = cs_r[s:s + 1, :]
        l_i = cs_i[s:s + 1, :]
        cs_r[s:s + 1, :] = c_r
        cs_i[s:s + 1, :] = c_i
        c_r, c_i = l_r + at_r * c_r - at_i * c_i, l_i + at_r * c_i + at_i * c_r
    car_r[...] = c_r
    car_i[...] = c_i

    for k in range(nk):
        sl = pl.ds(k * lw, lw)
        ar = are_ref[:, sl]
        ai = aim_ref[:, sl]

        def scan_step(j, carry, sl=sl, ar=ar, ai=ai):
            cr, cim = carry
            rows = pl.ds(pl.multiple_of(j * SUBLANES, SUBLANES), SUBLANES)
            nr = ar * cr - ai * cim + hr[rows, sl]
            ni = ar * cim + ai * cr + hi[rows, sl]
            hr[rows, sl] = nr
            hi[rows, sl] = ni
            return nr, ni

        lax.fori_loop(0, ts, scan_step, (cs_r[:, sl], cs_i[:, sl]), unroll=S5_UNROLL)
        hcat = jnp.concatenate([hr[:, sl], hi[:, sl]], axis=1).astype(BF16)
        yp[:, k * cw:(k + 1) * cw] = jnp.dot(hcat, cblk_ref[k], preferred_element_type=F32)

    y1, y2, y3 = _split3(yp[...])
    pmt = pmt_ref[...]
    y = (jnp.dot(pmt, y1, preferred_element_type=F32)
         + jnp.dot(pmt, y2, preferred_element_type=F32)
         + jnp.dot(pmt, y3, preferred_element_type=F32))
    y = y + d_ref[...] * u
    y = _gelu(y)
    gate = jax.nn.sigmoid(jnp.dot(y.astype(BF16), wglu_ref[...], preferred_element_type=F32))
    o_ref[...] = (y * gate).astype(o_ref.dtype)


def _s5_permutation(t_rows, ts):
    r = np.arange(t_rows)
    src = (r % SUBLANES) * ts + r // SUBLANES
    pm = np.zeros((t_rows, t_rows), np.float32)
    pm[r, src] = 1.0
    return jnp.asarray(pm, BF16), jnp.asarray(pm.T, BF16)


def _s5_branch(u, seq, a_re, a_im, at_re, at_im, bblk, cblk, d_skip, w_glu):
    n, w = u.shape
    ns = a_re.shape[1]
    t_rows = S5_CHUNK
    ts = t_rows // SUBLANES
    nk = ns // S5_LANE_BLOCK
    nchunk = seq // t_rows
    pm, pmt = _s5_permutation(t_rows, ts)
    full = lambda shape: pl.BlockSpec(shape, lambda b, c: (0,) * len(shape))
    rep8 = lambda row: jnp.broadcast_to(row, (SUBLANES, ns))
    return pl.pallas_call(
        functools.partial(_s5_kernel, ts=ts, nk=nk),
        grid=(n // seq, nchunk),
        in_specs=[
            pl.BlockSpec((t_rows, w), lambda b, c: (b * nchunk + c, 0)),
            full((t_rows, t_rows)), full((t_rows, t_rows)),
            full(bblk.shape),
            full((SUBLANES, ns)), full((SUBLANES, ns)), full((1, ns)), full((1, ns)),
            full(cblk.shape),
            full((1, w)), full((w, w)),
        ],
        out_specs=pl.BlockSpec((t_rows, w), lambda b, c: (b * nchunk + c, 0)),
        out_shape=jax.ShapeDtypeStruct((n, w), BF16),
        scratch_shapes=[
            pltpu.VMEM((t_rows, ns), F32), pltpu.VMEM((t_rows, ns), F32),
            pltpu.VMEM((1, ns), F32), pltpu.VMEM((1, ns), F32),
            pltpu.VMEM((SUBLANES, ns), F32), pltpu.VMEM((SUBLANES, ns), F32),
            pltpu.VMEM((t_rows, w), F32),
        ],
        compiler_params=_cparams("arbitrary", "arbitrary"),
        name="s5_branch",
    )(u, pm, pmt, bblk, rep8(a_re), rep8(a_im), at_re, at_im, cblk, d_skip.reshape(1, w), w_glu)


def _attn_kernel(q_ref, k_ref, v_ref, o_ref, o_s, m_s, l_s, *, seq, patterns, scale):
    blk = ATTN_BLOCK
    qi = lax.broadcasted_iota(jnp.int32, (blk, 2 * blk), 0)
    kj = lax.broadcasted_iota(jnp.int32, (blk, 2 * blk), 1)
    dist = blk + qi - kj
    nt = (((1,), (1,)), ((), ()))

    for pi, (window, dil) in enumerate(patterns):
        n_back = window // dil
        nblk = seq // dil // blk
        grp = min(ATTN_GROUP, nblk)
        ngrp = nblk // grp
        band = (dist >= 0) & (dist <= n_back)

        nres = max(1, ATTN_GROUP // grp)

        def group_step(idx, carry, pi=pi, dil=dil, grp=grp, ngrp=ngrp, nres=nres, band=band):
            r0 = (idx // ngrp) * nres
            n0 = (idx % ngrp) * grp

            def rows_of(r, nb):
                start = r + nb * (blk * dil)
                return pl.ds(start, blk) if dil == 1 else pl.ds(start, blk, stride=dil)

            chains = [[rows_of(r0 + ri, jnp.maximum(n0 - 1, 0))]
                      + [rows_of(r0 + ri, n0 + gi) for gi in range(grp)] for ri in range(nres)]
            units = [(ri, gi) for ri in range(nres) for gi in range(grp)]
            ks = [[k_ref[rw, :].astype(BF16) for rw in rows] for rows in chains]
            scores = []
            for ri, gi in units:
                q = q_ref[chains[ri][gi + 1], :].astype(BF16)
                scores.append(jnp.concatenate(
                    [lax.dot_general(q, ks[ri][gi], nt, preferred_element_type=F32),
                     lax.dot_general(q, ks[ri][gi + 1], nt, preferred_element_type=F32)], axis=1))
            probs = []
            for (ri, gi), sc in zip(units, scores):
                rows = chains[ri][gi + 1]
                mask = band & ((kj >= blk) | (n0 > 0)) if gi == 0 else band
                s = jnp.where(mask, sc * scale, -jnp.inf)
                m = jnp.max(s, axis=1, keepdims=True)
                p = jnp.exp(s - m)
                l = jnp.sum(p, axis=1, keepdims=True)
                m_s[pi, rows, :] = jnp.broadcast_to(m, (blk, HEAD_DIM))
                l_s[pi, rows, :] = jnp.broadcast_to(l, (blk, HEAD_DIM))
                probs.append(p.astype(BF16))
            vs = [[v_ref[rw, :].astype(BF16) for rw in rows] for rows in chains]
            for (ri, gi), pb in zip(units, probs):
                o_s[pi, chains[ri][gi + 1], :] = (
                    jnp.dot(pb[:, :blk], vs[ri][gi], preferred_element_type=F32)
                    + jnp.dot(pb[:, blk:], vs[ri][gi + 1], preferred_element_type=F32))
            return carry

        lax.fori_loop(0, (dil // nres) * ngrp, group_step, 0)

    npat = len(patterns)

    def merge_step(t, carry):
        rows = pl.ds(pl.multiple_of(t * blk, blk), blk)
        ms = [m_s[pi, rows, :] for pi in range(npat)]
        mx = functools.reduce(jnp.maximum, ms)
        num = jnp.zeros((blk, HEAD_DIM), F32)
        den = jnp.zeros((blk, HEAD_DIM), F32)
        for pi in range(npat):
            w = jnp.exp(ms[pi] - mx)
            num = num + w * o_s[pi, rows, :]
            den = den + w * l_s[pi, rows, :]
        o_ref[rows, :] = (num / den).astype(o_ref.dtype)
        return carry

    lax.fori_loop(0, seq // blk, merge_step, 0)


def _dilated_attention(qkv, seq):
    n = qkv.shape[0]
    nh = qkv.shape[1] // (3 * HEAD_DIM)
    for window, dil in DILATED_PATTERNS:
        assert window // dil <= ATTN_BLOCK and seq % (dil * ATTN_BLOCK) == 0
    blk = (seq, HEAD_DIM)
    return pl.pallas_call(
        functools.partial(_attn_kernel, seq=seq, patterns=DILATED_PATTERNS,
                          scale=np.float32(HEAD_DIM ** -0.5)),
        grid=(n // seq, nh),
        in_specs=[pl.BlockSpec(blk, lambda b, h: (b, h)),
                  pl.BlockSpec(blk, lambda b, h: (b, nh + h)),
                  pl.BlockSpec(blk, lambda b, h: (b, 2 * nh + h))],
        out_specs=pl.BlockSpec(blk, lambda b, h: (b, h)),
        out_shape=jax.ShapeDtypeStruct((n, nh * HEAD_DIM), BF16),
        scratch_shapes=[pltpu.VMEM((len(DILATED_PATTERNS),) + blk, F32)] * 3,
        compiler_params=_cparams("arbitrary", "arbitrary"),
        name="dilated_attn",
    )(qkv, qkv, qkv)


def _top16(s, kid, exact_ties):
    nkeys = s.shape[0]
    rank = jnp.full(s.shape, float(nkeys), F32)
    vals = []
    for it in range(PEER_TOPK):
        m = jnp.max(s, axis=0, keepdims=True)
        if exact_ties:
            idx = jnp.min(jnp.where(s == m, kid, float(nkeys)), axis=0, keepdims=True)
            sel = kid == idx
        else:
            sel = s == m
        rank = jnp.where(sel, float(it), rank)
        s = jnp.where(sel, -jnp.inf, s)
        vals.append(m)
    return rank, jnp.concatenate(vals, axis=0)


def _router_kernel(q_ref, sk_ref, p1_ref, n_ref, r2_ref, p2_ref, *, nsub):
    nk = PEER_NKEYS
    kt = PEER_TOPK
    half = kt // 2
    kid = lax.broadcasted_iota(jnp.int32, (nk, LANES), 0).astype(F32)
    ia = lax.broadcasted_iota(jnp.int32, (kt, LANES), 0).astype(F32)
    ih = ia[0:half]
    cid = jnp.concatenate([ia * kt] + [ih * kt + float(b) for b in range(1, half)] + [ih + float(half)],
                          axis=0)
    ncand = kt * kt
    tail0 = kt + (half - 1) * half

    def route(lanes, exact_ties):
        q1 = q_ref[0:nk, lanes].astype(BF16)
        q2 = q_ref[nk:2 * nk, lanes].astype(BF16)
        s1 = jnp.dot(sk_ref[0], q1, preferred_element_type=F32)
        s2 = jnp.dot(sk_ref[1], q2, preferred_element_type=F32)
        rank1, v1 = _top16(s1, kid, exact_ties)
        rank2, v2 = _top16(s2, kid, exact_ties)
        work = jnp.concatenate([v1 + v2[0:1]] + [v1[0:half] + v2[b:b + 1] for b in range(1, half)]
                               + [v1[0:1] + v2[half:kt]], axis=0)
        m0 = v1[0:1] + v2[0:1]
        z = jnp.zeros((1, LANES), F32)
        if exact_ties:
            n_a = jnp.zeros((kt, LANES), F32)
            for _ in range(kt):
                m = jnp.max(work, axis=0, keepdims=True)
                idx = jnp.min(jnp.where(work == m, cid, float(ncand)), axis=0, keepdims=True)
                work = jnp.where(cid == idx, -jnp.inf, work)
                a_sel = jnp.floor(idx * (1.0 / kt))
                n_a = n_a + jnp.where(ia == a_sel, 1.0, 0.0)
                z = z + jnp.exp(m - m0)
        else:
            for _ in range(kt):
                m = jnp.max(work, axis=0, keepdims=True)
                work = jnp.where(work == m, -jnp.inf, work)
                z = z + jnp.exp(m - m0)
            mark = jnp.where(work == -jnp.inf, 1.0, 0.0)
            low = mark[0:half]
            for b in range(1, half):
                low = low + mark[kt + (b - 1) * half:kt + b * half]
            tail = jnp.sum(mark[tail0:], axis=0, keepdims=True)
            n_a = jnp.concatenate([low + jnp.where(ih == 0.0, tail, 0.0), mark[half:kt]], axis=0)
        nkey = jnp.zeros((nk, LANES), F32)
        for a in range(kt):
            nkey = jnp.where(rank1 == float(a), n_a[a:a + 1], nkey)
        p1_ref[0, :, lanes] = jnp.exp(s1 - v1[0:1])
        n_ref[0, :, lanes] = nkey
        r2_ref[0, :, lanes] = rank2.astype(r2_ref.dtype)
        p2_ref[0, :, lanes] = (jnp.exp(s2 - v2[0:1]) / z).astype(p2_ref.dtype)
        if exact_ties:
            return None
        ranked = lambda r: jnp.sum(jnp.where(r < float(nk), 1.0, 0.0), axis=0, keepdims=True)
        ok = ((ranked(rank1) == float(kt)) & (ranked(rank2) == float(kt))
              & (jnp.sum(n_a, axis=0, keepdims=True) == float(kt)))
        return jnp.min(jnp.where(ok, 1.0, 0.0))

    def sub(st, carry):
        lanes = pl.ds(pl.multiple_of(st * LANES, LANES), LANES)
        all_distinct = route(lanes, False)

        @pl.when(all_distinct < 0.5)
        def _():
            route(lanes, True)

        return carry

    lax.fori_loop(0, nsub, sub, 0)


def _router(qt, sk, tt=512):
    dq2, n = qt.shape
    nh = PEER_HEADS
    nk = PEER_NKEYS
    out = jax.ShapeDtypeStruct((nh, nk, n), F32)
    out_b = jax.ShapeDtypeStruct((nh, nk, n), BF16)
    ospec = pl.BlockSpec((1, nk, tt), lambda i, h: (h, 0, i))
    return pl.pallas_call(
        functools.partial(_router_kernel, nsub=tt // LANES),
        grid=(n // tt, nh),
        in_specs=[pl.BlockSpec((2 * nk, tt), lambda i, h: (h, i)),
                  pl.BlockSpec((2, nk, nk), lambda i, h: (0, 0, 0))],
        out_specs=(ospec, ospec, ospec, ospec),
        out_shape=(out, out, out_b, out_b),
        compiler_params=_cparams("arbitrary", "arbitrary"),
        name="peer_router",
    )(qt, sk)


def _peer_kernel(ht_ref, u_ref, vt_ref, p1_ref, n_ref, r2_ref, p2_ref, x_ref, g_ref, o_ref,
                 acc, a_s, gbuf, *, ne1, nh):
    j = pl.program_id(1)
    nk = PEER_NKEYS

    @pl.when(j == 0)
    def _():
        acc[...] = jnp.zeros_like(acc)

    a_s[...] = jnp.dot(u_ref[...], ht_ref[...], preferred_element_type=F32)

    def per_e1(e, carry):
        rows = pl.ds(pl.multiple_of(e * nk, nk), nk)
        act = _gelu(a_s[rows, :])
        w = jnp.zeros(act.shape, BF16)
        for h in range(nh):
            p1 = jnp.broadcast_to(p1_ref[h, pl.ds(e, 1), :], act.shape).astype(BF16)
            cnt = jnp.broadcast_to(n_ref[h, pl.ds(e, 1), :], act.shape).astype(BF16)
            w = w + p1 * jnp.where(r2_ref[h] < cnt, p2_ref[h], jnp.zeros((), BF16))
        gbuf[rows, :] = act.astype(BF16) * w
        return carry

    lax.fori_loop(0, ne1, per_e1, 0)
    acc[...] += jnp.dot(vt_ref[...], gbuf[...], preferred_element_type=F32)

    @pl.when(j == pl.num_programs(1) - 1)
    def _():
        o_ref[...] = x_ref[...] + g_ref[0] * acc[...].T


def _peer(ht, u_tab, vt_tab, layer, p1, cnt, r2, p2, x2d, g, seq, tt=512, eb=1024):
    d, n = ht.shape
    ne = u_tab.shape[1]
    nh, nk, _ = p1.shape
    ne1 = eb // nk
    per_batch = seq // tt
    tok = pl.BlockSpec((nh, nk, tt), lambda i, j: (0, 0, i))
    e1b = pl.BlockSpec((nh, ne1, tt), lambda i, j: (0, j, i))
    return pl.pallas_call(
        functools.partial(_peer_kernel, ne1=ne1, nh=nh),
        grid=(n // tt, ne // eb),
        in_specs=[
            pl.BlockSpec((d, tt), lambda i, j: (0, i)),
            pl.BlockSpec((None, eb, d), lambda i, j: (layer, j, 0)),
            pl.BlockSpec((None, d, eb), lambda i, j: (layer, 0, j)),
            e1b, e1b, tok, tok,
            pl.BlockSpec((tt, d), lambda i, j: (i, 0)),
            pl.BlockSpec((1, 1, d), lambda i, j: (i // per_batch, 0, 0)),
        ],
        out_specs=pl.BlockSpec((tt, d), lambda i, j: (i, 0)),
        out_shape=jax.ShapeDtypeStruct((n, d), F32),
        scratch_shapes=[pltpu.VMEM((d, tt), F32), pltpu.VMEM((eb, tt), F32),
                        pltpu.VMEM((eb, tt), BF16)],
        compiler_params=_cparams("arbitrary", "arbitrary", vmem=60 * 1024 * 1024),
        name="peer_experts",
    )(ht, u_tab, vt_tab, p1, cnt, r2, p2, x2d, g)


def kernel(x, c, w_ada, b_ada, norm_mix, norm_ffn, w_in, ssm_lambda_re, ssm_lambda_im, ssm_log_dt, ssm_b_re, ssm_b_im, ssm_c_re, ssm_c_im, ssm_d, w_glu, w_br_ssm, w_br_attn, w_out, peer_wq, peer_subkeys, peer_u, peer_v, norm_final):
    bsz, seq, d = x.shape
    depth = w_ada.shape[0]
    n = bsz * seq
    ssm_w = ssm_d.shape[1]
    attn_w = w_br_attn.shape[1]
    assert bsz <= SUBLANES and seq % S5_CHUNK == 0

    c_pad = jnp.pad(c, ((0, SUBLANES - bsz), (0, 0)))
    mod_all = _modulation(c_pad, w_ada, b_ada)
    x2d = x.reshape(n, d)
    ts = S5_CHUNK // SUBLANES
    w_in_all = w_in.astype(BF16)
    u_all = peer_u.astype(BF16)
    vt_all = jnp.swapaxes(peer_v, 1, 2).astype(BF16)

    for l in range(depth):
        mods = [mod_all[l, :bsz, i * d:(i + 1) * d].reshape(bsz, 1, d) for i in range(N_MOD)]
        sh_m, sc_m, g_m, sh_f, sc_f, g_f = mods

        h = _norm_mod(x2d, norm_mix[l], sc_m, sh_m, seq, transpose=False)
        u = _matmul(h, w_in_all, layer=l, col0=0, ncols=ssm_w, out_dtype=F32, tm=1024, tn=1024,
                    name="in_proj_u")
        qkv = _matmul(h, w_in_all, layer=l, col0=ssm_w, ncols=3 * attn_w, out_dtype=F32,
                      tm=1024, tn=1024, name="in_proj_qkv")
        gates = _matmul(h, w_in_all, layer=l, col0=ssm_w + 3 * attn_w, ncols=2 * d, out_dtype=BF16,
                        act="sigmoid", tm=1024, tn=1024, name="in_proj_gates")

        a_re, a_im, bbar_re, bbar_im, at_re, at_im = _s5_prep(
            ssm_lambda_re[l], ssm_lambda_im[l], ssm_log_dt[l], ssm_b_re[l], ssm_b_im[l], ts)
        bblk, cblk = _s5_block_weights(bbar_re, bbar_im, ssm_c_re[l], ssm_c_im[l])
        y_s = _s5_branch(u, seq, a_re, a_im, at_re, at_im, bblk, cblk, ssm_d[l],
                         w_glu[l].astype(BF16))
        y_a = _dilated_attention(qkv, seq)
        merged = _merge(y_s, y_a, w_br_ssm[l].astype(BF16), w_br_attn[l].astype(BF16), gates)
        x2d = _outproj(merged, w_out[l].astype(BF16), x2d, g_m, seq)

        ht = _norm_mod(x2d, norm_ffn[l], sc_f, sh_f, seq, transpose=True)
        qt = _matmul(peer_wq[l].T.astype(BF16), ht, col0=0, ncols=n, out_dtype=F32,
                     tm=1024, tn=1024, name="peer_query")
        p1, cnt, r2, p2 = _router(qt, peer_subkeys[l].astype(BF16))
        x2d = _peer(ht, u_all, vt_all, l, p1, cnt, r2, p2, x2d, g_f, seq)

    return _final_norm(x2d, norm_final).reshape(bsz, seq, d)
```

```python
import functools
import math

import jax
import jax.numpy as jnp
import numpy as np
from jax import lax
from jax.experimental import pallas as pl
from jax.experimental.pallas import tpu as pltpu

F32 = jnp.float32
BF16 = jnp.bfloat16

EPS = 1e-6
N_MOD = 6
SSM_GROUP = 16
SSM_STATE = 64
HEAD_DIM = 128
ATTN_BLOCK = 128
ATTN_GROUP = 8
DILATED_PATTERNS = ((128, 1), (512, 4), (2048, 16))
PEER_HEADS = 8
PEER_NKEYS = 128
PEER_TOPK = 16
ROUTER_GROUPS = 2

LANES = 128
SUBLANES = 8
S5_CHUNK = 256
S5_LANE_BLOCK = 512
S5_CH_BLOCK = 128
S5_UNROLL = 4
VMEM_LIMIT = 56 * 1024 * 1024


def _cparams(*sem, vmem=VMEM_LIMIT):
    return pltpu.CompilerParams(dimension_semantics=sem, vmem_limit_bytes=vmem)


def _gelu_x2(x):
    return x * (1.0 + lax.erf(x * np.float32(math.sqrt(0.5))))


def _gelu(x):
    return 0.5 * _gelu_x2(x)


def _split3(x):
    hi = x.astype(BF16)
    r1 = x - hi.astype(F32)
    mid = r1.astype(BF16)
    lo = (r1 - mid.astype(F32)).astype(BF16)
    return hi, mid, lo


def _mod_kernel(c_ref, w_ref, b_ref, o_ref):
    c = c_ref[...]
    ca = c * jax.nn.sigmoid(c)
    w = w_ref[0]
    c_hi = ca.astype(BF16)
    c_lo = (ca - c_hi.astype(F32)).astype(BF16)
    w_hi = w.astype(BF16)
    w_lo = (w - w_hi.astype(F32)).astype(BF16)
    acc = jnp.dot(c_hi, w_hi, preferred_element_type=F32)
    acc += jnp.dot(c_lo, w_hi, preferred_element_type=F32)
    acc += jnp.dot(c_hi, w_lo, preferred_element_type=F32)
    o_ref[0] = acc + b_ref[0]


def _modulation(c_pad, w_ada, b_ada, tn=1024):
    depth, d, cols = w_ada.shape
    return pl.pallas_call(
        _mod_kernel,
        grid=(depth, cols // tn),
        in_specs=[
            pl.BlockSpec((SUBLANES, d), lambda l, j: (0, 0)),
            pl.BlockSpec((1, d, tn), lambda l, j: (l, 0, j)),
            pl.BlockSpec((1, 1, tn), lambda l, j: (l, 0, j)),
        ],
        out_specs=pl.BlockSpec((1, SUBLANES, tn), lambda l, j: (l, 0, j)),
        out_shape=jax.ShapeDtypeStruct((depth, SUBLANES, cols), F32),
        compiler_params=_cparams("arbitrary", "arbitrary"),
        name="adaln_mod",
    )(c_pad, w_ada, b_ada.reshape(depth, 1, cols))


def _norm_mod_kernel(x_ref, gain_ref, sc_ref, sh_ref, o_ref, *, transpose):
    x = x_ref[...]
    ms = jnp.mean(x * x, axis=-1, keepdims=True)
    y = x * lax.rsqrt(ms + EPS) * gain_ref[...]
    h = y * (1.0 + sc_ref[0]) + sh_ref[0]
    if transpose:
        o_ref[...] = h.T.astype(o_ref.dtype)
    else:
        o_ref[...] = h.astype(o_ref.dtype)


def _norm_mod(x2d, gain, sc, sh, seq, *, transpose, tm=512):
    n, d = x2d.shape
    per_batch = seq // tm
    if transpose:
        out_spec = pl.BlockSpec((d, tm), lambda i: (0, i))
        out_shape = jax.ShapeDtypeStruct((d, n), BF16)
    else:
        out_spec = pl.BlockSpec((tm, d), lambda i: (i, 0))
        out_shape = jax.ShapeDtypeStruct((n, d), BF16)
    return pl.pallas_call(
        functools.partial(_norm_mod_kernel, transpose=transpose),
        grid=(n // tm,),
        in_specs=[
            pl.BlockSpec((tm, d), lambda i: (i, 0)),
            pl.BlockSpec((1, d), lambda i: (0, 0)),
            pl.BlockSpec((1, 1, d), lambda i: (i // per_batch, 0, 0)),
            pl.BlockSpec((1, 1, d), lambda i: (i // per_batch, 0, 0)),
        ],
        out_specs=out_spec,
        out_shape=out_shape,
        compiler_params=_cparams("arbitrary"),
        name="norm_mod_t" if transpose else "norm_mod",
    )(x2d, gain.reshape(1, d), sc, sh)


def _final_norm_kernel(x_ref, gain_ref, o_ref):
    x = x_ref[...]
    ms = jnp.mean(x * x, axis=-1, keepdims=True)
    o_ref[...] = x * lax.rsqrt(ms + EPS) * gain_ref[...]


def _final_norm(x2d, gain, tm=512):
    n, d = x2d.shape
    return pl.pallas_call(
        _final_norm_kernel,
        grid=(n // tm,),
        in_specs=[pl.BlockSpec((tm, d), lambda i: (i, 0)),
                  pl.BlockSpec((1, d), lambda i: (0, 0))],
        out_specs=pl.BlockSpec((tm, d), lambda i: (i, 0)),
        out_shape=jax.ShapeDtypeStruct((n, d), F32),
        compiler_params=_cparams("arbitrary"),
        name="final_norm",
    )(x2d, gain.reshape(1, d))


def _mm_kernel(a_ref, w_ref, o_ref, *, act):
    acc = jnp.dot(a_ref[...], w_ref[...], preferred_element_type=F32)
    if act == "sigmoid":
        acc = jax.nn.sigmoid(acc)
    o_ref[...] = acc.astype(o_ref.dtype)


def _matmul(a, w, *, col0, ncols, out_dtype, act=None, tm, tn, name, layer=None):
    m, k = a.shape
    off = col0 // tn
    if layer is None:
        w_spec = pl.BlockSpec((k, tn), lambda i, j: (0, j + off))
    else:
        w_spec = pl.BlockSpec((None, k, tn), lambda i, j: (layer, 0, j + off))
    return pl.pallas_call(
        functools.partial(_mm_kernel, act=act),
        grid=(m // tm, ncols // tn),
        in_specs=[pl.BlockSpec((tm, k), lambda i, j: (i, 0)), w_spec],
        out_specs=pl.BlockSpec((tm, tn), lambda i, j: (i, j)),
        out_shape=jax.ShapeDtypeStruct((m, ncols), out_dtype),
        compiler_params=_cparams("arbitrary", "arbitrary"),
        name=name,
    )(a, w)


def _merge_kernel(ys_ref, ya_ref, ws_ref, wa_ref, gs_ref, ga_ref, o_ref):
    ps = jnp.dot(ys_ref[...], ws_ref[...], preferred_element_type=F32)
    pa = jnp.dot(ya_ref[...], wa_ref[...], preferred_element_type=F32)
    o_ref[...] = (gs_ref[...].astype(F32) * ps
                  + ga_ref[...].astype(F32) * pa).astype(o_ref.dtype)


def _merge(ys, ya, ws, wa, gates, tm=1024, tn=1024):
    n, ks = ys.shape
    ka = ya.shape[1]
    d = ws.shape[1]
    goff = d // tn
    return pl.pallas_call(
        _merge_kernel,
        grid=(n // tm, d // tn),
        in_specs=[
            pl.BlockSpec((tm, ks), lambda i, j: (i, 0)),
            pl.BlockSpec((tm, ka), lambda i, j: (i, 0)),
            pl.BlockSpec((ks, tn), lambda i, j: (0, j)),
            pl.BlockSpec((ka, tn), lambda i, j: (0, j)),
            pl.BlockSpec((tm, tn), lambda i, j: (i, j)),
            pl.BlockSpec((tm, tn), lambda i, j: (i, j + goff)),
        ],
        out_specs=pl.BlockSpec((tm, tn), lambda i, j: (i, j)),
        out_shape=jax.ShapeDtypeStruct((n, d), BF16),
        compiler_params=_cparams("arbitrary", "arbitrary"),
        name="branch_merge",
    )(ys, ya, ws, wa, gates, gates)


def _outproj_kernel(a_ref, w_ref, x_ref, g_ref, o_ref):
    acc = jnp.dot(a_ref[...], w_ref[...], preferred_element_type=F32)
    o_ref[...] = x_ref[...] + g_ref[0] * acc


def _outproj(a, w, x2d, g, seq, tm=1024, tn=1024):
    n, k = a.shape
    d = w.shape[1]
    per_batch = seq // tm
    return pl.pallas_call(
        _outproj_kernel,
        grid=(n // tm, d // tn),
        in_specs=[
            pl.BlockSpec((tm, k), lambda i, j: (i, 0)),
            pl.BlockSpec((k, tn), lambda i, j: (0, j)),
            pl.BlockSpec((tm, tn), lambda i, j: (i, j)),
            pl.BlockSpec((1, 1, tn), lambda i, j: (i // per_batch, 0, j)),
        ],
        out_specs=pl.BlockSpec((tm, tn), lambda i, j: (i, j)),
        out_shape=jax.ShapeDtypeStruct((n, d), F32),
        compiler_params=_cparams("arbitrary", "arbitrary"),
        name="out_proj",
    )(a, w, x2d, g)


def _s5_prep_kernel(lr_ref, li_ref, ldt_ref, br_ref, bi_ref,
                    are_ref, aim_ref, bbr_ref, bbi_ref, atr_ref, ati_ref, *, ts):
    lr, li = lr_ref[...], li_ref[...]
    dt = jnp.exp(ldt_ref[...])
    mag = jnp.exp(lr * dt)
    a_re = mag * jnp.cos(li * dt)
    a_im = mag * jnp.sin(li * dt)
    inv = 1.0 / (lr * lr + li * li)
    coef_re = ((a_re - 1.0) * lr + a_im * li) * inv
    coef_im = (a_im * lr - (a_re - 1.0) * li) * inv
    br, bi = br_ref[...], bi_ref[...]
    bbr_ref[...] = coef_re * br - coef_im * bi
    bbi_ref[...] = coef_re * bi + coef_im * br
    are_ref[...] = a_re
    aim_ref[...] = a_im
    pr, pi = a_re, a_im
    for _ in range(ts - 1):
        pr, pi = pr * a_re - pi * a_im, pr * a_im + pi * a_re
    atr_ref[...] = pr
    ati_ref[...] = pi


def _s5_prep(lam_re, lam_im, log_dt, b_re, b_im, ts):
    g, p, c = b_re.shape
    ns = g * p
    row = lambda t: t.reshape(1, ns)
    ldt = jnp.broadcast_to(log_dt[:, None], (g, p))
    bt = lambda t: jnp.transpose(t, (2, 0, 1)).reshape(c, ns)
    shapes = (
        jax.ShapeDtypeStruct((1, ns), F32), jax.ShapeDtypeStruct((1, ns), F32),
        jax.ShapeDtypeStruct((c, ns), F32), jax.ShapeDtypeStruct((c, ns), F32),
        jax.ShapeDtypeStruct((1, ns), F32), jax.ShapeDtypeStruct((1, ns), F32),
    )
    return pl.pallas_call(
        functools.partial(_s5_prep_kernel, ts=ts),
        out_shape=shapes,
        name="s5_prep",
    )(row(lam_re), row(lam_im), row(ldt), bt(b_re), bt(b_im))


def _s5_block_weights(bbar_re, bbar_im, c_re, c_im):
    c, ns = bbar_re.shape
    gpb = S5_CH_BLOCK // c
    nk = ns // (gpb * SSM_STATE)
    eye = jnp.eye(gpb, dtype=F32)

    def bpart(t):
        t = t.reshape(c, nk, gpb, SSM_STATE)
        t = jnp.einsum("ckjp,ij->kicjp", t, eye)
        return t.reshape(nk, gpb * c, gpb * SSM_STATE)

    def cpart(t):
        t = t.reshape(nk, gpb, c, SSM_STATE)
        t = jnp.einsum("kicp,ij->kjpic", t, eye)
        return t.reshape(nk, gpb * SSM_STATE, gpb * c)

    bblk = jnp.concatenate([bpart(bbar_re), bpart(bbar_im)], axis=2).astype(BF16)
    cblk = jnp.concatenate([cpart(c_re), cpart(-c_im)], axis=1).astype(BF16)
    return bblk, cblk


def _s5_kernel(u_ref, pm_ref, pmt_ref, bblk_ref, are_ref, aim_ref, atr_ref, ati_ref,
               cblk_ref, d_ref, wglu_ref, o_ref,
               hr, hi, car_r, car_i, cs_r, cs_i, yp, *, ts, nk):
    lw = S5_LANE_BLOCK
    cw = S5_CH_BLOCK

    @pl.when(pl.program_id(1) == 0)
    def _():
        car_r[...] = jnp.zeros_like(car_r)
        car_i[...] = jnp.zeros_like(car_i)

    u = u_ref[...]
    up = jnp.dot(pm_ref[...], u.astype(BF16), preferred_element_type=F32).astype(BF16)
    for k in range(nk):
        bu = jnp.dot(up[:, k * cw:(k + 1) * cw], bblk_ref[k], preferred_element_type=F32)
        hr[:, k * lw:(k + 1) * lw] = bu[:, :lw]
        hi[:, k * lw:(k + 1) * lw] = bu[:, lw:]

    for k in range(nk):
        sl = pl.ds(k * lw, lw)
        ar = are_ref[:, sl]
        ai = aim_ref[:, sl]

        def end_step(j, carry, sl=sl, ar=ar, ai=ai):
            cr, cim = carry
            rows = pl.ds(pl.multiple_of(j * SUBLANES, SUBLANES), SUBLANES)
            return ar * cr - ai * cim + hr[rows, sl], ar * cim + ai * cr + hi[rows, sl]

        zero = jnp.zeros((SUBLANES, lw), F32)
        end_r, end_i = lax.fori_loop(0, ts, end_step, (zero, zero), unroll=S5_UNROLL)
        cs_r[:, sl] = end_r
        cs_i[:, sl] = end_i

    c_r = car_r[...]
    c_i = car_i[...]
    at_r = atr_ref[...]
    at_i = ati_ref[...]
    for s in range(SUBLANES):
        l_r = cs_r[s:s + 1, :]
        l_i = cs_i[s:s + 1, :]
        cs_r[s:s + 1, :] = c_r
        cs_i[s:s + 1, :] = c_i
        c_r, c_i = l_r + at_r * c_r - at_i * c_i, l_i + at_r * c_i + at_i * c_r
    car_r[...] = c_r
    car_i[...] = c_i

    for k in range(nk):
        sl = pl.ds(k * lw, lw)
        ar = are_ref[:, sl]
        ai = aim_ref[:, sl]

        def scan_step(j, carry, sl=sl, ar=ar, ai=ai):
            cr, cim = carry
            rows = pl.ds(pl.multiple_of(j * SUBLANES, SUBLANES), SUBLANES)
            nr = ar * cr - ai * cim + hr[rows, sl]
            ni = ar * cim + ai * cr + hi[rows, sl]
            hr[rows, sl] = nr
            hi[rows, sl] = ni
            return nr, ni

        lax.fori_loop(0, ts, scan_step, (cs_r[:, sl], cs_i[:, sl]), unroll=S5_UNROLL)
        hcat = jnp.concatenate([hr[:, sl], hi[:, sl]], axis=1).astype(BF16)
        yp[:, k * cw:(k + 1) * cw] = jnp.dot(hcat, cblk_ref[k], preferred_element_type=F32)

    y1, y2, y3 = _split3(yp[...])
    pmt = pmt_ref[...]
    y = (jnp.dot(pmt, y1, preferred_element_type=F32)
         + jnp.dot(pmt, y2, preferred_element_type=F32)
         + jnp.dot(pmt, y3, preferred_element_type=F32))
    y = y + d_ref[...] * u
    y = _gelu(y)
    gate = jax.nn.sigmoid(jnp.dot(y.astype(BF16), wglu_ref[...], preferred_element_type=F32))
    o_ref[...] = (y * gate).astype(o_ref.dtype)


def _s5_permutation(t_rows, ts):
    r = np.arange(t_rows)
    src = (r % SUBLANES) * ts + r // SUBLANES
    pm = np.zeros((t_rows, t_rows), np.float32)
    pm[r, src] = 1.0
    return jnp.asarray(pm, BF16), jnp.asarray(pm.T, BF16)


def _s5_branch(u, seq, a_re, a_im, at_re, at_im, bblk, cblk, d_skip, w_glu):
    n, w = u.shape
    ns = a_re.shape[1]
    t_rows = S5_CHUNK
    ts = t_rows // SUBLANES
    nk = ns // S5_LANE_BLOCK
    nchunk = seq // t_rows
    pm, pmt = _s5_permutation(t_rows, ts)
    full = lambda shape: pl.BlockSpec(shape, lambda b, c: (0,) * len(shape))
    rep8 = lambda row: jnp.broadcast_to(row, (SUBLANES, ns))
    return pl.pallas_call(
        functools.partial(_s5_kernel, ts=ts, nk=nk),
        grid=(n // seq, nchunk),
        in_specs=[
            pl.BlockSpec((t_rows, w), lambda b, c: (b * nchunk + c, 0)),
            full((t_rows, t_rows)), full((t_rows, t_rows)),
            full(bblk.shape),
            full((SUBLANES, ns)), full((SUBLANES, ns)), full((1, ns)), full((1, ns)),
            full(cblk.shape),
            full((1, w)), full((w, w)),
        ],
        out_specs=pl.BlockSpec((t_rows, w), lambda b, c: (b * nchunk + c, 0)),
        out_shape=jax.ShapeDtypeStruct((n, w), BF16),
        scratch_shapes=[
            pltpu.VMEM((t_rows, ns), F32), pltpu.VMEM((t_rows, ns), F32),
            pltpu.VMEM((1, ns), F32), pltpu.VMEM((1, ns), F32),
            pltpu.VMEM((SUBLANES, ns), F32), pltpu.VMEM((SUBLANES, ns), F32),
            pltpu.VMEM((t_rows, w), F32),
        ],
        compiler_params=_cparams("arbitrary", "arbitrary"),
        name="s5_branch",
    )(u, pm, pmt, bblk, rep8(a_re), rep8(a_im), at_re, at_im, cblk, d_skip.reshape(1, w), w_glu)


def _attn_kernel(q_ref, k_ref, v_ref, o_ref, o_s, m_s, l_s, *, seq, patterns, scale):
    blk = ATTN_BLOCK
    qi = lax.broadcasted_iota(jnp.int32, (blk, 2 * blk), 0)
    kj = lax.broadcasted_iota(jnp.int32, (blk, 2 * blk), 1)
    dist = blk + qi - kj
    nt = (((1,), (1,)), ((), ()))

    for pi, (window, dil) in enumerate(patterns):
        n_back = window // dil
        nblk = seq // dil // blk
        grp = min(ATTN_GROUP, nblk)
        ngrp = nblk // grp
        band = (dist >= 0) & (dist <= n_back)

        nres = max(1, ATTN_GROUP // grp)

        def group_step(idx, carry, pi=pi, dil=dil, grp=grp, ngrp=ngrp, nres=nres, band=band):
            r0 = (idx // ngrp) * nres
            n0 = (idx % ngrp) * grp

            def rows_of(r, nb):
                start = r + nb * (blk * dil)
                return pl.ds(start, blk) if dil == 1 else pl.ds(start, blk, stride=dil)

            chains = [[rows_of(r0 + ri, jnp.maximum(n0 - 1, 0))]
                      + [rows_of(r0 + ri, n0 + gi) for gi in range(grp)] for ri in range(nres)]
            units = [(ri, gi) for ri in range(nres) for gi in range(grp)]
            ks = [[k_ref[rw, :].astype(BF16) for rw in rows] for rows in chains]
            scores = []
            for ri, gi in units:
                q = q_ref[chains[ri][gi + 1], :].astype(BF16)
                scores.append(jnp.concatenate(
                    [lax.dot_general(q, ks[ri][gi], nt, preferred_element_type=F32),
                     lax.dot_general(q, ks[ri][gi + 1], nt, preferred_element_type=F32)], axis=1))
            probs = []
            for (ri, gi), sc in zip(units, scores):
                rows = chains[ri][gi + 1]
                mask = band & ((kj >= blk) | (n0 > 0)) if gi == 0 else band
                s = jnp.where(mask, sc * scale, -jnp.inf)
                m = jnp.max(s, axis=1, keepdims=True)
                p = jnp.exp(s - m)
                l = jnp.sum(p, axis=1, keepdims=True)
                m_s[pi, rows, :] = jnp.broadcast_to(m, (blk, HEAD_DIM))
                l_s[pi, rows, :] = jnp.broadcast_to(l, (blk, HEAD_DIM))
                probs.append(p.astype(BF16))
            vs = [[v_ref[rw, :].astype(BF16) for rw in rows] for rows in chains]
            for (ri, gi), pb in zip(units, probs):
                o_s[pi, chains[ri][gi + 1], :] = (
                    jnp.dot(pb[:, :blk], vs[ri][gi], preferred_element_type=F32)
                    + jnp.dot(pb[:, blk:], vs[ri][gi + 1], preferred_element_type=F32))
            return carry

        lax.fori_loop(0, (dil // nres) * ngrp, group_step, 0)

    npat = len(patterns)

    def merge_step(t, carry):
        rows = pl.ds(pl.multiple_of(t * blk, blk), blk)
        ms = [m_s[pi, rows, :] for pi in range(npat)]
        mx = functools.reduce(jnp.maximum, ms)
        num = jnp.zeros((blk, HEAD_DIM), F32)
        den = jnp.zeros((blk, HEAD_DIM), F32)
        for pi in range(npat):
            w = jnp.exp(ms[pi] - mx)
            num = num + w * o_s[pi, rows, :]
            den = den + w * l_s[pi, rows, :]
        o_ref[rows, :] = (num / den).astype(o_ref.dtype)
        return carry

    lax.fori_loop(0, seq // blk, merge_step, 0)


def _dilated_attention(qkv, seq):
    n = qkv.shape[0]
    nh = qkv.shape[1] // (3 * HEAD_DIM)
    for window, dil in DILATED_PATTERNS:
        assert window // dil <= ATTN_BLOCK and seq % (dil * ATTN_BLOCK) == 0
    blk = (seq, HEAD_DIM)
    return pl.pallas_call(
        functools.partial(_attn_kernel, seq=seq, patterns=DILATED_PATTERNS,
                          scale=np.float32(HEAD_DIM ** -0.5)),
        grid=(n // seq, nh),
        in_specs=[pl.BlockSpec(blk, lambda b, h: (b, h)),
                  pl.BlockSpec(blk, lambda b, h: (b, nh + h)),
                  pl.BlockSpec(blk, lambda b, h: (b, 2 * nh + h))],
        out_specs=pl.BlockSpec(blk, lambda b, h: (b, h)),
        out_shape=jax.ShapeDtypeStruct((n, nh * HEAD_DIM), BF16),
        scratch_shapes=[pltpu.VMEM((len(DILATED_PATTERNS),) + blk, F32)] * 3,
        compiler_params=_cparams("arbitrary", "arbitrary"),
        name="dilated_attn",
    )(qkv, qkv, qkv)


def _top16(s, kid, exact_ties):
    nkeys = s.shape[0]
    vals = []
    if exact_ties:
        rank = jnp.full(s.shape, float(nkeys), F32)
        for it in range(PEER_TOPK):
            m = jnp.max(s, axis=0, keepdims=True)
            idx = jnp.min(jnp.where(s == m, kid, float(nkeys)), axis=0, keepdims=True)
            sel = kid == idx
            rank = jnp.where(sel, float(it), rank)
            s = jnp.where(sel, -jnp.inf, s)
            vals.append(m)
        return rank, jnp.concatenate(vals, axis=0)
    floor = np.float32(-(2.0 ** 125))
    for it in range(PEER_TOPK):
        m = jnp.max(s, axis=0, keepdims=True)
        s = jnp.where(s == m, np.float32(-(2.0 ** 126 + it * 2.0 ** 103)), s)
        vals.append(m)
    step = (-s - np.float32(2.0 ** 126)) * np.float32(2.0 ** -103)
    rank = jnp.where(s <= floor, jnp.maximum(step, 0.0), float(nkeys))
    return rank, jnp.concatenate(vals, axis=0)


def _router_kernel(q_ref, sk_ref, p1_ref, n_ref, r2_ref, p2_ref, *, nsub):
    nk = PEER_NKEYS
    kt = PEER_TOPK
    half = kt // 2
    kid = lax.broadcasted_iota(jnp.int32, (nk, LANES), 0).astype(F32)
    ia = lax.broadcasted_iota(jnp.int32, (kt, LANES), 0).astype(F32)
    ih = ia[0:half]
    cid = jnp.concatenate([ia * kt] + [ih * kt + float(b) for b in range(1, half)] + [ih + float(half)],
                          axis=0)
    ncand = kt * kt
    tail0 = kt + (half - 1) * half

    def route(lanes, exact_ties):
        q1 = q_ref[0:nk, lanes].astype(BF16)
        q2 = q_ref[nk:2 * nk, lanes].astype(BF16)
        s1 = jnp.dot(sk_ref[0], q1, preferred_element_type=F32)
        s2 = jnp.dot(sk_ref[1], q2, preferred_element_type=F32)
        rank1, v1 = _top16(s1, kid, exact_ties)
        rank2, v2 = _top16(s2, kid, exact_ties)
        work = jnp.concatenate([v1 + v2[0:1]] + [v1[0:half] + v2[b:b + 1] for b in range(1, half)]
                               + [v1[0:1] + v2[half:kt]], axis=0)
        m0 = v1[0:1] + v2[0:1]
        z = jnp.zeros((1, LANES), F32)
        if exact_ties:
            n_a = jnp.zeros((kt, LANES), F32)
            for _ in range(kt):
                m = jnp.max(work, axis=0, keepdims=True)
                idx = jnp.min(jnp.where(work == m, cid, float(ncand)), axis=0, keepdims=True)
                work = jnp.where(cid == idx, -jnp.inf, work)
                a_sel = jnp.floor(idx * (1.0 / kt))
                n_a = n_a + jnp.where(ia == a_sel, 1.0, 0.0)
                z = z + jnp.exp(m - m0)
        else:
            for _ in range(kt):
                m = jnp.max(work, axis=0, keepdims=True)
                work = jnp.where(work == m, -jnp.inf, work)
                z = z + jnp.exp(m - m0)
            mark = jnp.where(work == -jnp.inf, 1.0, 0.0)
            low = mark[0:half]
            for b in range(1, half):
                low = low + mark[kt + (b - 1) * half:kt + b * half]
            tail = jnp.sum(mark[tail0:], axis=0, keepdims=True)
            n_a = jnp.concatenate([low + jnp.where(ih == 0.0, tail, 0.0), mark[half:kt]], axis=0)
        nkey = jnp.zeros((nk, LANES), F32)
        for a in range(kt):
            nkey = jnp.where(rank1 == float(a), n_a[a:a + 1], nkey)
        p1_ref[0, :, lanes] = jnp.exp(s1 - v1[0:1])
        n_ref[0, :, lanes] = nkey
        r2_ref[0, :, lanes] = rank2.astype(r2_ref.dtype)
        p2_ref[0, :, lanes] = (jnp.exp(s2 - v2[0:1]) / (2.0 * z)).astype(p2_ref.dtype)
        if exact_ties:
            return None
        ranked = lambda r: jnp.sum(jnp.where(r < float(nk), 1.0, 0.0), axis=0, keepdims=True)
        ok = ((ranked(rank1) == float(kt)) & (ranked(rank2) == float(kt))
              & (jnp.sum(n_a, axis=0, keepdims=True) == float(kt)))
        return jnp.min(jnp.where(ok, 1.0, 0.0))

    def sub(st, carry):
        lanes = [pl.ds(pl.multiple_of((st * ROUTER_GROUPS + i) * LANES, LANES), LANES)
                 for i in range(ROUTER_GROUPS)]
        all_distinct = functools.reduce(jnp.minimum, [route(l, False) for l in lanes])

        @pl.when(all_distinct < 0.5)
        def _():
            for l in lanes:
                route(l, True)

        return carry

    lax.fori_loop(0, nsub // ROUTER_GROUPS, sub, 0)


def _router(qt, sk, tt=512):
    dq2, n = qt.shape
    nh = PEER_HEADS
    nk = PEER_NKEYS
    out = jax.ShapeDtypeStruct((nh, nk, n), F32)
    out_b = jax.ShapeDtypeStruct((nh, nk, n), BF16)
    ospec = pl.BlockSpec((1, nk, tt), lambda i, h: (h, 0, i))
    return pl.pallas_call(
        functools.partial(_router_kernel, nsub=tt // LANES),
        grid=(n // tt, nh),
        in_specs=[pl.BlockSpec((2 * nk, tt), lambda i, h: (h, i)),
                  pl.BlockSpec((2, nk, nk), lambda i, h: (0, 0, 0))],
        out_specs=(ospec, ospec, ospec, ospec),
        out_shape=(out, out, out_b, out_b),
        compiler_params=_cparams("arbitrary", "arbitrary"),
        name="peer_router",
    )(qt, sk)


def _peer_kernel(ht_ref, u_ref, vt_ref, p1_ref, n_ref, r2_ref, p2_ref, x_ref, g_ref, o_ref,
                 acc, a_s, gbuf, *, ne1, nh):
    j = pl.program_id(1)
    nk = PEER_NKEYS

    @pl.when(j == 0)
    def _():
        acc[...] = jnp.zeros_like(acc)

    a_s[...] = jnp.dot(u_ref[...], ht_ref[...], preferred_element_type=F32)

    def per_e1(e, carry):
        rows = pl.ds(pl.multiple_of(e * nk, nk), nk)
        act = _gelu_x2(a_s[rows, :])
        w = jnp.zeros(act.shape, BF16)
        for h in range(nh):
            p1 = jnp.broadcast_to(p1_ref[h, pl.ds(e, 1), :], act.shape).astype(BF16)
            cnt = jnp.broadcast_to(n_ref[h, pl.ds(e, 1), :], act.shape).astype(BF16)
            w = w + p1 * jnp.where(r2_ref[h] < cnt, p2_ref[h], jnp.zeros((), BF16))
        gbuf[rows, :] = act.astype(BF16) * w
        return carry

    lax.fori_loop(0, ne1, per_e1, 0)
    acc[...] += jnp.dot(vt_ref[...], gbuf[...], preferred_element_type=F32)

    @pl.when(j == pl.num_programs(1) - 1)
    def _():
        o_ref[...] = x_ref[...] + g_ref[0] * acc[...].T


def _peer(ht, u_tab, vt_tab, layer, p1, cnt, r2, p2, x2d, g, seq, tt=512, eb=1024):
    d, n = ht.shape
    ne = u_tab.shape[1]
    nh, nk, _ = p1.shape
    ne1 = eb // nk
    per_batch = seq // tt
    tok = pl.BlockSpec((nh, nk, tt), lambda i, j: (0, 0, i))
    e1b = pl.BlockSpec((nh, ne1, tt), lambda i, j: (0, j, i))
    return pl.pallas_call(
        functools.partial(_peer_kernel, ne1=ne1, nh=nh),
        grid=(n // tt, ne // eb),
        in_specs=[
            pl.BlockSpec((d, tt), lambda i, j: (0, i)),
            pl.BlockSpec((None, eb, d), lambda i, j: (layer, j, 0)),
            pl.BlockSpec((None, d, eb), lambda i, j: (layer, 0, j)),
            e1b, e1b, tok, tok,
            pl.BlockSpec((tt, d), lambda i, j: (i, 0)),
            pl.BlockSpec((1, 1, d), lambda i, j: (i // per_batch, 0, 0)),
        ],
        out_specs=pl.BlockSpec((tt, d), lambda i, j: (i, 0)),
        out_shape=jax.ShapeDtypeStruct((n, d), F32),
        scratch_shapes=[pltpu.VMEM((d, tt), F32), pltpu.VMEM((eb, tt), F32),
                        pltpu.VMEM((eb, tt), BF16)],
        compiler_params=_cparams("arbitrary", "arbitrary", vmem=60 * 1024 * 1024),
        name="peer_experts",
    )(ht, u_tab, vt_tab, p1, cnt, r2, p2, x2d, g)


def kernel(x, c, w_ada, b_ada, norm_mix, norm_ffn, w_in, ssm_lambda_re, ssm_lambda_im, ssm_log_dt, ssm_b_re, ssm_b_im, ssm_c_re, ssm_c_im, ssm_d, w_glu, w_br_ssm, w_br_attn, w_out, peer_wq, peer_subkeys, peer_u, peer_v, norm_final):
    bsz, seq, d = x.shape
    depth = w_ada.shape[0]
    n = bsz * seq
    ssm_w = ssm_d.shape[1]
    attn_w = w_br_attn.shape[1]
    assert bsz <= SUBLANES and seq % S5_CHUNK == 0

    c_pad = jnp.pad(c, ((0, SUBLANES - bsz), (0, 0)))
    mod_all = _modulation(c_pad, w_ada, b_ada)
    x2d = x.reshape(n, d)
    ts = S5_CHUNK // SUBLANES
    w_in_all = w_in.astype(BF16)
    u_all = peer_u.astype(BF16)
    vt_all = jnp.swapaxes(peer_v, 1, 2).astype(BF16)

    for l in range(depth):
        mods = [mod_all[l, :bsz, i * d:(i + 1) * d].reshape(bsz, 1, d) for i in range(N_MOD)]
        sh_m, sc_m, g_m, sh_f, sc_f, g_f = mods

        h = _norm_mod(x2d, norm_mix[l], sc_m, sh_m, seq, transpose=False)
        u = _matmul(h, w_in_all, layer=l, col0=0, ncols=ssm_w, out_dtype=F32, tm=1024, tn=1024,
                    name="in_proj_u")
        qkv = _matmul(h, w_in_all, layer=l, col0=ssm_w, ncols=3 * attn_w, out_dtype=F32,
                      tm=1024, tn=1024, name="in_proj_qkv")
        gates = _matmul(h, w_in_all, layer=l, col0=ssm_w + 3 * attn_w, ncols=2 * d, out_dtype=BF16,
                        act="sigmoid", tm=1024, tn=1024, name="in_proj_gates")

        a_re, a_im, bbar_re, bbar_im, at_re, at_im = _s5_prep(
            ssm_lambda_re[l], ssm_lambda_im[l], ssm_log_dt[l], ssm_b_re[l], ssm_b_im[l], ts)
        bblk, cblk = _s5_block_weights(bbar_re, bbar_im, ssm_c_re[l], ssm_c_im[l])
        y_s = _s5_branch(u, seq, a_re, a_im, at_re, at_im, bblk, cblk, ssm_d[l],
                         w_glu[l].astype(BF16))
        y_a = _dilated_attention(qkv, seq)
        merged = _merge(y_s, y_a, w_br_ssm[l].astype(BF16), w_br_attn[l].astype(BF16), gates)
        x2d = _outproj(merged, w_out[l].astype(BF16), x2d, g_m, seq)

        ht = _norm_mod(x2d, norm_ffn[l], sc_f, sh_f, seq, transpose=True)
        qt = _matmul(peer_wq[l].T.astype(BF16), ht, col0=0, ncols=n, out_dtype=F32,
                     tm=1024, tn=1024, name="peer_query")
        p1, cnt, r2, p2 = _router(qt, peer_subkeys[l].astype(BF16))
        x2d = _peer(ht, u_all, vt_all, l, p1, cnt, r2, p2, x2d, g_f, seq)

    return _final_norm(x2d, norm_final).reshape(bsz, seq, d)
```

```python
import functools
import math

import jax
import jax.numpy as jnp
import numpy as np
from jax import lax
from jax.experimental import pallas as pl
from jax.experimental.pallas import tpu as pltpu

F32 = jnp.float32
BF16 = jnp.bfloat16

EPS = 1e-6
N_MOD = 6
SSM_GROUP = 16
SSM_STATE = 64
HEAD_DIM = 128
ATTN_BLOCK = 128
ATTN_GROUP = 8
DILATED_PATTERNS = ((128, 1), (512, 4), (2048, 16))
PEER_HEADS = 8
PEER_NKEYS = 128
PEER_TOPK = 16
ROUTER_GROUPS = 2

LANES = 128
SUBLANES = 8
S5_CHUNK = 256
S5_LANE_BLOCK = 512
S5_CH_BLOCK = 128
VMEM_LIMIT = 56 * 1024 * 1024


def _cparams(*sem, vmem=VMEM_LIMIT):
    return pltpu.CompilerParams(dimension_semantics=sem, vmem_limit_bytes=vmem)


def _gelu_x2(x):
    return x * (1.0 + lax.erf(x * np.float32(math.sqrt(0.5))))


def _gelu(x):
    return 0.5 * _gelu_x2(x)


def _split3(x):
    hi = x.astype(BF16)
    r1 = x - hi.astype(F32)
    mid = r1.astype(BF16)
    lo = (r1 - mid.astype(F32)).astype(BF16)
    return hi, mid, lo


def _mod_kernel(c_ref, w_ref, b_ref, o_ref):
    c = c_ref[...]
    ca = c * jax.nn.sigmoid(c)
    w = w_ref[0]
    c_hi = ca.astype(BF16)
    c_lo = (ca - c_hi.astype(F32)).astype(BF16)
    w_hi = w.astype(BF16)
    w_lo = (w - w_hi.astype(F32)).astype(BF16)
    acc = jnp.dot(c_hi, w_hi, preferred_element_type=F32)
    acc += jnp.dot(c_lo, w_hi, preferred_element_type=F32)
    acc += jnp.dot(c_hi, w_lo, preferred_element_type=F32)
    o_ref[0] = acc + b_ref[0]


def _modulation(c_pad, w_ada, b_ada, tn=1024):
    depth, d, cols = w_ada.shape
    return pl.pallas_call(
        _mod_kernel,
        grid=(depth, cols // tn),
        in_specs=[
            pl.BlockSpec((SUBLANES, d), lambda l, j: (0, 0)),
            pl.BlockSpec((1, d, tn), lambda l, j: (l, 0, j)),
            pl.BlockSpec((1, 1, tn), lambda l, j: (l, 0, j)),
        ],
        out_specs=pl.BlockSpec((1, SUBLANES, tn), lambda l, j: (l, 0, j)),
        out_shape=jax.ShapeDtypeStruct((depth, SUBLANES, cols), F32),
        compiler_params=_cparams("arbitrary", "arbitrary"),
        name="adaln_mod",
    )(c_pad, w_ada, b_ada.reshape(depth, 1, cols))


def _norm_mod_kernel(x_ref, gain_ref, sc_ref, sh_ref, o_ref, *, transpose):
    x = x_ref[...]
    ms = jnp.mean(x * x, axis=-1, keepdims=True)
    y = x * lax.rsqrt(ms + EPS) * gain_ref[...]
    h = y * (1.0 + sc_ref[0]) + sh_ref[0]
    if transpose:
        o_ref[...] = h.T.astype(o_ref.dtype)
    else:
        o_ref[...] = h.astype(o_ref.dtype)


def _norm_mod(x2d, gain, sc, sh, seq, *, transpose, tm=512):
    n, d = x2d.shape
    per_batch = seq // tm
    if transpose:
        out_spec = pl.BlockSpec((d, tm), lambda i: (0, i))
        out_shape = jax.ShapeDtypeStruct((d, n), BF16)
    else:
        out_spec = pl.BlockSpec((tm, d), lambda i: (i, 0))
        out_shape = jax.ShapeDtypeStruct((n, d), BF16)
    return pl.pallas_call(
        functools.partial(_norm_mod_kernel, transpose=transpose),
        grid=(n // tm,),
        in_specs=[
            pl.BlockSpec((tm, d), lambda i: (i, 0)),
            pl.BlockSpec((1, d), lambda i: (0, 0)),
            pl.BlockSpec((1, 1, d), lambda i: (i // per_batch, 0, 0)),
            pl.BlockSpec((1, 1, d), lambda i: (i // per_batch, 0, 0)),
        ],
        out_specs=out_spec,
        out_shape=out_shape,
        compiler_params=_cparams("arbitrary"),
        name="norm_mod_t" if transpose else "norm_mod",
    )(x2d, gain.reshape(1, d), sc, sh)


def _final_norm_kernel(x_ref, gain_ref, o_ref):
    x = x_ref[...]
    ms = jnp.mean(x * x, axis=-1, keepdims=True)
    o_ref[...] = x * lax.rsqrt(ms + EPS) * gain_ref[...]


def _final_norm(x2d, gain, tm=512):
    n, d = x2d.shape
    return pl.pallas_call(
        _final_norm_kernel,
        grid=(n // tm,),
        in_specs=[pl.BlockSpec((tm, d), lambda i: (i, 0)),
                  pl.BlockSpec((1, d), lambda i: (0, 0))],
        out_specs=pl.BlockSpec((tm, d), lambda i: (i, 0)),
        out_shape=jax.ShapeDtypeStruct((n, d), F32),
        compiler_params=_cparams("arbitrary"),
        name="final_norm",
    )(x2d, gain.reshape(1, d))


def _mm_kernel(a_ref, w_ref, o_ref, *, act):
    acc = jnp.dot(a_ref[...], w_ref[...], preferred_element_type=F32)
    if act == "sigmoid":
        acc = jax.nn.sigmoid(acc)
    o_ref[...] = acc.astype(o_ref.dtype)


def _matmul(a, w, *, col0, ncols, out_dtype, act=None, tm, tn, name, layer=None):
    m, k = a.shape
    off = col0 // tn
    if layer is None:
        w_spec = pl.BlockSpec((k, tn), lambda i, j: (0, j + off))
    else:
        w_spec = pl.BlockSpec((None, k, tn), lambda i, j: (layer, 0, j + off))
    return pl.pallas_call(
        functools.partial(_mm_kernel, act=act),
        grid=(m // tm, ncols // tn),
        in_specs=[pl.BlockSpec((tm, k), lambda i, j: (i, 0)), w_spec],
        out_specs=pl.BlockSpec((tm, tn), lambda i, j: (i, j)),
        out_shape=jax.ShapeDtypeStruct((m, ncols), out_dtype),
        compiler_params=_cparams("arbitrary", "arbitrary"),
        name=name,
    )(a, w)


def _merge_kernel(ys_ref, ya_ref, ws_ref, wa_ref, gs_ref, ga_ref, o_ref):
    ps = jnp.dot(ys_ref[...], ws_ref[...], preferred_element_type=F32)
    pa = jnp.dot(ya_ref[...], wa_ref[...], preferred_element_type=F32)
    o_ref[...] = (gs_ref[...].astype(F32) * ps
                  + ga_ref[...].astype(F32) * pa).astype(o_ref.dtype)


def _merge(ys, ya, ws, wa, gates, tm=1024, tn=1024):
    n, ks = ys.shape
    ka = ya.shape[1]
    d = ws.shape[1]
    goff = d // tn
    return pl.pallas_call(
        _merge_kernel,
        grid=(n // tm, d // tn),
        in_specs=[
            pl.BlockSpec((tm, ks), lambda i, j: (i, 0)),
            pl.BlockSpec((tm, ka), lambda i, j: (i, 0)),
            pl.BlockSpec((ks, tn), lambda i, j: (0, j)),
            pl.BlockSpec((ka, tn), lambda i, j: (0, j)),
            pl.BlockSpec((tm, tn), lambda i, j: (i, j)),
            pl.BlockSpec((tm, tn), lambda i, j: (i, j + goff)),
        ],
        out_specs=pl.BlockSpec((tm, tn), lambda i, j: (i, j)),
        out_shape=jax.ShapeDtypeStruct((n, d), BF16),
        compiler_params=_cparams("arbitrary", "arbitrary"),
        name="branch_merge",
    )(ys, ya, ws, wa, gates, gates)


def _outproj_kernel(a_ref, w_ref, x_ref, g_ref, o_ref):
    acc = jnp.dot(a_ref[...], w_ref[...], preferred_element_type=F32)
    o_ref[...] = x_ref[...] + g_ref[0] * acc


def _outproj(a, w, x2d, g, seq, tm=1024, tn=1024):
    n, k = a.shape
    d = w.shape[1]
    per_batch = seq // tm
    return pl.pallas_call(
        _outproj_kernel,
        grid=(n // tm, d // tn),
        in_specs=[
            pl.BlockSpec((tm, k), lambda i, j: (i, 0)),
            pl.BlockSpec((k, tn), lambda i, j: (0, j)),
            pl.BlockSpec((tm, tn), lambda i, j: (i, j)),
            pl.BlockSpec((1, 1, tn), lambda i, j: (i // per_batch, 0, j)),
        ],
        out_specs=pl.BlockSpec((tm, tn), lambda i, j: (i, j)),
        out_shape=jax.ShapeDtypeStruct((n, d), F32),
        compiler_params=_cparams("arbitrary", "arbitrary"),
        name="out_proj",
    )(a, w, x2d, g)


def _s5_prep_kernel(lr_ref, li_ref, ldt_ref, br_ref, bi_ref,
                    are_ref, aim_ref, bbr_ref, bbi_ref, atr_ref, ati_ref, *, ts):
    lr, li = lr_ref[...], li_ref[...]
    dt = jnp.exp(ldt_ref[...])
    mag = jnp.exp(lr * dt)
    a_re = mag * jnp.cos(li * dt)
    a_im = mag * jnp.sin(li * dt)
    inv = 1.0 / (lr * lr + li * li)
    coef_re = ((a_re - 1.0) * lr + a_im * li) * inv
    coef_im = (a_im * lr - (a_re - 1.0) * li) * inv
    br, bi = br_ref[...], bi_ref[...]
    bbr_ref[...] = coef_re * br - coef_im * bi
    bbi_ref[...] = coef_re * bi + coef_im * br
    are_ref[...] = a_re
    aim_ref[...] = a_im
    pr, pi = a_re, a_im
    for _ in range(ts - 1):
        pr, pi = pr * a_re - pi * a_im, pr * a_im + pi * a_re
    atr_ref[...] = pr
    ati_ref[...] = pi


def _s5_prep(lam_re, lam_im, log_dt, b_re, b_im, ts):
    g, p, c = b_re.shape
    ns = g * p
    row = lambda t: t.reshape(1, ns)
    ldt = jnp.broadcast_to(log_dt[:, None], (g, p))
    bt = lambda t: jnp.transpose(t, (2, 0, 1)).reshape(c, ns)
    shapes = (
        jax.ShapeDtypeStruct((1, ns), F32), jax.ShapeDtypeStruct((1, ns), F32),
        jax.ShapeDtypeStruct((c, ns), F32), jax.ShapeDtypeStruct((c, ns), F32),
        jax.ShapeDtypeStruct((1, ns), F32), jax.ShapeDtypeStruct((1, ns), F32),
    )
    return pl.pallas_call(
        functools.partial(_s5_prep_kernel, ts=ts),
        out_shape=shapes,
        name="s5_prep",
    )(row(lam_re), row(lam_im), row(ldt), bt(b_re), bt(b_im))


def _s5_block_weights(bbar_re, bbar_im, c_re, c_im):
    c, ns = bbar_re.shape
    gpb = S5_CH_BLOCK // c
    nk = ns // (gpb * SSM_STATE)
    eye = jnp.eye(gpb, dtype=F32)

    def bpart(t):
        t = t.reshape(c, nk, gpb, SSM_STATE)
        t = jnp.einsum("ckjp,ij->kicjp", t, eye)
        return t.reshape(nk, gpb * c, gpb * SSM_STATE)

    def cpart(t):
        t = t.reshape(nk, gpb, c, SSM_STATE)
        t = jnp.einsum("kicp,ij->kjpic", t, eye)
        return t.reshape(nk, gpb * SSM_STATE, gpb * c)

    bblk = jnp.concatenate([bpart(bbar_re), bpart(bbar_im)], axis=2).astype(BF16)
    cblk = jnp.concatenate([cpart(c_re), cpart(-c_im)], axis=1).astype(BF16)
    return bblk, cblk


def _s5_kernel(u_ref, pm_ref, pmt_ref, bblk_ref, are_ref, aim_ref, atr_ref, ati_ref,
               cblk_ref, d_ref, wglu_ref, o_ref,
               hr, hi, car_r, car_i, cs_r, cs_i, yp, *, ts, nk):
    lw = S5_LANE_BLOCK
    cw = S5_CH_BLOCK

    @pl.when(pl.program_id(1) == 0)
    def _():
        car_r[...] = jnp.zeros_like(car_r)
        car_i[...] = jnp.zeros_like(car_i)

    u = u_ref[...]
    up = jnp.dot(pm_ref[...], u.astype(BF16), preferred_element_type=F32).astype(BF16)
    for k in range(nk):
        bu = jnp.dot(up[:, k * cw:(k + 1) * cw], bblk_ref[k], preferred_element_type=F32)
        hr[:, k * lw:(k + 1) * lw] = bu[:, :lw]
        hi[:, k * lw:(k + 1) * lw] = bu[:, lw:]

    for k in range(nk):
        sl = pl.ds(k * lw, lw)
        ar = are_ref[:, sl]
        ai = aim_ref[:, sl]

        def end_step(j, carry, sl=sl, ar=ar, ai=ai):
            cr, cim = carry
            rows = pl.ds(pl.multiple_of(j * SUBLANES, SUBLANES), SUBLANES)
            return ar * cr - ai * cim + hr[rows, sl], ar * cim + ai * cr + hi[rows, sl]

        zero = jnp.zeros((SUBLANES, lw), F32)
        end_r, end_i = lax.fori_loop(0, ts, end_step, (zero, zero), unroll=True)
        cs_r[:, sl] = end_r
        cs_i[:, sl] = end_i

    c_r = car_r[...]
    c_i = car_i[...]
    at_r = atr_ref[...]
    at_i = ati_ref[...]
    for s in range(SUBLANES):
        l_r = cs_r[s:s + 1, :]
        l_i = cs_i[s:s + 1, :]
        cs_r[s:s + 1, :] = c_r
        cs_i[s:s + 1, :] = c_i
        c_r, c_i = l_r + at_r * c_r - at_i * c_i, l_i + at_r * c_i + at_i * c_r
    car_r[...] = c_r
    car_i[...] = c_i

    for k in range(nk):
        sl = pl.ds(k * lw, lw)
        ar = are_ref[:, sl]
        ai = aim_ref[:, sl]

        def scan_step(j, carry, sl=sl, ar=ar, ai=ai):
            cr, cim = carry
            rows = pl.ds(pl.multiple_of(j * SUBLANES, SUBLANES), SUBLANES)
            nr = ar * cr - ai * cim + hr[rows, sl]
            ni = ar * cim + ai * cr + hi[rows, sl]
            hr[rows, sl] = nr
            hi[rows, sl] = ni
            return nr, ni

        lax.fori_loop(0, ts, scan_step, (cs_r[:, sl], cs_i[:, sl]), unroll=True)
        hcat = jnp.concatenate([hr[:, sl], hi[:, sl]], axis=1).astype(BF16)
        yp[:, k * cw:(k + 1) * cw] = jnp.dot(hcat, cblk_ref[k], preferred_element_type=F32)

    y1, y2, y3 = _split3(yp[...])
    pmt = pmt_ref[...]
    y = (jnp.dot(pmt, y1, preferred_element_type=F32)
         + jnp.dot(pmt, y2, preferred_element_type=F32)
         + jnp.dot(pmt, y3, preferred_element_type=F32))
    y = y + d_ref[...] * u
    y = _gelu(y)
    gate = jax.nn.sigmoid(jnp.dot(y.astype(BF16), wglu_ref[...], preferred_element_type=F32))
    o_ref[...] = (y * gate).astype(o_ref.dtype)


def _s5_permutation(t_rows, ts):
    r = np.arange(t_rows)
    src = (r % SUBLANES) * ts + r // SUBLANES
    pm = np.zeros((t_rows, t_rows), np.float32)
    pm[r, src] = 1.0
    return jnp.asarray(pm, BF16), jnp.asarray(pm.T, BF16)


def _s5_branch(u, seq, a_re, a_im, at_re, at_im, bblk, cblk, d_skip, w_glu):
    n, w = u.shape
    ns = a_re.shape[1]
    t_rows = S5_CHUNK
    ts = t_rows // SUBLANES
    nk = ns // S5_LANE_BLOCK
    nchunk = seq // t_rows
    pm, pmt = _s5_permutation(t_rows, ts)
    full = lambda shape: pl.BlockSpec(shape, lambda b, c: (0,) * len(shape))
    rep8 = lambda row: jnp.broadcast_to(row, (SUBLANES, ns))
    return pl.pallas_call(
        functools.partial(_s5_kernel, ts=ts, nk=nk),
        grid=(n // seq, nchunk),
        in_specs=[
            pl.BlockSpec((t_rows, w), lambda b, c: (b * nchunk + c, 0)),
            full((t_rows, t_rows)), full((t_rows, t_rows)),
            full(bblk.shape),
            full((SUBLANES, ns)), full((SUBLANES, ns)), full((1, ns)), full((1, ns)),
            full(cblk.shape),
            full((1, w)), full((w, w)),
        ],
        out_specs=pl.BlockSpec((t_rows, w), lambda b, c: (b * nchunk + c, 0)),
        out_shape=jax.ShapeDtypeStruct((n, w), BF16),
        scratch_shapes=[
            pltpu.VMEM((t_rows, ns), F32), pltpu.VMEM((t_rows, ns), F32),
            pltpu.VMEM((1, ns), F32), pltpu.VMEM((1, ns), F32),
            pltpu.VMEM((SUBLANES, ns), F32), pltpu.VMEM((SUBLANES, ns), F32),
            pltpu.VMEM((t_rows, w), F32),
        ],
        compiler_params=_cparams("arbitrary", "arbitrary"),
        name="s5_branch",
    )(u, pm, pmt, bblk, rep8(a_re), rep8(a_im), at_re, at_im, cblk, d_skip.reshape(1, w), w_glu)


def _attn_kernel(q_ref, k_ref, v_ref, o_ref, o_s, lse_s, *, seq, patterns, scale):
    blk = ATTN_BLOCK
    qi = lax.broadcasted_iota(jnp.int32, (blk, 2 * blk), 0)
    kj = lax.broadcasted_iota(jnp.int32, (blk, 2 * blk), 1)
    dist = blk + qi - kj
    nt = (((1,), (1,)), ((), ()))

    for pi, (window, dil) in enumerate(patterns):
        n_back = window // dil
        nblk = seq // dil // blk
        grp = min(ATTN_GROUP, nblk)
        ngrp = nblk // grp
        band = (dist >= 0) & (dist <= n_back)

        nres = max(1, ATTN_GROUP // grp)

        def group_step(idx, carry, pi=pi, dil=dil, grp=grp, ngrp=ngrp, nres=nres, band=band):
            r0 = (idx // ngrp) * nres
            n0 = (idx % ngrp) * grp

            def rows_of(r, nb):
                start = r + nb * (blk * dil)
                return pl.ds(start, blk) if dil == 1 else pl.ds(start, blk, stride=dil)

            chains = [[rows_of(r0 + ri, jnp.maximum(n0 - 1, 0))]
                      + [rows_of(r0 + ri, n0 + gi) for gi in range(grp)] for ri in range(nres)]
            units = [(ri, gi) for ri in range(nres) for gi in range(grp)]
            ks = [[k_ref[rw, :].astype(BF16) for rw in rows] for rows in chains]
            scores = []
            for ri, gi in units:
                q = q_ref[chains[ri][gi + 1], :].astype(BF16)
                scores.append(jnp.concatenate(
                    [lax.dot_general(q, ks[ri][gi], nt, preferred_element_type=F32),
                     lax.dot_general(q, ks[ri][gi + 1], nt, preferred_element_type=F32)], axis=1))
            probs, inv_l = [], []
            for (ri, gi), sc in zip(units, scores):
                rows = chains[ri][gi + 1]
                mask = band & ((kj >= blk) | (n0 > 0)) if gi == 0 else band
                s = jnp.where(mask, sc * scale, -jnp.inf)
                m = jnp.max(s, axis=1, keepdims=True)
                p = jnp.exp(s - m)
                l = jnp.sum(p, axis=1, keepdims=True)
                lse_s[pi, rows, :] = jnp.broadcast_to(m + jnp.log(l), (blk, HEAD_DIM))
                inv_l.append(1.0 / l)
                probs.append(p.astype(BF16))
            vs = [[v_ref[rw, :].astype(BF16) for rw in rows] for rows in chains]
            for (ri, gi), pb, il in zip(units, probs, inv_l):
                o_s[pi, chains[ri][gi + 1], :] = il * (
                    jnp.dot(pb[:, :blk], vs[ri][gi], preferred_element_type=F32)
                    + jnp.dot(pb[:, blk:], vs[ri][gi + 1], preferred_element_type=F32))
            return carry

        lax.fori_loop(0, (dil // nres) * ngrp, group_step, 0, unroll=2)

    npat = len(patterns)

    def merge_step(t, carry):
        rows = pl.ds(pl.multiple_of(t * blk, blk), blk)
        lses = [lse_s[pi, rows, :] for pi in range(npat)]
        mx = functools.reduce(jnp.maximum, lses)
        num = jnp.zeros((blk, HEAD_DIM), F32)
        den = jnp.zeros((blk, HEAD_DIM), F32)
        for pi in range(npat):
            w = jnp.exp(lses[pi] - mx)
            num = num + w * o_s[pi, rows, :]
            den = den + w
        o_ref[rows, :] = (num / den).astype(o_ref.dtype)
        return carry

    lax.fori_loop(0, seq // blk, merge_step, 0)


def _dilated_attention(qkv, seq):
    n = qkv.shape[0]
    nh = qkv.shape[1] // (3 * HEAD_DIM)
    for window, dil in DILATED_PATTERNS:
        assert window // dil <= ATTN_BLOCK and seq % (dil * ATTN_BLOCK) == 0
    blk = (seq, HEAD_DIM)
    return pl.pallas_call(
        functools.partial(_attn_kernel, seq=seq, patterns=DILATED_PATTERNS,
                          scale=np.float32(HEAD_DIM ** -0.5)),
        grid=(n // seq, nh),
        in_specs=[pl.BlockSpec(blk, lambda b, h: (b, h)),
                  pl.BlockSpec(blk, lambda b, h: (b, nh + h)),
                  pl.BlockSpec(blk, lambda b, h: (b, 2 * nh + h))],
        out_specs=pl.BlockSpec(blk, lambda b, h: (b, h)),
        out_shape=jax.ShapeDtypeStruct((n, nh * HEAD_DIM), BF16),
        scratch_shapes=[pltpu.VMEM((len(DILATED_PATTERNS),) + blk, F32)] * 2,
        compiler_params=_cparams("arbitrary", "arbitrary"),
        name="dilated_attn",
    )(qkv, qkv, qkv)


def _top16(s, kid, exact_ties):
    nkeys = s.shape[0]
    vals = []
    if exact_ties:
        rank = jnp.full(s.shape, float(nkeys), F32)
        for it in range(PEER_TOPK):
            m = jnp.max(s, axis=0, keepdims=True)
            idx = jnp.min(jnp.where(s == m, kid, float(nkeys)), axis=0, keepdims=True)
            sel = kid == idx
            rank = jnp.where(sel, float(it), rank)
            s = jnp.where(sel, -jnp.inf, s)
            vals.append(m)
        return rank, jnp.concatenate(vals, axis=0)
    floor = np.float32(-(2.0 ** 125))
    for it in range(PEER_TOPK):
        m = jnp.max(s, axis=0, keepdims=True)
        s = jnp.where(s == m, np.float32(-(2.0 ** 126 + it * 2.0 ** 103)), s)
        vals.append(m)
    step = (-s - np.float32(2.0 ** 126)) * np.float32(2.0 ** -103)
    rank = jnp.where(s <= floor, jnp.maximum(step, 0.0), float(nkeys))
    return rank, jnp.concatenate(vals, axis=0)


def _router_kernel(q_ref, sk_ref, p1_ref, n_ref, r2_ref, p2_ref, *, nsub):
    nk = PEER_NKEYS
    kt = PEER_TOPK
    half = kt // 2
    kid = lax.broadcasted_iota(jnp.int32, (nk, LANES), 0).astype(F32)
    ia = lax.broadcasted_iota(jnp.int32, (kt, LANES), 0).astype(F32)
    ih = ia[0:half]
    cid = jnp.concatenate([ia * kt] + [ih * kt + float(b) for b in range(1, half)] + [ih + float(half)],
                          axis=0)
    ncand = kt * kt
    tail0 = kt + (half - 1) * half

    def route(lanes, exact_ties):
        q1 = q_ref[0:nk, lanes].astype(BF16)
        q2 = q_ref[nk:2 * nk, lanes].astype(BF16)
        s1 = jnp.dot(sk_ref[0], q1, preferred_element_type=F32)
        s2 = jnp.dot(sk_ref[1], q2, preferred_element_type=F32)
        rank1, v1 = _top16(s1, kid, exact_ties)
        rank2, v2 = _top16(s2, kid, exact_ties)
        work = jnp.concatenate([v1 + v2[0:1]] + [v1[0:half] + v2[b:b + 1] for b in range(1, half)]
                               + [v1[0:1] + v2[half:kt]], axis=0)
        m0 = v1[0:1] + v2[0:1]
        z = jnp.zeros((1, LANES), F32)
        if exact_ties:
            n_a = jnp.zeros((kt, LANES), F32)
            for _ in range(kt):
                m = jnp.max(work, axis=0, keepdims=True)
                idx = jnp.min(jnp.where(work == m, cid, float(ncand)), axis=0, keepdims=True)
                work = jnp.where(cid == idx, -jnp.inf, work)
                a_sel = jnp.floor(idx * (1.0 / kt))
                n_a = n_a + jnp.where(ia == a_sel, 1.0, 0.0)
                z = z + jnp.exp(m - m0)
        else:
            for _ in range(kt):
                m = jnp.max(work, axis=0, keepdims=True)
                work = jnp.where(work == m, -jnp.inf, work)
                z = z + jnp.exp(m - m0)
            mark = jnp.where(work == -jnp.inf, 1.0, 0.0)
            low = mark[0:half]
            for b in range(1, half):
                low = low + mark[kt + (b - 1) * half:kt + b * half]
            tail = jnp.sum(mark[tail0:], axis=0, keepdims=True)
            n_a = jnp.concatenate([low + jnp.where(ih == 0.0, tail, 0.0), mark[half:kt]], axis=0)
        nkey = jnp.zeros((nk, LANES), F32)
        for a in range(kt):
            nkey = jnp.where(rank1 == float(a), n_a[a:a + 1], nkey)
        p1_ref[0, :, lanes] = jnp.exp(s1 - v1[0:1])
        n_ref[0, :, lanes] = nkey
        r2_ref[0, :, lanes] = rank2.astype(r2_ref.dtype)
        p2_ref[0, :, lanes] = (jnp.exp(s2 - v2[0:1]) / (2.0 * z)).astype(p2_ref.dtype)
        if exact_ties:
            return None
        ranked = lambda r: jnp.sum(jnp.where(r < float(nk), 1.0, 0.0), axis=0, keepdims=True)
        ok = ((ranked(rank1) == float(kt)) & (ranked(rank2) == float(kt))
              & (jnp.sum(n_a, axis=0, keepdims=True) == float(kt)))
        return jnp.min(jnp.where(ok, 1.0, 0.0))

    def sub(st, carry):
        lanes = [pl.ds(pl.multiple_of((st * ROUTER_GROUPS + i) * LANES, LANES), LANES)
                 for i in range(ROUTER_GROUPS)]
        all_distinct = functools.reduce(jnp.minimum, [route(l, False) for l in lanes])

        @pl.when(all_distinct < 0.5)
        def _():
            for l in lanes:
                route(l, True)

        return carry

    lax.fori_loop(0, nsub // ROUTER_GROUPS, sub, 0)


def _router(qt, sk, tt=512):
    dq2, n = qt.shape
    nh = PEER_HEADS
    nk = PEER_NKEYS
    out = jax.ShapeDtypeStruct((nh, nk, n), F32)
    out_b = jax.ShapeDtypeStruct((nh, nk, n), BF16)
    ospec = pl.BlockSpec((1, nk, tt), lambda i, h: (h, 0, i))
    return pl.pallas_call(
        functools.partial(_router_kernel, nsub=tt // LANES),
        grid=(n // tt, nh),
        in_specs=[pl.BlockSpec((2 * nk, tt), lambda i, h: (h, i)),
                  pl.BlockSpec((2, nk, nk), lambda i, h: (0, 0, 0))],
        out_specs=(ospec, ospec, ospec, ospec),
        out_shape=(out, out, out_b, out_b),
        compiler_params=_cparams("arbitrary", "arbitrary"),
        name="peer_router",
    )(qt, sk)


def _peer_kernel(ht_ref, u_ref, vt_ref, p1_ref, n_ref, r2_ref, p2_ref, x_ref, g_ref, o_ref,
                 acc, gbuf, *, ne1, nh):
    j = pl.program_id(1)
    nk = PEER_NKEYS

    @pl.when(j == 0)
    def _():
        acc[...] = jnp.zeros_like(acc)

    ht = ht_ref[...]
    for e in range(ne1):
        rows = slice(e * nk, (e + 1) * nk)
        act = _gelu_x2(jnp.dot(u_ref[rows, :], ht, preferred_element_type=F32))
        w = jnp.zeros(act.shape, BF16)
        for h in range(nh):
            p1 = jnp.broadcast_to(p1_ref[h, e:e + 1, :], act.shape).astype(BF16)
            cnt = jnp.broadcast_to(n_ref[h, e:e + 1, :], act.shape).astype(BF16)
            w = w + p1 * jnp.where(r2_ref[h] < cnt, p2_ref[h], jnp.zeros((), BF16))
        gbuf[rows, :] = act.astype(BF16) * w
    acc[...] += jnp.dot(vt_ref[...], gbuf[...], preferred_element_type=F32)

    @pl.when(j == pl.num_programs(1) - 1)
    def _():
        o_ref[...] = x_ref[...] + g_ref[0] * acc[...].T


def _peer(ht, u_tab, vt_tab, layer, p1, cnt, r2, p2, x2d, g, seq, tt=512, eb=1024):
    d, n = ht.shape
    ne = u_tab.shape[1]
    nh, nk, _ = p1.shape
    ne1 = eb // nk
    per_batch = seq // tt
    tok = pl.BlockSpec((nh, nk, tt), lambda i, j: (0, 0, i))
    e1b = pl.BlockSpec((nh, ne1, tt), lambda i, j: (0, j, i))
    return pl.pallas_call(
        functools.partial(_peer_kernel, ne1=ne1, nh=nh),
        grid=(n // tt, ne // eb),
        in_specs=[
            pl.BlockSpec((d, tt), lambda i, j: (0, i)),
            pl.BlockSpec((None, eb, d), lambda i, j: (layer, j, 0)),
            pl.BlockSpec((None, d, eb), lambda i, j: (layer, 0, j)),
            e1b, e1b, tok, tok,
            pl.BlockSpec((tt, d), lambda i, j: (i, 0)),
            pl.BlockSpec((1, 1, d), lambda i, j: (i // per_batch, 0, 0)),
        ],
        out_specs=pl.BlockSpec((tt, d), lambda i, j: (i, 0)),
        out_shape=jax.ShapeDtypeStruct((n, d), F32),
        scratch_shapes=[pltpu.VMEM((d, tt), F32), pltpu.VMEM((eb, tt), BF16)],
        compiler_params=_cparams("arbitrary", "arbitrary", vmem=60 * 1024 * 1024),
        name="peer_experts",
    )(ht, u_tab, vt_tab, p1, cnt, r2, p2, x2d, g)


def kernel(x, c, w_ada, b_ada, norm_mix, norm_ffn, w_in, ssm_lambda_re, ssm_lambda_im, ssm_log_dt, ssm_b_re, ssm_b_im, ssm_c_re, ssm_c_im, ssm_d, w_glu, w_br_ssm, w_br_attn, w_out, peer_wq, peer_subkeys, peer_u, peer_v, norm_final):
    bsz, seq, d = x.shape
    depth = w_ada.shape[0]
    n = bsz * seq
    ssm_w = ssm_d.shape[1]
    attn_w = w_br_attn.shape[1]
    assert bsz <= SUBLANES and seq % S5_CHUNK == 0

    c_pad = jnp.pad(c, ((0, SUBLANES - bsz), (0, 0)))
    mod_all = _modulation(c_pad, w_ada, b_ada)
    x2d = x.reshape(n, d)
    ts = S5_CHUNK // SUBLANES
    w_in_all = w_in.astype(BF16)
    u_all = peer_u.astype(BF16)
    vt_all = jnp.swapaxes(peer_v, 1, 2).astype(BF16)

    for l in range(depth):
        mods = [mod_all[l, :bsz, i * d:(i + 1) * d].reshape(bsz, 1, d) for i in range(N_MOD)]
        sh_m, sc_m, g_m, sh_f, sc_f, g_f = mods

        h = _norm_mod(x2d, norm_mix[l], sc_m, sh_m, seq, transpose=False)
        u = _matmul(h, w_in_all, layer=l, col0=0, ncols=ssm_w, out_dtype=F32, tm=1024, tn=1024,
                    name="in_proj_u")
        qkv = _matmul(h, w_in_all, layer=l, col0=ssm_w, ncols=3 * attn_w, out_dtype=F32,
                      tm=1024, tn=1024, name="in_proj_qkv")
        gates = _matmul(h, w_in_all, layer=l, col0=ssm_w + 3 * attn_w, ncols=2 * d, out_dtype=BF16,
                        act="sigmoid", tm=1024, tn=1024, name="in_proj_gates")

        a_re, a_im, bbar_re, bbar_im, at_re, at_im = _s5_prep(
            ssm_lambda_re[l], ssm_lambda_im[l], ssm_log_dt[l], ssm_b_re[l], ssm_b_im[l], ts)
        bblk, cblk = _s5_block_weights(bbar_re, bbar_im, ssm_c_re[l], ssm_c_im[l])
        y_s = _s5_branch(u, seq, a_re, a_im, at_re, at_im, bblk, cblk, ssm_d[l],
                         w_glu[l].astype(BF16))
        y_a = _dilated_attention(qkv, seq)
        merged = _merge(y_s, y_a, w_br_ssm[l].astype(BF16), w_br_attn[l].astype(BF16), gates)
        x2d = _outproj(merged, w_out[l].astype(BF16), x2d, g_m, seq)

        ht = _norm_mod(x2d, norm_ffn[l], sc_f, sh_f, seq, transpose=True)
        qt = _matmul(peer_wq[l].T.astype(BF16), ht, col0=0, ncols=n, out_dtype=F32,
                     tm=1024, tn=1024, name="peer_query")
        p1, cnt, r2, p2 = _router(qt, peer_subkeys[l].astype(BF16))
        x2d = _peer(ht, u_all, vt_all, l, p1, cnt, r2, p2, x2d, g_f, seq)

    return _final_norm(x2d, norm_final).reshape(bsz, seq, d)
```

```python
import functools
import math

import jax
import jax.numpy as jnp
import numpy as np
from jax import lax
from jax.experimental import pallas as pl
from jax.experimental.pallas import tpu as pltpu

F32 = jnp.float32
BF16 = jnp.bfloat16

EPS = 1e-6
N_MOD = 6
SSM_GROUP = 16
SSM_STATE = 64
HEAD_DIM = 128
ATTN_BLOCK = 128
ATTN_GROUP = 8
DILATED_PATTERNS = ((128, 1), (512, 4), (2048, 16))
PEER_HEADS = 8
PEER_NKEYS = 128
PEER_TOPK = 16
ROUTER_GROUPS = 2

LANES = 128
SUBLANES = 8
S5_CHUNK = 256
S5_LANE_BLOCK = 512
S5_CH_BLOCK = 128
VMEM_LIMIT = 56 * 1024 * 1024


def _cparams(*sem, vmem=VMEM_LIMIT):
    return pltpu.CompilerParams(dimension_semantics=sem, vmem_limit_bytes=vmem)


def _gelu_x2(x):
    return x * (1.0 + lax.erf(x * np.float32(math.sqrt(0.5))))


def _gelu(x):
    return 0.5 * _gelu_x2(x)


def _split3(x):
    hi = x.astype(BF16)
    r1 = x - hi.astype(F32)
    mid = r1.astype(BF16)
    lo = (r1 - mid.astype(F32)).astype(BF16)
    return hi, mid, lo


def _mod_kernel(c_ref, w_ref, b_ref, o_ref):
    c = c_ref[...]
    ca = c * jax.nn.sigmoid(c)
    w = w_ref[0]
    c_hi = ca.astype(BF16)
    c_lo = (ca - c_hi.astype(F32)).astype(BF16)
    w_hi = w.astype(BF16)
    w_lo = (w - w_hi.astype(F32)).astype(BF16)
    acc = jnp.dot(c_hi, w_hi, preferred_element_type=F32)
    acc += jnp.dot(c_lo, w_hi, preferred_element_type=F32)
    acc += jnp.dot(c_hi, w_lo, preferred_element_type=F32)
    o_ref[0] = acc + b_ref[0]


def _modulation(c_pad, w_ada, b_ada, tn=1024):
    depth, d, cols = w_ada.shape
    return pl.pallas_call(
        _mod_kernel,
        grid=(depth, cols // tn),
        in_specs=[
            pl.BlockSpec((SUBLANES, d), lambda l, j: (0, 0)),
            pl.BlockSpec((1, d, tn), lambda l, j: (l, 0, j)),
            pl.BlockSpec((1, 1, tn), lambda l, j: (l, 0, j)),
        ],
        out_specs=pl.BlockSpec((1, SUBLANES, tn), lambda l, j: (l, 0, j)),
        out_shape=jax.ShapeDtypeStruct((depth, SUBLANES, cols), F32),
        compiler_params=_cparams("arbitrary", "arbitrary"),
        name="adaln_mod",
    )(c_pad, w_ada, b_ada.reshape(depth, 1, cols))


def _rmsnorm(x, gain):
    ms = jnp.mean(x * x, axis=-1, keepdims=True)
    return x * lax.rsqrt(ms + EPS) * gain


def _in_proj_kernel(x_ref, gain_ref, sc_ref, sh_ref, w_ref, u_ref, qkv_ref, gates_ref, h_s, *, nu, nqkv):
    j = pl.program_id(1)

    @pl.when(j == 0)
    def _():
        h = _rmsnorm(x_ref[...], gain_ref[...]) * (1.0 + sc_ref[0]) + sh_ref[0]
        h_s[...] = h.astype(h_s.dtype)

    def proj():
        return jnp.dot(h_s[...], w_ref[...], preferred_element_type=F32)

    @pl.when(j < nu)
    def _():
        u_ref[...] = proj()

    @pl.when((j >= nu) & (j < nu + nqkv))
    def _():
        qkv_ref[...] = proj()

    @pl.when(j >= nu + nqkv)
    def _():
        gates_ref[...] = jax.nn.sigmoid(proj()).astype(gates_ref.dtype)


def _in_proj(x2d, gain, sc, sh, w_all, layer, seq, u_cols, qkv_cols, tm=1024, tn=512):
    n, d = x2d.shape
    cols = w_all.shape[2]
    nu, nqkv = u_cols // tn, qkv_cols // tn
    ng = cols // tn - nu - nqkv
    per_batch = seq // tm
    mod = pl.BlockSpec((1, 1, d), lambda i, j: (i // per_batch, 0, 0))
    return pl.pallas_call(
        functools.partial(_in_proj_kernel, nu=nu, nqkv=nqkv),
        grid=(n // tm, cols // tn),
        in_specs=[
            pl.BlockSpec((tm, d), lambda i, j: (i, 0)),
            pl.BlockSpec((1, d), lambda i, j: (0, 0)),
            mod, mod,
            pl.BlockSpec((None, d, tn), lambda i, j: (layer, 0, j)),
        ],
        out_specs=(
            pl.BlockSpec((tm, tn), lambda i, j: (i, jnp.minimum(j, nu - 1))),
            pl.BlockSpec((tm, tn), lambda i, j: (i, jnp.clip(j - nu, 0, nqkv - 1))),
            pl.BlockSpec((tm, tn), lambda i, j: (i, jnp.clip(j - nu - nqkv, 0, ng - 1))),
        ),
        out_shape=(jax.ShapeDtypeStruct((n, u_cols), F32),
                   jax.ShapeDtypeStruct((n, qkv_cols), F32),
                   jax.ShapeDtypeStruct((n, ng * tn), BF16)),
        scratch_shapes=[pltpu.VMEM((tm, d), BF16)],
        compiler_params=_cparams("arbitrary", "arbitrary"),
        name="in_proj",
    )(x2d, gain.reshape(1, d), sc, sh, w_all)


def _mm_kernel(a_ref, w_ref, o_ref, *, act):
    acc = jnp.dot(a_ref[...], w_ref[...], preferred_element_type=F32)
    if act == "sigmoid":
        acc = jax.nn.sigmoid(acc)
    o_ref[...] = acc.astype(o_ref.dtype)


def _matmul(a, w, *, col0, ncols, out_dtype, act=None, tm, tn, name, layer=None):
    m, k = a.shape
    off = col0 // tn
    if layer is None:
        w_spec = pl.BlockSpec((k, tn), lambda i, j: (0, j + off))
    else:
        w_spec = pl.BlockSpec((None, k, tn), lambda i, j: (layer, 0, j + off))
    return pl.pallas_call(
        functools.partial(_mm_kernel, act=act),
        grid=(m // tm, ncols // tn),
        in_specs=[pl.BlockSpec((tm, k), lambda i, j: (i, 0)), w_spec],
        out_specs=pl.BlockSpec((tm, tn), lambda i, j: (i, j)),
        out_shape=jax.ShapeDtypeStruct((m, ncols), out_dtype),
        compiler_params=_cparams("arbitrary", "arbitrary"),
        name=name,
    )(a, w)


def _merge_kernel(ys_ref, ya_ref, ws_ref, wa_ref, gs_ref, ga_ref, o_ref):
    ps = jnp.dot(ys_ref[...], ws_ref[...], preferred_element_type=F32)
    pa = jnp.dot(ya_ref[...], wa_ref[...], preferred_element_type=F32)
    o_ref[...] = (gs_ref[...].astype(F32) * ps
                  + ga_ref[...].astype(F32) * pa).astype(o_ref.dtype)


def _merge(ys, ya, ws, wa, gates, tm=1024, tn=1024):
    n, ks = ys.shape
    ka = ya.shape[1]
    d = ws.shape[1]
    goff = d // tn
    return pl.pallas_call(
        _merge_kernel,
        grid=(n // tm, d // tn),
        in_specs=[
            pl.BlockSpec((tm, ks), lambda i, j: (i, 0)),
            pl.BlockSpec((tm, ka), lambda i, j: (i, 0)),
            pl.BlockSpec((ks, tn), lambda i, j: (0, j)),
            pl.BlockSpec((ka, tn), lambda i, j: (0, j)),
            pl.BlockSpec((tm, tn), lambda i, j: (i, j)),
            pl.BlockSpec((tm, tn), lambda i, j: (i, j + goff)),
        ],
        out_specs=pl.BlockSpec((tm, tn), lambda i, j: (i, j)),
        out_shape=jax.ShapeDtypeStruct((n, d), BF16),
        compiler_params=_cparams("arbitrary", "arbitrary"),
        name="branch_merge",
    )(ys, ya, ws, wa, gates, gates)


def _outproj_norm_kernel(a_ref, w_ref, x_ref, g_ref, gain_ref, sc_ref, sh_ref, xo_ref, ht_ref):
    acc = jnp.dot(a_ref[...], w_ref[...], preferred_element_type=F32)
    xn = x_ref[...] + g_ref[0] * acc
    xo_ref[...] = xn
    h = _rmsnorm(xn, gain_ref[...]) * (1.0 + sc_ref[0]) + sh_ref[0]
    ht_ref[...] = h.T.astype(ht_ref.dtype)


def _outproj_norm(a, w, x2d, g, gain, sc, sh, seq, tm=512):
    n, k = a.shape
    d = w.shape[1]
    per_batch = seq // tm
    mod = pl.BlockSpec((1, 1, d), lambda i: (i // per_batch, 0, 0))
    return pl.pallas_call(
        _outproj_norm_kernel,
        grid=(n // tm,),
        in_specs=[
            pl.BlockSpec((tm, k), lambda i: (i, 0)),
            pl.BlockSpec((k, d), lambda i: (0, 0)),
            pl.BlockSpec((tm, d), lambda i: (i, 0)),
            mod,
            pl.BlockSpec((1, d), lambda i: (0, 0)),
            mod, mod,
        ],
        out_specs=(pl.BlockSpec((tm, d), lambda i: (i, 0)),
                   pl.BlockSpec((d, tm), lambda i: (0, i))),
        out_shape=(jax.ShapeDtypeStruct((n, d), F32), jax.ShapeDtypeStruct((d, n), BF16)),
        compiler_params=_cparams("arbitrary"),
        name="out_proj_norm",
    )(a, w, x2d, g, gain.reshape(1, d), sc, sh)


def _s5_prep_kernel(lr_ref, li_ref, ldt_ref, br_ref, bi_ref,
                    are_ref, aim_ref, bbr_ref, bbi_ref, atr_ref, ati_ref, *, ts):
    lr, li = lr_ref[...], li_ref[...]
    dt = jnp.exp(ldt_ref[...])
    mag = jnp.exp(lr * dt)
    a_re = mag * jnp.cos(li * dt)
    a_im = mag * jnp.sin(li * dt)
    inv = 1.0 / (lr * lr + li * li)
    coef_re = ((a_re - 1.0) * lr + a_im * li) * inv
    coef_im = (a_im * lr - (a_re - 1.0) * li) * inv
    br, bi = br_ref[...], bi_ref[...]
    bbr_ref[...] = coef_re * br - coef_im * bi
    bbi_ref[...] = coef_re * bi + coef_im * br
    are_ref[...] = a_re
    aim_ref[...] = a_im
    pr, pi = a_re, a_im
    for _ in range(ts - 1):
        pr, pi = pr * a_re - pi * a_im, pr * a_im + pi * a_re
    atr_ref[...] = pr
    ati_ref[...] = pi


def _s5_prep(lam_re, lam_im, log_dt, b_re, b_im, ts):
    g, p, c = b_re.shape
    ns = g * p
    row = lambda t: t.reshape(1, ns)
    ldt = jnp.broadcast_to(log_dt[:, None], (g, p))
    bt = lambda t: jnp.transpose(t, (2, 0, 1)).reshape(c, ns)
    shapes = (
        jax.ShapeDtypeStruct((1, ns), F32), jax.ShapeDtypeStruct((1, ns), F32),
        jax.ShapeDtypeStruct((c, ns), F32), jax.ShapeDtypeStruct((c, ns), F32),
        jax.ShapeDtypeStruct((1, ns), F32), jax.ShapeDtypeStruct((1, ns), F32),
    )
    return pl.pallas_call(
        functools.partial(_s5_prep_kernel, ts=ts),
        out_shape=shapes,
        name="s5_prep",
    )(row(lam_re), row(lam_im), row(ldt), bt(b_re), bt(b_im))


def _s5_block_weights(bbar_re, bbar_im, c_re, c_im):
    c, ns = bbar_re.shape
    gpb = S5_CH_BLOCK // c
    nk = ns // (gpb * SSM_STATE)
    eye = jnp.eye(gpb, dtype=F32)

    def bpart(t):
        t = t.reshape(c, nk, gpb, SSM_STATE)
        t = jnp.einsum("ckjp,ij->kicjp", t, eye)
        return t.reshape(nk, gpb * c, gpb * SSM_STATE)

    def cpart(t):
        t = t.reshape(nk, gpb, c, SSM_STATE)
        t = jnp.einsum("kicp,ij->kjpic", t, eye)
        return t.reshape(nk, gpb * SSM_STATE, gpb * c)

    bblk = jnp.concatenate([bpart(bbar_re), bpart(bbar_im)], axis=2).astype(BF16)
    cblk = jnp.concatenate([cpart(c_re), cpart(-c_im)], axis=1).astype(BF16)
    return bblk, cblk


def _s5_kernel(u_ref, pm_ref, pmt_ref, bblk_ref, are_ref, aim_ref, atr_ref, ati_ref,
               cblk_ref, d_ref, wglu_ref, o_ref,
               hr, hi, car_r, car_i, cs_r, cs_i, yp, *, ts, nk):
    lw = S5_LANE_BLOCK
    cw = S5_CH_BLOCK

    @pl.when(pl.program_id(1) == 0)
    def _():
        car_r[...] = jnp.zeros_like(car_r)
        car_i[...] = jnp.zeros_like(car_i)

    u = u_ref[...]
    up = jnp.dot(pm_ref[...], u.astype(BF16), preferred_element_type=F32).astype(BF16)
    for k in range(nk):
        bu = jnp.dot(up[:, k * cw:(k + 1) * cw], bblk_ref[k], preferred_element_type=F32)
        hr[:, k * lw:(k + 1) * lw] = bu[:, :lw]
        hi[:, k * lw:(k + 1) * lw] = bu[:, lw:]

    for k in range(nk):
        sl = pl.ds(k * lw, lw)
        ar = are_ref[:, sl]
        ai = aim_ref[:, sl]

        def end_step(j, carry, sl=sl, ar=ar, ai=ai):
            cr, cim = carry
            rows = pl.ds(pl.multiple_of(j * SUBLANES, SUBLANES), SUBLANES)
            return ar * cr - ai * cim + hr[rows, sl], ar * cim + ai * cr + hi[rows, sl]

        zero = jnp.zeros((SUBLANES, lw), F32)
        end_r, end_i = lax.fori_loop(0, ts, end_step, (zero, zero), unroll=True)
        cs_r[:, sl] = end_r
        cs_i[:, sl] = end_i

    c_r = car_r[...]
    c_i = car_i[...]
    at_r = atr_ref[...]
    at_i = ati_ref[...]
    for s in range(SUBLANES):
        l_r = cs_r[s:s + 1, :]
        l_i = cs_i[s:s + 1, :]
        cs_r[s:s + 1, :] = c_r
        cs_i[s:s + 1, :] = c_i
        c_r, c_i = l_r + at_r * c_r - at_i * c_i, l_i + at_r * c_i + at_i * c_r
    car_r[...] = c_r
    car_i[...] = c_i

    for k in range(nk):
        sl = pl.ds(k * lw, lw)
        ar = are_ref[:, sl]
        ai = aim_ref[:, sl]

        def scan_step(j, carry, sl=sl, ar=ar, ai=ai):
            cr, cim = carry
            rows = pl.ds(pl.multiple_of(j * SUBLANES, SUBLANES), SUBLANES)
            nr = ar * cr - ai * cim + hr[rows, sl]
            ni = ar * cim + ai * cr + hi[rows, sl]
            hr[rows, sl] = nr
            hi[rows, sl] = ni
            return nr, ni

        lax.fori_loop(0, ts, scan_step, (cs_r[:, sl], cs_i[:, sl]), unroll=True)
        hcat = jnp.concatenate([hr[:, sl], hi[:, sl]], axis=1).astype(BF16)
        yp[:, k * cw:(k + 1) * cw] = jnp.dot(hcat, cblk_ref[k], preferred_element_type=F32)

    y1, y2, y3 = _split3(yp[...])
    pmt = pmt_ref[...]
    y = (jnp.dot(pmt, y1, preferred_element_type=F32)
         + jnp.dot(pmt, y2, preferred_element_type=F32)
         + jnp.dot(pmt, y3, preferred_element_type=F32))
    y = y + d_ref[...] * u
    y = _gelu(y)
    gate = jax.nn.sigmoid(jnp.dot(y.astype(BF16), wglu_ref[...], preferred_element_type=F32))
    o_ref[...] = (y * gate).astype(o_ref.dtype)


def _s5_permutation(t_rows, ts):
    r = np.arange(t_rows)
    src = (r % SUBLANES) * ts + r // SUBLANES
    pm = np.zeros((t_rows, t_rows), np.float32)
    pm[r, src] = 1.0
    return jnp.asarray(pm, BF16), jnp.asarray(pm.T, BF16)


def _s5_branch(u, seq, a_re, a_im, at_re, at_im, bblk, cblk, d_skip, w_glu):
    n, w = u.shape
    ns = a_re.shape[1]
    t_rows = S5_CHUNK
    ts = t_rows // SUBLANES
    nk = ns // S5_LANE_BLOCK
    nchunk = seq // t_rows
    pm, pmt = _s5_permutation(t_rows, ts)
    full = lambda shape: pl.BlockSpec(shape, lambda b, c: (0,) * len(shape))
    rep8 = lambda row: jnp.broadcast_to(row, (SUBLANES, ns))
    return pl.pallas_call(
        functools.partial(_s5_kernel, ts=ts, nk=nk),
        grid=(n // seq, nchunk),
        in_specs=[
            pl.BlockSpec((t_rows, w), lambda b, c: (b * nchunk + c, 0)),
            full((t_rows, t_rows)), full((t_rows, t_rows)),
            full(bblk.shape),
            full((SUBLANES, ns)), full((SUBLANES, ns)), full((1, ns)), full((1, ns)),
            full(cblk.shape),
            full((1, w)), full((w, w)),
        ],
        out_specs=pl.BlockSpec((t_rows, w), lambda b, c: (b * nchunk + c, 0)),
        out_shape=jax.ShapeDtypeStruct((n, w), BF16),
        scratch_shapes=[
            pltpu.VMEM((t_rows, ns), F32), pltpu.VMEM((t_rows, ns), F32),
            pltpu.VMEM((1, ns), F32), pltpu.VMEM((1, ns), F32),
            pltpu.VMEM((SUBLANES, ns), F32), pltpu.VMEM((SUBLANES, ns), F32),
            pltpu.VMEM((t_rows, w), F32),
        ],
        compiler_params=_cparams("arbitrary", "arbitrary"),
        name="s5_branch",
    )(u, pm, pmt, bblk, rep8(a_re), rep8(a_im), at_re, at_im, cblk, d_skip.reshape(1, w), w_glu)


def _attn_kernel(q_ref, k_ref, v_ref, o_ref, o_s, lse_s, *, seq, patterns, scale):
    blk = ATTN_BLOCK
    qi = lax.broadcasted_iota(jnp.int32, (blk, 2 * blk), 0)
    kj = lax.broadcasted_iota(jnp.int32, (blk, 2 * blk), 1)
    dist = blk + qi - kj
    nt = (((1,), (1,)), ((), ()))

    for pi, (window, dil) in enumerate(patterns):
        n_back = window // dil
        nblk = seq // dil // blk
        grp = min(ATTN_GROUP, nblk)
        ngrp = nblk // grp
        band = (dist >= 0) & (dist <= n_back)

        nres = max(1, ATTN_GROUP // grp)

        def group_step(idx, carry, pi=pi, dil=dil, grp=grp, ngrp=ngrp, nres=nres, band=band):
            r0 = (idx // ngrp) * nres
            n0 = (idx % ngrp) * grp

            def rows_of(r, nb):
                start = r + nb * (blk * dil)
                return pl.ds(start, blk) if dil == 1 else pl.ds(start, blk, stride=dil)

            chains = [[rows_of(r0 + ri, jnp.maximum(n0 - 1, 0))]
                      + [rows_of(r0 + ri, n0 + gi) for gi in range(grp)] for ri in range(nres)]
            units = [(ri, gi) for ri in range(nres) for gi in range(grp)]
            ks = [[k_ref[rw, :].astype(BF16) for rw in rows] for rows in chains]
            scores = []
            for ri, gi in units:
                q = q_ref[chains[ri][gi + 1], :].astype(BF16)
                scores.append(jnp.concatenate(
                    [lax.dot_general(q, ks[ri][gi], nt, preferred_element_type=F32),
                     lax.dot_general(q, ks[ri][gi + 1], nt, preferred_element_type=F32)], axis=1))
            probs, inv_l = [], []
            for (ri, gi), sc in zip(units, scores):
                rows = chains[ri][gi + 1]
                mask = band & ((kj >= blk) | (n0 > 0)) if gi == 0 else band
                s = jnp.where(mask, sc * scale, -jnp.inf)
                m = jnp.max(s, axis=1, keepdims=True)
                p = jnp.exp(s - m)
                l = jnp.sum(p, axis=1, keepdims=True)
                lse_s[pi, rows, :] = jnp.broadcast_to(m + jnp.log(l), (blk, HEAD_DIM))
                inv_l.append(1.0 / l)
                probs.append(p.astype(BF16))
            vs = [[v_ref[rw, :].astype(BF16) for rw in rows] for rows in chains]
            for (ri, gi), pb, il in zip(units, probs, inv_l):
                o_s[pi, chains[ri][gi + 1], :] = il * (
                    jnp.dot(pb[:, :blk], vs[ri][gi], preferred_element_type=F32)
                    + jnp.dot(pb[:, blk:], vs[ri][gi + 1], preferred_element_type=F32))
            return carry

        lax.fori_loop(0, (dil // nres) * ngrp, group_step, 0, unroll=2)

    npat = len(patterns)

    def merge_step(t, carry):
        rows = pl.ds(pl.multiple_of(t * blk, blk), blk)
        lses = [lse_s[pi, rows, :] for pi in range(npat)]
        mx = functools.reduce(jnp.maximum, lses)
        num = jnp.zeros((blk, HEAD_DIM), F32)
        den = jnp.zeros((blk, HEAD_DIM), F32)
        for pi in range(npat):
            w = jnp.exp(lses[pi] - mx)
            num = num + w * o_s[pi, rows, :]
            den = den + w
        o_ref[rows, :] = (num / den).astype(o_ref.dtype)
        return carry

    lax.fori_loop(0, seq // blk, merge_step, 0)


def _dilated_attention(qkv, seq):
    n = qkv.shape[0]
    nh = qkv.shape[1] // (3 * HEAD_DIM)
    for window, dil in DILATED_PATTERNS:
        assert window // dil <= ATTN_BLOCK and seq % (dil * ATTN_BLOCK) == 0
    blk = (seq, HEAD_DIM)
    return pl.pallas_call(
        functools.partial(_attn_kernel, seq=seq, patterns=DILATED_PATTERNS,
                          scale=np.float32(HEAD_DIM ** -0.5)),
        grid=(n // seq, nh),
        in_specs=[pl.BlockSpec(blk, lambda b, h: (b, h)),
                  pl.BlockSpec(blk, lambda b, h: (b, nh + h)),
                  pl.BlockSpec(blk, lambda b, h: (b, 2 * nh + h))],
        out_specs=pl.BlockSpec(blk, lambda b, h: (b, h)),
        out_shape=jax.ShapeDtypeStruct((n, nh * HEAD_DIM), BF16),
        scratch_shapes=[pltpu.VMEM((len(DILATED_PATTERNS),) + blk, F32)] * 2,
        compiler_params=_cparams("arbitrary", "arbitrary"),
        name="dilated_attn",
    )(qkv, qkv, qkv)


def _top16(s, kid, exact_ties):
    nkeys = s.shape[0]
    vals = []
    if exact_ties:
        rank = jnp.full(s.shape, float(nkeys), F32)
        for it in range(PEER_TOPK):
            m = jnp.max(s, axis=0, keepdims=True)
            idx = jnp.min(jnp.where(s == m, kid, float(nkeys)), axis=0, keepdims=True)
            sel = kid == idx
            rank = jnp.where(sel, float(it), rank)
            s = jnp.where(sel, -jnp.inf, s)
            vals.append(m)
        return rank, jnp.concatenate(vals, axis=0)
    floor = np.float32(-(2.0 ** 125))
    for it in range(PEER_TOPK):
        m = jnp.max(s, axis=0, keepdims=True)
        s = jnp.where(s == m, np.float32(-(2.0 ** 126 + it * 2.0 ** 103)), s)
        vals.append(m)
    step = (-s - np.float32(2.0 ** 126)) * np.float32(2.0 ** -103)
    rank = jnp.where(s <= floor, jnp.maximum(step, 0.0), float(nkeys))
    return rank, jnp.concatenate(vals, axis=0)


def _router_kernel(q_ref, sk_ref, p1_ref, n_ref, r2_ref, p2_ref, *, nsub):
    nk = PEER_NKEYS
    kt = PEER_TOPK
    half = kt // 2
    kid = lax.broadcasted_iota(jnp.int32, (nk, LANES), 0).astype(F32)
    ia = lax.broadcasted_iota(jnp.int32, (kt, LANES), 0).astype(F32)
    ih = ia[0:half]
    cid = jnp.concatenate([ia * kt] + [ih * kt + float(b) for b in range(1, half)] + [ih + float(half)],
                          axis=0)
    ncand = kt * kt
    tail0 = kt + (half - 1) * half

    def route(lanes, exact_ties):
        q1 = q_ref[0:nk, lanes].astype(BF16)
        q2 = q_ref[nk:2 * nk, lanes].astype(BF16)
        s1 = jnp.dot(sk_ref[0], q1, preferred_element_type=F32)
        s2 = jnp.dot(sk_ref[1], q2, preferred_element_type=F32)
        rank1, v1 = _top16(s1, kid, exact_ties)
        rank2, v2 = _top16(s2, kid, exact_ties)
        work = jnp.concatenate([v1 + v2[0:1]] + [v1[0:half] + v2[b:b + 1] for b in range(1, half)]
                               + [v1[0:1] + v2[half:kt]], axis=0)
        m0 = v1[0:1] + v2[0:1]
        z = jnp.zeros((1, LANES), F32)
        if exact_ties:
            n_a = jnp.zeros((kt, LANES), F32)
            for _ in range(kt):
                m = jnp.max(work, axis=0, keepdims=True)
                idx = jnp.min(jnp.where(work == m, cid, float(ncand)), axis=0, keepdims=True)
                work = jnp.where(cid == idx, -jnp.inf, work)
                a_sel = jnp.floor(idx * (1.0 / kt))
                n_a = n_a + jnp.where(ia == a_sel, 1.0, 0.0)
                z = z + jnp.exp(m - m0)
        else:
            for _ in range(kt):
                m = jnp.max(work, axis=0, keepdims=True)
                work = jnp.where(work == m, -jnp.inf, work)
                z = z + jnp.exp(m - m0)
            mark = jnp.where(work == -jnp.inf, 1.0, 0.0)
            low = mark[0:half]
            for b in range(1, half):
                low = low + mark[kt + (b - 1) * half:kt + b * half]
            tail = jnp.sum(mark[tail0:], axis=0, keepdims=True)
            n_a = jnp.concatenate([low + jnp.where(ih == 0.0, tail, 0.0), mark[half:kt]], axis=0)
        nkey = jnp.zeros((nk, LANES), F32)
        for a in range(kt):
            nkey = jnp.where(rank1 == float(a), n_a[a:a + 1], nkey)
        p1_ref[0, :, lanes] = jnp.exp(s1 - v1[0:1])
        n_ref[0, :, lanes] = nkey
        r2_ref[0, :, lanes] = rank2.astype(r2_ref.dtype)
        p2_ref[0, :, lanes] = (jnp.exp(s2 - v2[0:1]) / (2.0 * z)).astype(p2_ref.dtype)
        if exact_ties:
            return None
        ranked = lambda r: jnp.sum(jnp.where(r < float(nk), 1.0, 0.0), axis=0, keepdims=True)
        ok = ((ranked(rank1) == float(kt)) & (ranked(rank2) == float(kt))
              & (jnp.sum(n_a, axis=0, keepdims=True) == float(kt)))
        return jnp.min(jnp.where(ok, 1.0, 0.0))

    def sub(st, carry):
        lanes = [pl.ds(pl.multiple_of((st * ROUTER_GROUPS + i) * LANES, LANES), LANES)
                 for i in range(ROUTER_GROUPS)]
        all_distinct = functools.reduce(jnp.minimum, [route(l, False) for l in lanes])

        @pl.when(all_distinct < 0.5)
        def _():
            for l in lanes:
                route(l, True)

        return carry

    lax.fori_loop(0, nsub // ROUTER_GROUPS, sub, 0)


def _router(qt, sk, tt=512):
    dq2, n = qt.shape
    nh = PEER_HEADS
    nk = PEER_NKEYS
    out = jax.ShapeDtypeStruct((nh, nk, n), F32)
    out_b = jax.ShapeDtypeStruct((nh, nk, n), BF16)
    ospec = pl.BlockSpec((1, nk, tt), lambda i, h: (h, 0, i))
    return pl.pallas_call(
        functools.partial(_router_kernel, nsub=tt // LANES),
        grid=(n // tt, nh),
        in_specs=[pl.BlockSpec((2 * nk, tt), lambda i, h: (h, i)),
                  pl.BlockSpec((2, nk, nk), lambda i, h: (0, 0, 0))],
        out_specs=(ospec, ospec, ospec, ospec),
        out_shape=(out, out, out_b, out_b),
        compiler_params=_cparams("arbitrary", "arbitrary"),
        name="peer_router",
    )(qt, sk)


def _peer_kernel(ht_ref, u_ref, vt_ref, p1_ref, n_ref, r2_ref, p2_ref, x_ref, g_ref, *rest,
                 ne1, nh, final_norm):
    fgain_ref = rest[0] if final_norm else None
    o_ref, acc, gbuf = rest[-3:]
    j = pl.program_id(1)
    nk = PEER_NKEYS

    @pl.when(j == 0)
    def _():
        acc[...] = jnp.zeros_like(acc)

    ht = ht_ref[...]
    for e in range(ne1):
        rows = slice(e * nk, (e + 1) * nk)
        act = _gelu_x2(jnp.dot(u_ref[rows, :], ht, preferred_element_type=F32))
        w = jnp.zeros(act.shape, BF16)
        for h in range(nh):
            p1 = jnp.broadcast_to(p1_ref[h, e:e + 1, :], act.shape).astype(BF16)
            cnt = jnp.broadcast_to(n_ref[h, e:e + 1, :], act.shape).astype(BF16)
            w = w + p1 * jnp.where(r2_ref[h] < cnt, p2_ref[h], jnp.zeros((), BF16))
        gbuf[rows, :] = act.astype(BF16) * w
    acc[...] += jnp.dot(vt_ref[...], gbuf[...], preferred_element_type=F32)

    @pl.when(j == pl.num_programs(1) - 1)
    def _():
        xn = x_ref[...] + g_ref[0] * acc[...].T
        o_ref[...] = _rmsnorm(xn, fgain_ref[...]) if final_norm else xn


def _peer(ht, u_tab, vt_tab, layer, p1, cnt, r2, p2, x2d, g, seq, final_gain=None, tt=512, eb=1024):
    d, n = ht.shape
    ne = u_tab.shape[1]
    nh, nk, _ = p1.shape
    ne1 = eb // nk
    per_batch = seq // tt
    tok = pl.BlockSpec((nh, nk, tt), lambda i, j: (0, 0, i))
    e1b = pl.BlockSpec((nh, ne1, tt), lambda i, j: (0, j, i))
    final_norm = final_gain is not None
    extra_specs = [pl.BlockSpec((1, d), lambda i, j: (0, 0))] if final_norm else []
    extra_args = [final_gain.reshape(1, d)] if final_norm else []
    return pl.pallas_call(
        functools.partial(_peer_kernel, ne1=ne1, nh=nh, final_norm=final_norm),
        grid=(n // tt, ne // eb),
        in_specs=[
            pl.BlockSpec((d, tt), lambda i, j: (0, i)),
            pl.BlockSpec((None, eb, d), lambda i, j: (layer, j, 0)),
            pl.BlockSpec((None, d, eb), lambda i, j: (layer, 0, j)),
            e1b, e1b, tok, tok,
            pl.BlockSpec((tt, d), lambda i, j: (i, 0)),
            pl.BlockSpec((1, 1, d), lambda i, j: (i // per_batch, 0, 0)),
        ] + extra_specs,
        out_specs=pl.BlockSpec((tt, d), lambda i, j: (i, 0)),
        out_shape=jax.ShapeDtypeStruct((n, d), F32),
        scratch_shapes=[pltpu.VMEM((d, tt), F32), pltpu.VMEM((eb, tt), BF16)],
        compiler_params=_cparams("arbitrary", "arbitrary", vmem=60 * 1024 * 1024),
        name="peer_experts",
    )(ht, u_tab, vt_tab, p1, cnt, r2, p2, x2d, g, *extra_args)


def kernel(x, c, w_ada, b_ada, norm_mix, norm_ffn, w_in, ssm_lambda_re, ssm_lambda_im, ssm_log_dt, ssm_b_re, ssm_b_im, ssm_c_re, ssm_c_im, ssm_d, w_glu, w_br_ssm, w_br_attn, w_out, peer_wq, peer_subkeys, peer_u, peer_v, norm_final):
    bsz, seq, d = x.shape
    depth = w_ada.shape[0]
    n = bsz * seq
    ssm_w = ssm_d.shape[1]
    attn_w = w_br_attn.shape[1]
    assert bsz <= SUBLANES and seq % S5_CHUNK == 0

    c_pad = jnp.pad(c, ((0, SUBLANES - bsz), (0, 0)))
    mod_all = _modulation(c_pad, w_ada, b_ada)
    x2d = x.reshape(n, d)
    ts = S5_CHUNK // SUBLANES
    w_in_all = w_in.astype(BF16)
    u_all = peer_u.astype(BF16)
    vt_all = jnp.swapaxes(peer_v, 1, 2).astype(BF16)

    for l in range(depth):
        mods = [mod_all[l, :bsz, i * d:(i + 1) * d].reshape(bsz, 1, d) for i in range(N_MOD)]
        sh_m, sc_m, g_m, sh_f, sc_f, g_f = mods

        u, qkv, gates = _in_proj(x2d, norm_mix[l], sc_m, sh_m, w_in_all, l, seq, ssm_w, 3 * attn_w)

        a_re, a_im, bbar_re, bbar_im, at_re, at_im = _s5_prep(
            ssm_lambda_re[l], ssm_lambda_im[l], ssm_log_dt[l], ssm_b_re[l], ssm_b_im[l], ts)
        bblk, cblk = _s5_block_weights(bbar_re, bbar_im, ssm_c_re[l], ssm_c_im[l])
        y_s = _s5_branch(u, seq, a_re, a_im, at_re, at_im, bblk, cblk, ssm_d[l],
                         w_glu[l].astype(BF16))
        y_a = _dilated_attention(qkv, seq)
        merged = _merge(y_s, y_a, w_br_ssm[l].astype(BF16), w_br_attn[l].astype(BF16), gates)
        x2d, ht = _outproj_norm(merged, w_out[l].astype(BF16), x2d, g_m, norm_ffn[l], sc_f, sh_f, seq)
        qt = _matmul(peer_wq[l].T.astype(BF16), ht, col0=0, ncols=n, out_dtype=F32,
                     tm=1024, tn=1024, name="peer_query")
        p1, cnt, r2, p2 = _router(qt, peer_subkeys[l].astype(BF16))
        x2d = _peer(ht, u_all, vt_all, l, p1, cnt, r2, p2, x2d, g_f, seq,
                    final_gain=norm_final if l == depth - 1 else None)

    return x2d.reshape(bsz, seq, d)
```

```python
import functools
import math

import jax
import jax.numpy as jnp
import numpy as np
from jax import lax
from jax.experimental import pallas as pl
from jax.experimental.pallas import tpu as pltpu

F32 = jnp.float32
BF16 = jnp.bfloat16

EPS = 1e-6
N_MOD = 6
SSM_GROUP = 16
SSM_STATE = 64
HEAD_DIM = 128
ATTN_BLOCK = 128
ATTN_GROUP = 8
DILATED_PATTERNS = ((128, 1), (512, 4), (2048, 16))
PEER_HEADS = 8
PEER_NKEYS = 128
PEER_TOPK = 16
ROUTER_GROUPS = 2

LANES = 128
SUBLANES = 8
S5_CHUNK = 256
S5_LANE_BLOCK = 512
S5_CH_BLOCK = 128
VMEM_LIMIT = 56 * 1024 * 1024


def _cparams(*sem, vmem=VMEM_LIMIT):
    return pltpu.CompilerParams(dimension_semantics=sem, vmem_limit_bytes=vmem)


def _gelu_x2(x):
    return x * (1.0 + lax.erf(x * np.float32(math.sqrt(0.5))))


def _gelu(x):
    return 0.5 * _gelu_x2(x)


def _split3(x):
    hi = x.astype(BF16)
    r1 = x - hi.astype(F32)
    mid = r1.astype(BF16)
    lo = (r1 - mid.astype(F32)).astype(BF16)
    return hi, mid, lo


def _mod_kernel(c_ref, w_ref, b_ref, o_ref):
    c = c_ref[...]
    ca = c * jax.nn.sigmoid(c)
    w = w_ref[0]
    c_hi = ca.astype(BF16)
    c_lo = (ca - c_hi.astype(F32)).astype(BF16)
    w_hi = w.astype(BF16)
    w_lo = (w - w_hi.astype(F32)).astype(BF16)
    acc = jnp.dot(c_hi, w_hi, preferred_element_type=F32)
    acc += jnp.dot(c_lo, w_hi, preferred_element_type=F32)
    acc += jnp.dot(c_hi, w_lo, preferred_element_type=F32)
    o_ref[0] = acc + b_ref[0]


def _modulation(c_pad, w_ada, b_ada, tn=1024):
    depth, d, cols = w_ada.shape
    return pl.pallas_call(
        _mod_kernel,
        grid=(depth, cols // tn),
        in_specs=[
            pl.BlockSpec((SUBLANES, d), lambda l, j: (0, 0)),
            pl.BlockSpec((1, d, tn), lambda l, j: (l, 0, j)),
            pl.BlockSpec((1, 1, tn), lambda l, j: (l, 0, j)),
        ],
        out_specs=pl.BlockSpec((1, SUBLANES, tn), lambda l, j: (l, 0, j)),
        out_shape=jax.ShapeDtypeStruct((depth, SUBLANES, cols), F32),
        compiler_params=_cparams("arbitrary", "arbitrary"),
        name="adaln_mod",
    )(c_pad, w_ada, b_ada.reshape(depth, 1, cols))


def _rmsnorm(x, gain):
    ms = jnp.mean(x * x, axis=-1, keepdims=True)
    return x * lax.rsqrt(ms + EPS) * gain


def _norm_mod_kernel(x_ref, gain_ref, sc_ref, sh_ref, o_ref):
    h = _rmsnorm(x_ref[...], gain_ref[...]) * (1.0 + sc_ref[0]) + sh_ref[0]
    o_ref[...] = h.astype(o_ref.dtype)


def _norm_mod(x2d, gain, sc, sh, seq, tm=512):
    n, d = x2d.shape
    per_batch = seq // tm
    mod = pl.BlockSpec((1, 1, d), lambda i: (i // per_batch, 0, 0))
    return pl.pallas_call(
        _norm_mod_kernel,
        grid=(n // tm,),
        in_specs=[pl.BlockSpec((tm, d), lambda i: (i, 0)),
                  pl.BlockSpec((1, d), lambda i: (0, 0)),
                  mod, mod],
        out_specs=pl.BlockSpec((tm, d), lambda i: (i, 0)),
        out_shape=jax.ShapeDtypeStruct((n, d), BF16),
        compiler_params=_cparams("arbitrary"),
        name="norm_mod",
    )(x2d, gain.reshape(1, d), sc, sh)


def _in_proj_kernel(h_ref, w_ref, u_ref, qkv_ref, gates_ref, *, nu, nqkv):
    j = pl.program_id(1)

    def proj():
        return jnp.dot(h_ref[...], w_ref[...], preferred_element_type=F32)

    @pl.when(j < nu)
    def _():
        u_ref[...] = proj()

    @pl.when((j >= nu) & (j < nu + nqkv))
    def _():
        qkv_ref[...] = proj()

    @pl.when(j >= nu + nqkv)
    def _():
        gates_ref[...] = jax.nn.sigmoid(proj()).astype(gates_ref.dtype)


def _in_proj(h, w_all, layer, u_cols, qkv_cols, tm=1024, tn=1024):
    n, d = h.shape
    cols = w_all.shape[2]
    nu, nqkv = u_cols // tn, qkv_cols // tn
    ng = cols // tn - nu - nqkv
    return pl.pallas_call(
        functools.partial(_in_proj_kernel, nu=nu, nqkv=nqkv),
        grid=(n // tm, cols // tn),
        in_specs=[
            pl.BlockSpec((tm, d), lambda i, j: (i, 0)),
            pl.BlockSpec((None, d, tn), lambda i, j: (layer, 0, j)),
        ],
        out_specs=(
            pl.BlockSpec((tm, tn), lambda i, j: (i, jnp.minimum(j, nu - 1))),
            pl.BlockSpec((tm, tn), lambda i, j: (i, jnp.clip(j - nu, 0, nqkv - 1))),
            pl.BlockSpec((tm, tn), lambda i, j: (i, jnp.clip(j - nu - nqkv, 0, ng - 1))),
        ),
        out_shape=(jax.ShapeDtypeStruct((n, u_cols), F32),
                   jax.ShapeDtypeStruct((n, qkv_cols), F32),
                   jax.ShapeDtypeStruct((n, ng * tn), BF16)),
        compiler_params=_cparams("arbitrary", "arbitrary"),
        name="in_proj",
    )(h, w_all)


def _mm_kernel(a_ref, w_ref, o_ref, *, act):
    acc = jnp.dot(a_ref[...], w_ref[...], preferred_element_type=F32)
    if act == "sigmoid":
        acc = jax.nn.sigmoid(acc)
    o_ref[...] = acc.astype(o_ref.dtype)


def _matmul(a, w, *, col0, ncols, out_dtype, act=None, tm, tn, name, layer=None):
    m, k = a.shape
    off = col0 // tn
    if layer is None:
        w_spec = pl.BlockSpec((k, tn), lambda i, j: (0, j + off))
    else:
        w_spec = pl.BlockSpec((None, k, tn), lambda i, j: (layer, 0, j + off))
    return pl.pallas_call(
        functools.partial(_mm_kernel, act=act),
        grid=(m // tm, ncols // tn),
        in_specs=[pl.BlockSpec((tm, k), lambda i, j: (i, 0)), w_spec],
        out_specs=pl.BlockSpec((tm, tn), lambda i, j: (i, j)),
        out_shape=jax.ShapeDtypeStruct((m, ncols), out_dtype),
        compiler_params=_cparams("arbitrary", "arbitrary"),
        name=name,
    )(a, w)


def _merge_kernel(ys_ref, ya_ref, ws_ref, wa_ref, gs_ref, ga_ref, o_ref):
    ps = jnp.dot(ys_ref[...], ws_ref[...], preferred_element_type=F32)
    pa = jnp.dot(ya_ref[...], wa_ref[...], preferred_element_type=F32)
    o_ref[...] = (gs_ref[...].astype(F32) * ps
                  + ga_ref[...].astype(F32) * pa).astype(o_ref.dtype)


def _merge(ys, ya, ws, wa, gates, tm=1024, tn=1024):
    n, ks = ys.shape
    ka = ya.shape[1]
    d = ws.shape[1]
    goff = d // tn
    return pl.pallas_call(
        _merge_kernel,
        grid=(n // tm, d // tn),
        in_specs=[
            pl.BlockSpec((tm, ks), lambda i, j: (i, 0)),
            pl.BlockSpec((tm, ka), lambda i, j: (i, 0)),
            pl.BlockSpec((ks, tn), lambda i, j: (0, j)),
            pl.BlockSpec((ka, tn), lambda i, j: (0, j)),
            pl.BlockSpec((tm, tn), lambda i, j: (i, j)),
            pl.BlockSpec((tm, tn), lambda i, j: (i, j + goff)),
        ],
        out_specs=pl.BlockSpec((tm, tn), lambda i, j: (i, j)),
        out_shape=jax.ShapeDtypeStruct((n, d), BF16),
        compiler_params=_cparams("arbitrary", "arbitrary"),
        name="branch_merge",
    )(ys, ya, ws, wa, gates, gates)


def _outproj_norm_kernel(a_ref, w_ref, x_ref, g_ref, gain_ref, sc_ref, sh_ref, xo_ref, ht_ref):
    acc = jnp.dot(a_ref[...], w_ref[...], preferred_element_type=F32)
    xn = x_ref[...] + g_ref[0] * acc
    xo_ref[...] = xn
    h = _rmsnorm(xn, gain_ref[...]) * (1.0 + sc_ref[0]) + sh_ref[0]
    ht_ref[...] = h.T.astype(ht_ref.dtype)


def _outproj_norm(a, w, x2d, g, gain, sc, sh, seq, tm=512):
    n, k = a.shape
    d = w.shape[1]
    per_batch = seq // tm
    mod = pl.BlockSpec((1, 1, d), lambda i: (i // per_batch, 0, 0))
    return pl.pallas_call(
        _outproj_norm_kernel,
        grid=(n // tm,),
        in_specs=[
            pl.BlockSpec((tm, k), lambda i: (i, 0)),
            pl.BlockSpec((k, d), lambda i: (0, 0)),
            pl.BlockSpec((tm, d), lambda i: (i, 0)),
            mod,
            pl.BlockSpec((1, d), lambda i: (0, 0)),
            mod, mod,
        ],
        out_specs=(pl.BlockSpec((tm, d), lambda i: (i, 0)),
                   pl.BlockSpec((d, tm), lambda i: (0, i))),
        out_shape=(jax.ShapeDtypeStruct((n, d), F32), jax.ShapeDtypeStruct((d, n), BF16)),
        compiler_params=_cparams("arbitrary"),
        name="out_proj_norm",
    )(a, w, x2d, g, gain.reshape(1, d), sc, sh)


def _s5_prep_kernel(lr_ref, li_ref, ldt_ref, br_ref, bi_ref,
                    are_ref, aim_ref, bbr_ref, bbi_ref, atr_ref, ati_ref, *, ts):
    lr, li = lr_ref[...], li_ref[...]
    dt = jnp.exp(ldt_ref[...])
    mag = jnp.exp(lr * dt)
    a_re = mag * jnp.cos(li * dt)
    a_im = mag * jnp.sin(li * dt)
    inv = 1.0 / (lr * lr + li * li)
    coef_re = ((a_re - 1.0) * lr + a_im * li) * inv
    coef_im = (a_im * lr - (a_re - 1.0) * li) * inv
    br, bi = br_ref[...], bi_ref[...]
    bbr_ref[...] = coef_re * br - coef_im * bi
    bbi_ref[...] = coef_re * bi + coef_im * br
    are_ref[...] = a_re
    aim_ref[...] = a_im
    pr, pi = a_re, a_im
    for _ in range(ts - 1):
        pr, pi = pr * a_re - pi * a_im, pr * a_im + pi * a_re
    atr_ref[...] = pr
    ati_ref[...] = pi


def _s5_prep(lam_re, lam_im, log_dt, b_re, b_im, ts):
    g, p, c = b_re.shape
    ns = g * p
    row = lambda t: t.reshape(1, ns)
    ldt = jnp.broadcast_to(log_dt[:, None], (g, p))
    bt = lambda t: jnp.transpose(t, (2, 0, 1)).reshape(c, ns)
    shapes = (
        jax.ShapeDtypeStruct((1, ns), F32), jax.ShapeDtypeStruct((1, ns), F32),
        jax.ShapeDtypeStruct((c, ns), F32), jax.ShapeDtypeStruct((c, ns), F32),
        jax.ShapeDtypeStruct((1, ns), F32), jax.ShapeDtypeStruct((1, ns), F32),
    )
    return pl.pallas_call(
        functools.partial(_s5_prep_kernel, ts=ts),
        out_shape=shapes,
        name="s5_prep",
    )(row(lam_re), row(lam_im), row(ldt), bt(b_re), bt(b_im))


def _s5_block_weights(bbar_re, bbar_im, c_re, c_im):
    c, ns = bbar_re.shape
    gpb = S5_CH_BLOCK // c
    nk = ns // (gpb * SSM_STATE)
    eye = jnp.eye(gpb, dtype=F32)

    def bpart(t):
        t = t.reshape(c, nk, gpb, SSM_STATE)
        t = jnp.einsum("ckjp,ij->kicjp", t, eye)
        return t.reshape(nk, gpb * c, gpb * SSM_STATE)

    def cpart(t):
        t = t.reshape(nk, gpb, c, SSM_STATE)
        t = jnp.einsum("kicp,ij->kjpic", t, eye)
        return t.reshape(nk, gpb * SSM_STATE, gpb * c)

    bblk = jnp.concatenate([bpart(bbar_re), bpart(bbar_im)], axis=2).astype(BF16)
    cblk = jnp.concatenate([cpart(c_re), cpart(-c_im)], axis=1).astype(BF16)
    return bblk, cblk


def _s5_kernel(u_ref, pm_ref, pmt_ref, bblk_ref, are_ref, aim_ref, atr_ref, ati_ref,
               cblk_ref, d_ref, wglu_ref, o_ref,
               hr, hi, car_r, car_i, cs_r, cs_i, yp, *, ts, nk):
    lw = S5_LANE_BLOCK
    cw = S5_CH_BLOCK

    @pl.when(pl.program_id(1) == 0)
    def _():
        car_r[...] = jnp.zeros_like(car_r)
        car_i[...] = jnp.zeros_like(car_i)

    u = u_ref[...]
    up = jnp.dot(pm_ref[...], u.astype(BF16), preferred_element_type=F32).astype(BF16)
    for k in range(nk):
        bu = jnp.dot(up[:, k * cw:(k + 1) * cw], bblk_ref[k], preferred_element_type=F32)
        hr[:, k * lw:(k + 1) * lw] = bu[:, :lw]
        hi[:, k * lw:(k + 1) * lw] = bu[:, lw:]

    for k in range(nk):
        sl = pl.ds(k * lw, lw)
        ar = are_ref[:, sl]
        ai = aim_ref[:, sl]

        def end_step(j, carry, sl=sl, ar=ar, ai=ai):
            cr, cim = carry
            rows = pl.ds(pl.multiple_of(j * SUBLANES, SUBLANES), SUBLANES)
            return ar * cr - ai * cim + hr[rows, sl], ar * cim + ai * cr + hi[rows, sl]

        zero = jnp.zeros((SUBLANES, lw), F32)
        end_r, end_i = lax.fori_loop(0, ts, end_step, (zero, zero), unroll=True)
        cs_r[:, sl] = end_r
        cs_i[:, sl] = end_i

    c_r = car_r[...]
    c_i = car_i[...]
    at_r = atr_ref[...]
    at_i = ati_ref[...]
    for s in range(SUBLANES):
        l_r = cs_r[s:s + 1, :]
        l_i = cs_i[s:s + 1, :]
        cs_r[s:s + 1, :] = c_r
        cs_i[s:s + 1, :] = c_i
        c_r, c_i = l_r + at_r * c_r - at_i * c_i, l_i + at_r * c_i + at_i * c_r
    car_r[...] = c_r
    car_i[...] = c_i

    for k in range(nk):
        sl = pl.ds(k * lw, lw)
        ar = are_ref[:, sl]
        ai = aim_ref[:, sl]

        def scan_step(j, carry, sl=sl, ar=ar, ai=ai):
            cr, cim = carry
            rows = pl.ds(pl.multiple_of(j * SUBLANES, SUBLANES), SUBLANES)
            nr = ar * cr - ai * cim + hr[rows, sl]
            ni = ar * cim + ai * cr + hi[rows, sl]
            hr[rows, sl] = nr
            hi[rows, sl] = ni
            return nr, ni

        lax.fori_loop(0, ts, scan_step, (cs_r[:, sl], cs_i[:, sl]), unroll=True)
        hcat = jnp.concatenate([hr[:, sl], hi[:, sl]], axis=1).astype(BF16)
        yp[:, k * cw:(k + 1) * cw] = jnp.dot(hcat, cblk_ref[k], preferred_element_type=F32)

    y1, y2, y3 = _split3(yp[...])
    pmt = pmt_ref[...]
    y = (jnp.dot(pmt, y1, preferred_element_type=F32)
         + jnp.dot(pmt, y2, preferred_element_type=F32)
         + jnp.dot(pmt, y3, preferred_element_type=F32))
    y = y + d_ref[...] * u
    y = _gelu(y)
    gate = jax.nn.sigmoid(jnp.dot(y.astype(BF16), wglu_ref[...], preferred_element_type=F32))
    o_ref[...] = (y * gate).astype(o_ref.dtype)


def _s5_permutation(t_rows, ts):
    r = np.arange(t_rows)
    src = (r % SUBLANES) * ts + r // SUBLANES
    pm = np.zeros((t_rows, t_rows), np.float32)
    pm[r, src] = 1.0
    return jnp.asarray(pm, BF16), jnp.asarray(pm.T, BF16)


def _s5_branch(u, seq, a_re, a_im, at_re, at_im, bblk, cblk, d_skip, w_glu):
    n, w = u.shape
    ns = a_re.shape[1]
    t_rows = S5_CHUNK
    ts = t_rows // SUBLANES
    nk = ns // S5_LANE_BLOCK
    nchunk = seq // t_rows
    pm, pmt = _s5_permutation(t_rows, ts)
    full = lambda shape: pl.BlockSpec(shape, lambda b, c: (0,) * len(shape))
    rep8 = lambda row: jnp.broadcast_to(row, (SUBLANES, ns))
    return pl.pallas_call(
        functools.partial(_s5_kernel, ts=ts, nk=nk),
        grid=(n // seq, nchunk),
        in_specs=[
            pl.BlockSpec((t_rows, w), lambda b, c: (b * nchunk + c, 0)),
            full((t_rows, t_rows)), full((t_rows, t_rows)),
            full(bblk.shape),
            full((SUBLANES, ns)), full((SUBLANES, ns)), full((1, ns)), full((1, ns)),
            full(cblk.shape),
            full((1, w)), full((w, w)),
        ],
        out_specs=pl.BlockSpec((t_rows, w), lambda b, c: (b * nchunk + c, 0)),
        out_shape=jax.ShapeDtypeStruct((n, w), BF16),
        scratch_shapes=[
            pltpu.VMEM((t_rows, ns), F32), pltpu.VMEM((t_rows, ns), F32),
            pltpu.VMEM((1, ns), F32), pltpu.VMEM((1, ns), F32),
            pltpu.VMEM((SUBLANES, ns), F32), pltpu.VMEM((SUBLANES, ns), F32),
            pltpu.VMEM((t_rows, w), F32),
        ],
        compiler_params=_cparams("arbitrary", "arbitrary"),
        name="s5_branch",
    )(u, pm, pmt, bblk, rep8(a_re), rep8(a_im), at_re, at_im, cblk, d_skip.reshape(1, w), w_glu)


def _attn_kernel(q_ref, k_ref, v_ref, o_ref, o_s, lse_s, *, seq, patterns, scale):
    blk = ATTN_BLOCK
    qi = lax.broadcasted_iota(jnp.int32, (blk, 2 * blk), 0)
    kj = lax.broadcasted_iota(jnp.int32, (blk, 2 * blk), 1)
    dist = blk + qi - kj
    nt = (((1,), (1,)), ((), ()))

    for pi, (window, dil) in enumerate(patterns):
        n_back = window // dil
        nblk = seq // dil // blk
        grp = min(ATTN_GROUP, nblk)
        ngrp = nblk // grp
        band = (dist >= 0) & (dist <= n_back)

        nres = max(1, ATTN_GROUP // grp)

        def group_step(idx, carry, pi=pi, dil=dil, grp=grp, ngrp=ngrp, nres=nres, band=band):
            r0 = (idx // ngrp) * nres
            n0 = (idx % ngrp) * grp

            def rows_of(r, nb):
                start = r + nb * (blk * dil)
                return pl.ds(start, blk) if dil == 1 else pl.ds(start, blk, stride=dil)

            chains = [[rows_of(r0 + ri, jnp.maximum(n0 - 1, 0))]
                      + [rows_of(r0 + ri, n0 + gi) for gi in range(grp)] for ri in range(nres)]
            units = [(ri, gi) for ri in range(nres) for gi in range(grp)]
            ks = [[k_ref[rw, :].astype(BF16) for rw in rows] for rows in chains]
            scores = []
            for ri, gi in units:
                q = q_ref[chains[ri][gi + 1], :].astype(BF16)
                scores.append(jnp.concatenate(
                    [lax.dot_general(q, ks[ri][gi], nt, preferred_element_type=F32),
                     lax.dot_general(q, ks[ri][gi + 1], nt, preferred_element_type=F32)], axis=1))
            probs, inv_l = [], []
            for (ri, gi), sc in zip(units, scores):
                rows = chains[ri][gi + 1]
                mask = band & ((kj >= blk) | (n0 > 0)) if gi == 0 else band
                s = jnp.where(mask, sc * scale, -jnp.inf)
                m = jnp.max(s, axis=1, keepdims=True)
                p = jnp.exp(s - m)
                l = jnp.sum(p, axis=1, keepdims=True)
                lse_s[pi, rows, :] = jnp.broadcast_to(m + jnp.log(l), (blk, HEAD_DIM))
                inv_l.append(1.0 / l)
                probs.append(p.astype(BF16))
            vs = [[v_ref[rw, :].astype(BF16) for rw in rows] for rows in chains]
            for (ri, gi), pb, il in zip(units, probs, inv_l):
                o_s[pi, chains[ri][gi + 1], :] = il * (
                    jnp.dot(pb[:, :blk], vs[ri][gi], preferred_element_type=F32)
                    + jnp.dot(pb[:, blk:], vs[ri][gi + 1], preferred_element_type=F32))
            return carry

        lax.fori_loop(0, (dil // nres) * ngrp, group_step, 0, unroll=2)

    npat = len(patterns)

    def merge_step(t, carry):
        rows = pl.ds(pl.multiple_of(t * blk, blk), blk)
        lses = [lse_s[pi, rows, :] for pi in range(npat)]
        mx = functools.reduce(jnp.maximum, lses)
        num = jnp.zeros((blk, HEAD_DIM), F32)
        den = jnp.zeros((blk, HEAD_DIM), F32)
        for pi in range(npat):
            w = jnp.exp(lses[pi] - mx)
            num = num + w * o_s[pi, rows, :]
            den = den + w
        o_ref[rows, :] = (num / den).astype(o_ref.dtype)
        return carry

    lax.fori_loop(0, seq // blk, merge_step, 0)


def _dilated_attention(qkv, seq):
    n = qkv.shape[0]
    nh = qkv.shape[1] // (3 * HEAD_DIM)
    for window, dil in DILATED_PATTERNS:
        assert window // dil <= ATTN_BLOCK and seq % (dil * ATTN_BLOCK) == 0
    blk = (seq, HEAD_DIM)
    return pl.pallas_call(
        functools.partial(_attn_kernel, seq=seq, patterns=DILATED_PATTERNS,
                          scale=np.float32(HEAD_DIM ** -0.5)),
        grid=(n // seq, nh),
        in_specs=[pl.BlockSpec(blk, lambda b, h: (b, h)),
                  pl.BlockSpec(blk, lambda b, h: (b, nh + h)),
                  pl.BlockSpec(blk, lambda b, h: (b, 2 * nh + h))],
        out_specs=pl.BlockSpec(blk, lambda b, h: (b, h)),
        out_shape=jax.ShapeDtypeStruct((n, nh * HEAD_DIM), BF16),
        scratch_shapes=[pltpu.VMEM((len(DILATED_PATTERNS),) + blk, F32)] * 2,
        compiler_params=_cparams("arbitrary", "arbitrary"),
        name="dilated_attn",
    )(qkv, qkv, qkv)


def _top16(s, kid, exact_ties):
    nkeys = s.shape[0]
    vals = []
    if exact_ties:
        rank = jnp.full(s.shape, float(nkeys), F32)
        for it in range(PEER_TOPK):
            m = jnp.max(s, axis=0, keepdims=True)
            idx = jnp.min(jnp.where(s == m, kid, float(nkeys)), axis=0, keepdims=True)
            sel = kid == idx
            rank = jnp.where(sel, float(it), rank)
            s = jnp.where(sel, -jnp.inf, s)
            vals.append(m)
        return rank, jnp.concatenate(vals, axis=0)
    floor = np.float32(-(2.0 ** 125))
    for it in range(PEER_TOPK):
        m = jnp.max(s, axis=0, keepdims=True)
        s = jnp.where(s == m, np.float32(-(2.0 ** 126 + it * 2.0 ** 103)), s)
        vals.append(m)
    step = (-s - np.float32(2.0 ** 126)) * np.float32(2.0 ** -103)
    rank = jnp.where(s <= floor, jnp.maximum(step, 0.0), float(nkeys))
    return rank, jnp.concatenate(vals, axis=0)


def _router_kernel(q_ref, sk_ref, p1_ref, n_ref, r2_ref, p2_ref, *, nsub):
    nk = PEER_NKEYS
    kt = PEER_TOPK
    half = kt // 2
    kid = lax.broadcasted_iota(jnp.int32, (nk, LANES), 0).astype(F32)
    ia = lax.broadcasted_iota(jnp.int32, (kt, LANES), 0).astype(F32)
    ih = ia[0:half]
    cid = jnp.concatenate([ia * kt] + [ih * kt + float(b) for b in range(1, half)] + [ih + float(half)],
                          axis=0)
    ncand = kt * kt
    tail0 = kt + (half - 1) * half

    def route(lanes, exact_ties):
        q1 = q_ref[0:nk, lanes].astype(BF16)
        q2 = q_ref[nk:2 * nk, lanes].astype(BF16)
        s1 = jnp.dot(sk_ref[0], q1, preferred_element_type=F32)
        s2 = jnp.dot(sk_ref[1], q2, preferred_element_type=F32)
        rank1, v1 = _top16(s1, kid, exact_ties)
        rank2, v2 = _top16(s2, kid, exact_ties)
        work = jnp.concatenate([v1 + v2[0:1]] + [v1[0:half] + v2[b:b + 1] for b in range(1, half)]
                               + [v1[0:1] + v2[half:kt]], axis=0)
        m0 = v1[0:1] + v2[0:1]
        z = jnp.zeros((1, LANES), F32)
        if exact_ties:
            n_a = jnp.zeros((kt, LANES), F32)
            for _ in range(kt):
                m = jnp.max(work, axis=0, keepdims=True)
                idx = jnp.min(jnp.where(work == m, cid, float(ncand)), axis=0, keepdims=True)
                work = jnp.where(cid == idx, -jnp.inf, work)
                a_sel = jnp.floor(idx * (1.0 / kt))
                n_a = n_a + jnp.where(ia == a_sel, 1.0, 0.0)
                z = z + jnp.exp(m - m0)
        else:
            for _ in range(kt):
                m = jnp.max(work, axis=0, keepdims=True)
                work = jnp.where(work == m, -jnp.inf, work)
                z = z + jnp.exp(m - m0)
            mark = jnp.where(work == -jnp.inf, 1.0, 0.0)
            low = mark[0:half]
            for b in range(1, half):
                low = low + mark[kt + (b - 1) * half:kt + b * half]
            tail = jnp.sum(mark[tail0:], axis=0, keepdims=True)
            n_a = jnp.concatenate([low + jnp.where(ih == 0.0, tail, 0.0), mark[half:kt]], axis=0)
        nkey = jnp.zeros((nk, LANES), F32)
        for a in range(kt):
            nkey = jnp.where(rank1 == float(a), n_a[a:a + 1], nkey)
        p1_ref[0, :, lanes] = jnp.exp(s1 - v1[0:1])
        n_ref[0, :, lanes] = nkey
        r2_ref[0, :, lanes] = rank2.astype(r2_ref.dtype)
        p2_ref[0, :, lanes] = (jnp.exp(s2 - v2[0:1]) / (2.0 * z)).astype(p2_ref.dtype)
        if exact_ties:
            return None
        ranked = lambda r: jnp.sum(jnp.where(r < float(nk), 1.0, 0.0), axis=0, keepdims=True)
        ok = ((ranked(rank1) == float(kt)) & (ranked(rank2) == float(kt))
              & (jnp.sum(n_a, axis=0, keepdims=True) == float(kt)))
        return jnp.min(jnp.where(ok, 1.0, 0.0))

    def sub(st, carry):
        lanes = [pl.ds(pl.multiple_of((st * ROUTER_GROUPS + i) * LANES, LANES), LANES)
                 for i in range(ROUTER_GROUPS)]
        all_distinct = functools.reduce(jnp.minimum, [route(l, False) for l in lanes])

        @pl.when(all_distinct < 0.5)
        def _():
            for l in lanes:
                route(l, True)

        return carry

    lax.fori_loop(0, nsub // ROUTER_GROUPS, sub, 0)


def _router(qt, sk, tt=512):
    dq2, n = qt.shape
    nh = PEER_HEADS
    nk = PEER_NKEYS
    out = jax.ShapeDtypeStruct((nh, nk, n), F32)
    out_b = jax.ShapeDtypeStruct((nh, nk, n), BF16)
    ospec = pl.BlockSpec((1, nk, tt), lambda i, h: (h, 0, i))
    return pl.pallas_call(
        functools.partial(_router_kernel, nsub=tt // LANES),
        grid=(n // tt, nh),
        in_specs=[pl.BlockSpec((2 * nk, tt), lambda i, h: (h, i)),
                  pl.BlockSpec((2, nk, nk), lambda i, h: (0, 0, 0))],
        out_specs=(ospec, ospec, ospec, ospec),
        out_shape=(out, out, out_b, out_b),
        compiler_params=_cparams("arbitrary", "arbitrary"),
        name="peer_router",
    )(qt, sk)


def _peer_kernel(ht_ref, u_ref, vt_ref, p1_ref, n_ref, r2_ref, p2_ref, x_ref, g_ref, *rest,
                 ne1, nh, final_norm):
    fgain_ref = rest[0] if final_norm else None
    o_ref, acc, gbuf = rest[-3:]
    j = pl.program_id(1)
    nk = PEER_NKEYS

    @pl.when(j == 0)
    def _():
        acc[...] = jnp.zeros_like(acc)

    ht = ht_ref[...]
    for e in range(ne1):
        rows = slice(e * nk, (e + 1) * nk)
        act = _gelu_x2(jnp.dot(u_ref[rows, :], ht, preferred_element_type=F32))
        w = jnp.zeros(act.shape, BF16)
        for h in range(nh):
            p1 = jnp.broadcast_to(p1_ref[h, e:e + 1, :], act.shape).astype(BF16)
            cnt = jnp.broadcast_to(n_ref[h, e:e + 1, :], act.shape).astype(BF16)
            w = w + p1 * jnp.where(r2_ref[h] < cnt, p2_ref[h], jnp.zeros((), BF16))
        gbuf[rows, :] = act.astype(BF16) * w
    acc[...] += jnp.dot(vt_ref[...], gbuf[...], preferred_element_type=F32)

    @pl.when(j == pl.num_programs(1) - 1)
    def _():
        xn = x_ref[...] + g_ref[0] * acc[...].T
        o_ref[...] = _rmsnorm(xn, fgain_ref[...]) if final_norm else xn


def _peer(ht, u_tab, vt_tab, layer, p1, cnt, r2, p2, x2d, g, seq, final_gain=None, tt=512, eb=1024):
    d, n = ht.shape
    ne = u_tab.shape[1]
    nh, nk, _ = p1.shape
    ne1 = eb // nk
    per_batch = seq // tt
    tok = pl.BlockSpec((nh, nk, tt), lambda i, j: (0, 0, i))
    e1b = pl.BlockSpec((nh, ne1, tt), lambda i, j: (0, j, i))
    final_norm = final_gain is not None
    extra_specs = [pl.BlockSpec((1, d), lambda i, j: (0, 0))] if final_norm else []
    extra_args = [final_gain.reshape(1, d)] if final_norm else []
    return pl.pallas_call(
        functools.partial(_peer_kernel, ne1=ne1, nh=nh, final_norm=final_norm),
        grid=(n // tt, ne // eb),
        in_specs=[
            pl.BlockSpec((d, tt), lambda i, j: (0, i)),
            pl.BlockSpec((None, eb, d), lambda i, j: (layer, j, 0)),
            pl.BlockSpec((None, d, eb), lambda i, j: (layer, 0, j)),
            e1b, e1b, tok, tok,
            pl.BlockSpec((tt, d), lambda i, j: (i, 0)),
            pl.BlockSpec((1, 1, d), lambda i, j: (i // per_batch, 0, 0)),
        ] + extra_specs,
        out_specs=pl.BlockSpec((tt, d), lambda i, j: (i, 0)),
        out_shape=jax.ShapeDtypeStruct((n, d), F32),
        scratch_shapes=[pltpu.VMEM((d, tt), F32), pltpu.VMEM((eb, tt), BF16)],
        compiler_params=_cparams("arbitrary", "arbitrary", vmem=60 * 1024 * 1024),
        name="peer_experts",
    )(ht, u_tab, vt_tab, p1, cnt, r2, p2, x2d, g, *extra_args)


def kernel(x, c, w_ada, b_ada, norm_mix, norm_ffn, w_in, ssm_lambda_re, ssm_lambda_im, ssm_log_dt, ssm_b_re, ssm_b_im, ssm_c_re, ssm_c_im, ssm_d, w_glu, w_br_ssm, w_br_attn, w_out, peer_wq, peer_subkeys, peer_u, peer_v, norm_final):
    bsz, seq, d = x.shape
    depth = w_ada.shape[0]
    n = bsz * seq
    ssm_w = ssm_d.shape[1]
    attn_w = w_br_attn.shape[1]
    assert bsz <= SUBLANES and seq % S5_CHUNK == 0

    c_pad = jnp.pad(c, ((0, SUBLANES - bsz), (0, 0)))
    mod_all = _modulation(c_pad, w_ada, b_ada)
    x2d = x.reshape(n, d)
    ts = S5_CHUNK // SUBLANES
    w_in_all = w_in.astype(BF16)
    u_all = peer_u.astype(BF16)
    vt_all = jnp.swapaxes(peer_v, 1, 2).astype(BF16)

    for l in range(depth):
        mods = [mod_all[l, :bsz, i * d:(i + 1) * d].reshape(bsz, 1, d) for i in range(N_MOD)]
        sh_m, sc_m, g_m, sh_f, sc_f, g_f = mods

        h = _norm_mod(x2d, norm_mix[l], sc_m, sh_m, seq)
        u, qkv, gates = _in_proj(h, w_in_all, l, ssm_w, 3 * attn_w)

        a_re, a_im, bbar_re, bbar_im, at_re, at_im = _s5_prep(
            ssm_lambda_re[l], ssm_lambda_im[l], ssm_log_dt[l], ssm_b_re[l], ssm_b_im[l], ts)
        bblk, cblk = _s5_block_weights(bbar_re, bbar_im, ssm_c_re[l], ssm_c_im[l])
        y_s = _s5_branch(u, seq, a_re, a_im, at_re, at_im, bblk, cblk, ssm_d[l],
                         w_glu[l].astype(BF16))
        y_a = _dilated_attention(qkv, seq)
        merged = _merge(y_s, y_a, w_br_ssm[l].astype(BF16), w_br_attn[l].astype(BF16), gates)
        x2d, ht = _outproj_norm(merged, w_out[l].astype(BF16), x2d, g_m, norm_ffn[l], sc_f, sh_f, seq)
        qt = _matmul(peer_wq[l].T.astype(BF16), ht, col0=0, ncols=n, out_dtype=F32,
                     tm=1024, tn=1024, name="peer_query")
        p1, cnt, r2, p2 = _router(qt, peer_subkeys[l].astype(BF16))
        x2d = _peer(ht, u_all, vt_all, l, p1, cnt, r2, p2, x2d, g_f, seq,
                    final_gain=norm_final if l == depth - 1 else None)

    return x2d.reshape(bsz, seq, d)
```

```python
import functools
import math

import jax
import jax.numpy as jnp
import numpy as np
from jax import lax
from jax.experimental import pallas as pl
from jax.experimental.pallas import tpu as pltpu

F32 = jnp.float32
BF16 = jnp.bfloat16

EPS = 1e-6
N_MOD = 6
SSM_GROUP = 16
SSM_STATE = 64
HEAD_DIM = 128
ATTN_BLOCK = 128
ATTN_GROUP = 8
DILATED_PATTERNS = ((128, 1), (512, 4), (2048, 16))
PEER_HEADS = 8
PEER_NKEYS = 128
PEER_TOPK = 16
ROUTER_GROUPS = 2

LANES = 128
SUBLANES = 8
S5_CHUNK = 256
S5_LANE_BLOCK = 512
S5_CH_BLOCK = 128
VMEM_LIMIT = 56 * 1024 * 1024
PEER_VMEM_LIMIT = 60 * 1024 * 1024


def _cparams(*sem, vmem=VMEM_LIMIT):
    return pltpu.CompilerParams(dimension_semantics=sem, vmem_limit_bytes=vmem)


def _gelu_x2(x):
    return x * (1.0 + lax.erf(x * np.float32(math.sqrt(0.5))))


def _gelu(x):
    return 0.5 * _gelu_x2(x)


def _split3(x):
    hi = x.astype(BF16)
    r1 = x - hi.astype(F32)
    mid = r1.astype(BF16)
    lo = (r1 - mid.astype(F32)).astype(BF16)
    return hi, mid, lo


def _mod_kernel(c_ref, w_ref, b_ref, o_ref):
    c = c_ref[...]
    ca = c * jax.nn.sigmoid(c)
    w = w_ref[0]
    c_hi = ca.astype(BF16)
    c_lo = (ca - c_hi.astype(F32)).astype(BF16)
    w_hi = w.astype(BF16)
    w_lo = (w - w_hi.astype(F32)).astype(BF16)
    acc = jnp.dot(c_hi, w_hi, preferred_element_type=F32)
    acc += jnp.dot(c_lo, w_hi, preferred_element_type=F32)
    acc += jnp.dot(c_hi, w_lo, preferred_element_type=F32)
    o_ref[0] = acc + b_ref[0]


def _modulation(c_pad, w_ada, b_ada, tn=1024):
    depth, d, cols = w_ada.shape
    return pl.pallas_call(
        _mod_kernel,
        grid=(depth, cols // tn),
        in_specs=[
            pl.BlockSpec((SUBLANES, d), lambda l, j: (0, 0)),
            pl.BlockSpec((1, d, tn), lambda l, j: (l, 0, j)),
            pl.BlockSpec((1, 1, tn), lambda l, j: (l, 0, j)),
        ],
        out_specs=pl.BlockSpec((1, SUBLANES, tn), lambda l, j: (l, 0, j)),
        out_shape=jax.ShapeDtypeStruct((depth, SUBLANES, cols), F32),
        compiler_params=_cparams("arbitrary", "arbitrary"),
        name="adaln_mod",
    )(c_pad, w_ada, b_ada.reshape(depth, 1, cols))


def _rmsnorm(x, gain):
    ms = jnp.mean(x * x, axis=-1, keepdims=True)
    return x * lax.rsqrt(ms + EPS) * gain


def _norm_mod_kernel(x_ref, gain_ref, sc_ref, sh_ref, o_ref):
    h = _rmsnorm(x_ref[...], gain_ref[...]) * (1.0 + sc_ref[0]) + sh_ref[0]
    o_ref[...] = h.astype(o_ref.dtype)


def _norm_mod(x2d, gain, sc, sh, seq, tm=512):
    n, d = x2d.shape
    per_batch = seq // tm
    mod = pl.BlockSpec((1, 1, d), lambda i: (i // per_batch, 0, 0))
    return pl.pallas_call(
        _norm_mod_kernel,
        grid=(n // tm,),
        in_specs=[pl.BlockSpec((tm, d), lambda i: (i, 0)),
                  pl.BlockSpec((1, d), lambda i: (0, 0)),
                  mod, mod],
        out_specs=pl.BlockSpec((tm, d), lambda i: (i, 0)),
        out_shape=jax.ShapeDtypeStruct((n, d), BF16),
        compiler_params=_cparams("arbitrary"),
        name="norm_mod",
    )(x2d, gain.reshape(1, d), sc, sh)


def _in_proj_kernel(h_ref, w_ref, u_ref, qkv_ref, gates_ref, *, nu, nqkv):
    j = pl.program_id(1)

    def proj():
        return jnp.dot(h_ref[...], w_ref[...], preferred_element_type=F32)

    @pl.when(j < nu)
    def _():
        u_ref[...] = proj()

    @pl.when((j >= nu) & (j < nu + nqkv))
    def _():
        qkv_ref[...] = proj()

    @pl.when(j >= nu + nqkv)
    def _():
        gates_ref[...] = jax.nn.sigmoid(proj()).astype(gates_ref.dtype)


def _in_proj(h, w_all, layer, u_cols, qkv_cols, tm=1024, tn=1024):
    n, d = h.shape
    cols = w_all.shape[2]
    nu, nqkv = u_cols // tn, qkv_cols // tn
    ng = cols // tn - nu - nqkv
    return pl.pallas_call(
        functools.partial(_in_proj_kernel, nu=nu, nqkv=nqkv),
        grid=(n // tm, cols // tn),
        in_specs=[
            pl.BlockSpec((tm, d), lambda i, j: (i, 0)),
            pl.BlockSpec((None, d, tn), lambda i, j: (layer, 0, j)),
        ],
        out_specs=(
            pl.BlockSpec((tm, tn), lambda i, j: (i, jnp.minimum(j, nu - 1))),
            pl.BlockSpec((tm, tn), lambda i, j: (i, jnp.clip(j - nu, 0, nqkv - 1))),
            pl.BlockSpec((tm, tn), lambda i, j: (i, jnp.clip(j - nu - nqkv, 0, ng - 1))),
        ),
        out_shape=(jax.ShapeDtypeStruct((n, u_cols), F32),
                   jax.ShapeDtypeStruct((n, qkv_cols), F32),
                   jax.ShapeDtypeStruct((n, ng * tn), BF16)),
        compiler_params=_cparams("arbitrary", "arbitrary"),
        name="in_proj",
    )(h, w_all)


def _mm_kernel(a_ref, w_ref, o_ref):
    o_ref[...] = jnp.dot(a_ref[...], w_ref[...], preferred_element_type=F32).astype(o_ref.dtype)


def _matmul(a, w, *, out_dtype, tm, tn, name):
    m, k = a.shape
    ncols = w.shape[1]
    return pl.pallas_call(
        _mm_kernel,
        grid=(m // tm, ncols // tn),
        in_specs=[pl.BlockSpec((tm, k), lambda i, j: (i, 0)),
                  pl.BlockSpec((k, tn), lambda i, j: (0, j))],
        out_specs=pl.BlockSpec((tm, tn), lambda i, j: (i, j)),
        out_shape=jax.ShapeDtypeStruct((m, ncols), out_dtype),
        compiler_params=_cparams("arbitrary", "arbitrary"),
        name=name,
    )(a, w)


def _merge_kernel(ys_ref, ya_ref, ws_ref, wa_ref, gs_ref, ga_ref, o_ref):
    ps = jnp.dot(ys_ref[...], ws_ref[...], preferred_element_type=F32)
    pa = jnp.dot(ya_ref[...], wa_ref[...], preferred_element_type=F32)
    o_ref[...] = (gs_ref[...].astype(F32) * ps
                  + ga_ref[...].astype(F32) * pa).astype(o_ref.dtype)


def _merge(ys, ya, ws, wa, gates, tm=1024, tn=1024):
    n, ks = ys.shape
    ka = ya.shape[1]
    d = ws.shape[1]
    goff = d // tn
    return pl.pallas_call(
        _merge_kernel,
        grid=(n // tm, d // tn),
        in_specs=[
            pl.BlockSpec((tm, ks), lambda i, j: (i, 0)),
            pl.BlockSpec((tm, ka), lambda i, j: (i, 0)),
            pl.BlockSpec((ks, tn), lambda i, j: (0, j)),
            pl.BlockSpec((ka, tn), lambda i, j: (0, j)),
            pl.BlockSpec((tm, tn), lambda i, j: (i, j)),
            pl.BlockSpec((tm, tn), lambda i, j: (i, j + goff)),
        ],
        out_specs=pl.BlockSpec((tm, tn), lambda i, j: (i, j)),
        out_shape=jax.ShapeDtypeStruct((n, d), BF16),
        compiler_params=_cparams("arbitrary", "arbitrary"),
        name="branch_merge",
    )(ys, ya, ws, wa, gates, gates)


def _outproj_norm_kernel(a_ref, w_ref, x_ref, g_ref, gain_ref, sc_ref, sh_ref, xo_ref, ht_ref):
    acc = jnp.dot(a_ref[...], w_ref[...], preferred_element_type=F32)
    xn = x_ref[...] + g_ref[0] * acc
    xo_ref[...] = xn
    h = _rmsnorm(xn, gain_ref[...]) * (1.0 + sc_ref[0]) + sh_ref[0]
    ht_ref[...] = h.T.astype(ht_ref.dtype)


def _outproj_norm(a, w, x2d, g, gain, sc, sh, seq, tm=512):
    n, k = a.shape
    d = w.shape[1]
    per_batch = seq // tm
    mod = pl.BlockSpec((1, 1, d), lambda i: (i // per_batch, 0, 0))
    return pl.pallas_call(
        _outproj_norm_kernel,
        grid=(n // tm,),
        in_specs=[
            pl.BlockSpec((tm, k), lambda i: (i, 0)),
            pl.BlockSpec((k, d), lambda i: (0, 0)),
            pl.BlockSpec((tm, d), lambda i: (i, 0)),
            mod,
            pl.BlockSpec((1, d), lambda i: (0, 0)),
            mod, mod,
        ],
        out_specs=(pl.BlockSpec((tm, d), lambda i: (i, 0)),
                   pl.BlockSpec((d, tm), lambda i: (0, i))),
        out_shape=(jax.ShapeDtypeStruct((n, d), F32), jax.ShapeDtypeStruct((d, n), BF16)),
        compiler_params=_cparams("arbitrary"),
        name="out_proj_norm",
    )(a, w, x2d, g, gain.reshape(1, d), sc, sh)


def _s5_prep_kernel(lr_ref, li_ref, ldt_ref, br_ref, bi_ref,
                    are_ref, aim_ref, bbr_ref, bbi_ref, atr_ref, ati_ref, *, ts):
    lr, li = lr_ref[...], li_ref[...]
    dt = jnp.exp(ldt_ref[...])
    mag = jnp.exp(lr * dt)
    a_re = mag * jnp.cos(li * dt)
    a_im = mag * jnp.sin(li * dt)
    inv = 1.0 / (lr * lr + li * li)
    coef_re = ((a_re - 1.0) * lr + a_im * li) * inv
    coef_im = (a_im * lr - (a_re - 1.0) * li) * inv
    br, bi = br_ref[...], bi_ref[...]
    bbr_ref[...] = coef_re * br - coef_im * bi
    bbi_ref[...] = coef_re * bi + coef_im * br
    are_ref[...] = a_re
    aim_ref[...] = a_im
    pr, pi = a_re, a_im
    for _ in range(ts - 1):
        pr, pi = pr * a_re - pi * a_im, pr * a_im + pi * a_re
    atr_ref[...] = pr
    ati_ref[...] = pi


def _s5_prep(lam_re, lam_im, log_dt, b_re, b_im, ts):
    g, p, c = b_re.shape
    ns = g * p
    row = lambda t: t.reshape(1, ns)
    ldt = jnp.broadcast_to(log_dt[:, None], (g, p))
    bt = lambda t: jnp.transpose(t, (2, 0, 1)).reshape(c, ns)
    shapes = (
        jax.ShapeDtypeStruct((1, ns), F32), jax.ShapeDtypeStruct((1, ns), F32),
        jax.ShapeDtypeStruct((c, ns), F32), jax.ShapeDtypeStruct((c, ns), F32),
        jax.ShapeDtypeStruct((1, ns), F32), jax.ShapeDtypeStruct((1, ns), F32),
    )
    return pl.pallas_call(
        functools.partial(_s5_prep_kernel, ts=ts),
        out_shape=shapes,
        name="s5_prep",
    )(row(lam_re), row(lam_im), row(ldt), bt(b_re), bt(b_im))


def _s5_block_weights(bbar_re, bbar_im, c_re, c_im):
    c, ns = bbar_re.shape
    gpb = S5_CH_BLOCK // c
    nk = ns // (gpb * SSM_STATE)
    eye = jnp.eye(gpb, dtype=F32)

    def bpart(t):
        t = t.reshape(c, nk, gpb, SSM_STATE)
        t = jnp.einsum("ckjp,ij->kicjp", t, eye)
        return t.reshape(nk, gpb * c, gpb * SSM_STATE)

    def cpart(t):
        t = t.reshape(nk, gpb, c, SSM_STATE)
        t = jnp.einsum("kicp,ij->kjpic", t, eye)
        return t.reshape(nk, gpb * SSM_STATE, gpb * c)

    bblk = jnp.concatenate([bpart(bbar_re), bpart(bbar_im)], axis=2).astype(BF16)
    cblk = jnp.concatenate([cpart(c_re), cpart(-c_im)], axis=1).astype(BF16)
    return bblk, cblk


def _s5_kernel(u_ref, pm_ref, pmt_ref, bblk_ref, are_ref, aim_ref, atr_ref, ati_ref,
               cblk_ref, d_ref, wglu_ref, o_ref,
               hr, hi, car_r, car_i, cs_r, cs_i, yp, *, ts, nk):
    lw = S5_LANE_BLOCK
    cw = S5_CH_BLOCK

    @pl.when(pl.program_id(1) == 0)
    def _():
        car_r[...] = jnp.zeros_like(car_r)
        car_i[...] = jnp.zeros_like(car_i)

    u = u_ref[...]
    up = jnp.dot(pm_ref[...], u.astype(BF16), preferred_element_type=F32).astype(BF16)
    for k in range(nk):
        bu = jnp.dot(up[:, k * cw:(k + 1) * cw], bblk_ref[k], preferred_element_type=F32)
        hr[:, k * lw:(k + 1) * lw] = bu[:, :lw]
        hi[:, k * lw:(k + 1) * lw] = bu[:, lw:]

    for k in range(nk):
        sl = pl.ds(k * lw, lw)
        ar = are_ref[:, sl]
        ai = aim_ref[:, sl]

        def end_step(j, carry, sl=sl, ar=ar, ai=ai):
            cr, cim = carry
            rows = pl.ds(pl.multiple_of(j * SUBLANES, SUBLANES), SUBLANES)
            return ar * cr - ai * cim + hr[rows, sl], ar * cim + ai * cr + hi[rows, sl]

        zero = jnp.zeros((SUBLANES, lw), F32)
        end_r, end_i = lax.fori_loop(0, ts, end_step, (zero, zero), unroll=True)
        cs_r[:, sl] = end_r
        cs_i[:, sl] = end_i

    c_r = car_r[...]
    c_i = car_i[...]
    at_r = atr_ref[...]
    at_i = ati_ref[...]
    for s in range(SUBLANES):
        l_r = cs_r[s:s + 1, :]
        l_i = cs_i[s:s + 1, :]
        cs_r[s:s + 1, :] = c_r
        cs_i[s:s + 1, :] = c_i
        c_r, c_i = l_r + at_r * c_r - at_i * c_i, l_i + at_r * c_i + at_i * c_r
    car_r[...] = c_r
    car_i[...] = c_i

    for k in range(nk):
        sl = pl.ds(k * lw, lw)
        ar = are_ref[:, sl]
        ai = aim_ref[:, sl]

        def scan_step(j, carry, sl=sl, ar=ar, ai=ai):
            cr, cim = carry
            rows = pl.ds(pl.multiple_of(j * SUBLANES, SUBLANES), SUBLANES)
            nr = ar * cr - ai * cim + hr[rows, sl]
            ni = ar * cim + ai * cr + hi[rows, sl]
            hr[rows, sl] = nr
            hi[rows, sl] = ni
            return nr, ni

        lax.fori_loop(0, ts, scan_step, (cs_r[:, sl], cs_i[:, sl]), unroll=True)
        hcat = jnp.concatenate([hr[:, sl], hi[:, sl]], axis=1).astype(BF16)
        yp[:, k * cw:(k + 1) * cw] = jnp.dot(hcat, cblk_ref[k], preferred_element_type=F32)

    y1, y2, y3 = _split3(yp[...])
    pmt = pmt_ref[...]
    y = (jnp.dot(pmt, y1, preferred_element_type=F32)
         + jnp.dot(pmt, y2, preferred_element_type=F32)
         + jnp.dot(pmt, y3, preferred_element_type=F32))
    y = y + d_ref[...] * u
    y = _gelu(y)
    gate = jax.nn.sigmoid(jnp.dot(y.astype(BF16), wglu_ref[...], preferred_element_type=F32))
    o_ref[...] = (y * gate).astype(o_ref.dtype)


def _s5_permutation(t_rows, ts):
    r = np.arange(t_rows)
    src = (r % SUBLANES) * ts + r // SUBLANES
    pm = np.zeros((t_rows, t_rows), np.float32)
    pm[r, src] = 1.0
    return jnp.asarray(pm, BF16), jnp.asarray(pm.T, BF16)


def _s5_branch(u, seq, a_re, a_im, at_re, at_im, bblk, cblk, d_skip, w_glu):
    n, w = u.shape
    ns = a_re.shape[1]
    t_rows = S5_CHUNK
    ts = t_rows // SUBLANES
    nk = ns // S5_LANE_BLOCK
    nchunk = seq // t_rows
    pm, pmt = _s5_permutation(t_rows, ts)
    full = lambda shape: pl.BlockSpec(shape, lambda b, c: (0,) * len(shape))
    rep8 = lambda row: jnp.broadcast_to(row, (SUBLANES, ns))
    return pl.pallas_call(
        functools.partial(_s5_kernel, ts=ts, nk=nk),
        grid=(n // seq, nchunk),
        in_specs=[
            pl.BlockSpec((t_rows, w), lambda b, c: (b * nchunk + c, 0)),
            full((t_rows, t_rows)), full((t_rows, t_rows)),
            full(bblk.shape),
            full((SUBLANES, ns)), full((SUBLANES, ns)), full((1, ns)), full((1, ns)),
            full(cblk.shape),
            full((1, w)), full((w, w)),
        ],
        out_specs=pl.BlockSpec((t_rows, w), lambda b, c: (b * nchunk + c, 0)),
        out_shape=jax.ShapeDtypeStruct((n, w), BF16),
        scratch_shapes=[
            pltpu.VMEM((t_rows, ns), F32), pltpu.VMEM((t_rows, ns), F32),
            pltpu.VMEM((1, ns), F32), pltpu.VMEM((1, ns), F32),
            pltpu.VMEM((SUBLANES, ns), F32), pltpu.VMEM((SUBLANES, ns), F32),
            pltpu.VMEM((t_rows, w), F32),
        ],
        compiler_params=_cparams("arbitrary", "arbitrary"),
        name="s5_branch",
    )(u, pm, pmt, bblk, rep8(a_re), rep8(a_im), at_re, at_im, cblk, d_skip.reshape(1, w), w_glu)


def _attn_kernel(q_ref, k_ref, v_ref, o_ref, o_s, lse_s, *, seq, patterns, scale):
    blk = ATTN_BLOCK
    qi = lax.broadcasted_iota(jnp.int32, (blk, 2 * blk), 0)
    kj = lax.broadcasted_iota(jnp.int32, (blk, 2 * blk), 1)
    dist = blk + qi - kj
    nt = (((1,), (1,)), ((), ()))

    for pi, (window, dil) in enumerate(patterns):
        n_back = window // dil
        nblk = seq // dil // blk
        grp = min(ATTN_GROUP, nblk)
        ngrp = nblk // grp
        band = (dist >= 0) & (dist <= n_back)

        nres = max(1, ATTN_GROUP // grp)

        def group_step(idx, carry, pi=pi, dil=dil, grp=grp, ngrp=ngrp, nres=nres, band=band):
            r0 = (idx // ngrp) * nres
            n0 = (idx % ngrp) * grp

            def rows_of(r, nb):
                start = r + nb * (blk * dil)
                return pl.ds(start, blk) if dil == 1 else pl.ds(start, blk, stride=dil)

            chains = [[rows_of(r0 + ri, jnp.maximum(n0 - 1, 0))]
                      + [rows_of(r0 + ri, n0 + gi) for gi in range(grp)] for ri in range(nres)]
            units = [(ri, gi) for ri in range(nres) for gi in range(grp)]
            ks = [[k_ref[rw, :].astype(BF16) for rw in rows] for rows in chains]
            scores = []
            for ri, gi in units:
                q = q_ref[chains[ri][gi + 1], :].astype(BF16)
                scores.append(jnp.concatenate(
                    [lax.dot_general(q, ks[ri][gi], nt, preferred_element_type=F32),
                     lax.dot_general(q, ks[ri][gi + 1], nt, preferred_element_type=F32)], axis=1))
            probs, inv_l = [], []
            for (ri, gi), sc in zip(units, scores):
                rows = chains[ri][gi + 1]
                mask = band & ((kj >= blk) | (n0 > 0)) if gi == 0 else band
                s = jnp.where(mask, sc * scale, -jnp.inf)
                m = jnp.max(s, axis=1, keepdims=True)
                p = jnp.exp(s - m)
                l = jnp.sum(p, axis=1, keepdims=True)
                lse_s[pi, rows, :] = jnp.broadcast_to(m + jnp.log(l), (blk, HEAD_DIM))
                inv_l.append(1.0 / l)
                probs.append(p.astype(BF16))
            vs = [[v_ref[rw, :].astype(BF16) for rw in rows] for rows in chains]
            for (ri, gi), pb, il in zip(units, probs, inv_l):
                o_s[pi, chains[ri][gi + 1], :] = il * (
                    jnp.dot(pb[:, :blk], vs[ri][gi], preferred_element_type=F32)
                    + jnp.dot(pb[:, blk:], vs[ri][gi + 1], preferred_element_type=F32))
            return carry

        lax.fori_loop(0, (dil // nres) * ngrp, group_step, 0, unroll=2)

    npat = len(patterns)

    def merge_step(t, carry):
        rows = pl.ds(pl.multiple_of(t * blk, blk), blk)
        lses = [lse_s[pi, rows, :] for pi in range(npat)]
        mx = functools.reduce(jnp.maximum, lses)
        num = jnp.zeros((blk, HEAD_DIM), F32)
        den = jnp.zeros((blk, HEAD_DIM), F32)
        for pi in range(npat):
            w = jnp.exp(lses[pi] - mx)
            num = num + w * o_s[pi, rows, :]
            den = den + w
        o_ref[rows, :] = (num / den).astype(o_ref.dtype)
        return carry

    lax.fori_loop(0, seq // blk, merge_step, 0)


def _dilated_attention(qkv, seq):
    n = qkv.shape[0]
    nh = qkv.shape[1] // (3 * HEAD_DIM)
    for window, dil in DILATED_PATTERNS:
        assert window // dil <= ATTN_BLOCK and seq % (dil * ATTN_BLOCK) == 0
    blk = (seq, HEAD_DIM)
    return pl.pallas_call(
        functools.partial(_attn_kernel, seq=seq, patterns=DILATED_PATTERNS,
                          scale=np.float32(HEAD_DIM ** -0.5)),
        grid=(n // seq, nh),
        in_specs=[pl.BlockSpec(blk, lambda b, h: (b, h)),
                  pl.BlockSpec(blk, lambda b, h: (b, nh + h)),
                  pl.BlockSpec(blk, lambda b, h: (b, 2 * nh + h))],
        out_specs=pl.BlockSpec(blk, lambda b, h: (b, h)),
        out_shape=jax.ShapeDtypeStruct((n, nh * HEAD_DIM), BF16),
        scratch_shapes=[pltpu.VMEM((len(DILATED_PATTERNS),) + blk, F32)] * 2,
        compiler_params=_cparams("arbitrary", "arbitrary"),
        name="dilated_attn",
    )(qkv, qkv, qkv)


def _top16(s, kid, exact_ties):
    nkeys = s.shape[0]
    vals = []
    if exact_ties:
        rank = jnp.full(s.shape, float(nkeys), F32)
        for it in range(PEER_TOPK):
            m = jnp.max(s, axis=0, keepdims=True)
            idx = jnp.min(jnp.where(s == m, kid, float(nkeys)), axis=0, keepdims=True)
            sel = kid == idx
            rank = jnp.where(sel, float(it), rank)
            s = jnp.where(sel, -jnp.inf, s)
            vals.append(m)
        return rank, jnp.concatenate(vals, axis=0)
    floor = np.float32(-(2.0 ** 125))
    for it in range(PEER_TOPK):
        m = jnp.max(s, axis=0, keepdims=True)
        s = jnp.where(s == m, np.float32(-(2.0 ** 126 + it * 2.0 ** 103)), s)
        vals.append(m)
    step = (-s - np.float32(2.0 ** 126)) * np.float32(2.0 ** -103)
    rank = jnp.where(s <= floor, jnp.maximum(step, 0.0), float(nkeys))
    return rank, jnp.concatenate(vals, axis=0)


def _router_kernel(q_ref, sk_ref, p1_ref, n_ref, r2_ref, p2_ref, *, nsub):
    nk = PEER_NKEYS
    kt = PEER_TOPK
    half = kt // 2
    kid = lax.broadcasted_iota(jnp.int32, (nk, LANES), 0).astype(F32)
    ia = lax.broadcasted_iota(jnp.int32, (kt, LANES), 0).astype(F32)
    ih = ia[0:half]
    cid = jnp.concatenate([ia * kt] + [ih * kt + float(b) for b in range(1, half)] + [ih + float(half)],
                          axis=0)
    ncand = kt * kt
    tail0 = kt + (half - 1) * half

    def route(lanes, exact_ties):
        q1 = q_ref[0:nk, lanes].astype(BF16)
        q2 = q_ref[nk:2 * nk, lanes].astype(BF16)
        s1 = jnp.dot(sk_ref[0], q1, preferred_element_type=F32)
        s2 = jnp.dot(sk_ref[1], q2, preferred_element_type=F32)
        rank1, v1 = _top16(s1, kid, exact_ties)
        rank2, v2 = _top16(s2, kid, exact_ties)
        work = jnp.concatenate([v1 + v2[0:1]] + [v1[0:half] + v2[b:b + 1] for b in range(1, half)]
                               + [v1[0:1] + v2[half:kt]], axis=0)
        m0 = v1[0:1] + v2[0:1]
        z = jnp.zeros((1, LANES), F32)
        if exact_ties:
            n_a = jnp.zeros((kt, LANES), F32)
            for _ in range(kt):
                m = jnp.max(work, axis=0, keepdims=True)
                idx = jnp.min(jnp.where(work == m, cid, float(ncand)), axis=0, keepdims=True)
                work = jnp.where(cid == idx, -jnp.inf, work)
                a_sel = jnp.floor(idx * (1.0 / kt))
                n_a = n_a + jnp.where(ia == a_sel, 1.0, 0.0)
                z = z + jnp.exp(m - m0)
        else:
            for _ in range(kt):
                m = jnp.max(work, axis=0, keepdims=True)
                work = jnp.where(work == m, -jnp.inf, work)
                z = z + jnp.exp(m - m0)
            mark = jnp.where(work == -jnp.inf, 1.0, 0.0)
            low = mark[0:half]
            for b in range(1, half):
                low = low + mark[kt + (b - 1) * half:kt + b * half]
            tail = jnp.sum(mark[tail0:], axis=0, keepdims=True)
            n_a = jnp.concatenate([low + jnp.where(ih == 0.0, tail, 0.0), mark[half:kt]], axis=0)
        nkey = jnp.zeros((nk, LANES), F32)
        for a in range(kt):
            nkey = jnp.where(rank1 == float(a), n_a[a:a + 1], nkey)
        p1_ref[0, :, lanes] = jnp.exp(s1 - v1[0:1])
        n_ref[0, :, lanes] = nkey
        r2_ref[0, :, lanes] = rank2.astype(r2_ref.dtype)
        p2_ref[0, :, lanes] = (jnp.exp(s2 - v2[0:1]) / (2.0 * z)).astype(p2_ref.dtype)
        if exact_ties:
            return None
        ranked = lambda r: jnp.sum(jnp.where(r < float(nk), 1.0, 0.0), axis=0, keepdims=True)
        ok = ((ranked(rank1) == float(kt)) & (ranked(rank2) == float(kt))
              & (jnp.sum(n_a, axis=0, keepdims=True) == float(kt)))
        return jnp.min(jnp.where(ok, 1.0, 0.0))

    def sub(st, carry):
        lanes = [pl.ds(pl.multiple_of((st * ROUTER_GROUPS + i) * LANES, LANES), LANES)
                 for i in range(ROUTER_GROUPS)]
        all_distinct = functools.reduce(jnp.minimum, [route(l, False) for l in lanes])

        @pl.when(all_distinct < 0.5)
        def _():
            for l in lanes:
                route(l, True)

        return carry

    lax.fori_loop(0, nsub // ROUTER_GROUPS, sub, 0)


def _router(qt, sk, tt=512):
    dq2, n = qt.shape
    nh = PEER_HEADS
    nk = PEER_NKEYS
    out = jax.ShapeDtypeStruct((nh, nk, n), F32)
    out_b = jax.ShapeDtypeStruct((nh, nk, n), BF16)
    ospec = pl.BlockSpec((1, nk, tt), lambda i, h: (h, 0, i))
    return pl.pallas_call(
        functools.partial(_router_kernel, nsub=tt // LANES),
        grid=(n // tt, nh),
        in_specs=[pl.BlockSpec((2 * nk, tt), lambda i, h: (h, i)),
                  pl.BlockSpec((2, nk, nk), lambda i, h: (0, 0, 0))],
        out_specs=(ospec, ospec, ospec, ospec),
        out_shape=(out, out, out_b, out_b),
        compiler_params=_cparams("arbitrary", "arbitrary"),
        name="peer_router",
    )(qt, sk)


def _peer_kernel(ht_ref, u_ref, vt_ref, p1_ref, n_ref, r2_ref, p2_ref, x_ref, g_ref, *rest,
                 ne1, nh, final_norm):
    fgain_ref = rest[0] if final_norm else None
    o_ref, acc, gbuf = rest[-3:]
    j = pl.program_id(1)
    nk = PEER_NKEYS

    @pl.when(j == 0)
    def _():
        acc[...] = jnp.zeros_like(acc)

    ht = ht_ref[...]
    for e in range(ne1):
        rows = slice(e * nk, (e + 1) * nk)
        act = _gelu_x2(jnp.dot(u_ref[rows, :], ht, preferred_element_type=F32))
        w = jnp.zeros(act.shape, BF16)
        for h in range(nh):
            p1 = jnp.broadcast_to(p1_ref[h, e:e + 1, :], act.shape).astype(BF16)
            cnt = jnp.broadcast_to(n_ref[h, e:e + 1, :], act.shape).astype(BF16)
            w = w + p1 * jnp.where(r2_ref[h] < cnt, p2_ref[h], jnp.zeros((), BF16))
        gbuf[rows, :] = act.astype(BF16) * w
    acc[...] += jnp.dot(vt_ref[...], gbuf[...], preferred_element_type=F32)

    @pl.when(j == pl.num_programs(1) - 1)
    def _():
        xn = x_ref[...] + g_ref[0] * acc[...].T
        o_ref[...] = _rmsnorm(xn, fgain_ref[...]) if final_norm else xn


def _peer(ht, u_tab, vt_tab, layer, p1, cnt, r2, p2, x2d, g, seq, final_gain=None, tt=512, eb=1024):
    d, n = ht.shape
    ne = u_tab.shape[1]
    nh, nk, _ = p1.shape
    ne1 = eb // nk
    per_batch = seq // tt
    tok = pl.BlockSpec((nh, nk, tt), lambda i, j: (0, 0, i))
    e1b = pl.BlockSpec((nh, ne1, tt), lambda i, j: (0, j, i))
    final_norm = final_gain is not None
    extra_specs = [pl.BlockSpec((1, d), lambda i, j: (0, 0))] if final_norm else []
    extra_args = [final_gain.reshape(1, d)] if final_norm else []
    return pl.pallas_call(
        functools.partial(_peer_kernel, ne1=ne1, nh=nh, final_norm=final_norm),
        grid=(n // tt, ne // eb),
        in_specs=[
            pl.BlockSpec((d, tt), lambda i, j: (0, i)),
            pl.BlockSpec((None, eb, d), lambda i, j: (layer, j, 0)),
            pl.BlockSpec((None, d, eb), lambda i, j: (layer, 0, j)),
            e1b, e1b, tok, tok,
            pl.BlockSpec((tt, d), lambda i, j: (i, 0)),
            pl.BlockSpec((1, 1, d), lambda i, j: (i // per_batch, 0, 0)),
        ] + extra_specs,
        out_specs=pl.BlockSpec((tt, d), lambda i, j: (i, 0)),
        out_shape=jax.ShapeDtypeStruct((n, d), F32),
        scratch_shapes=[pltpu.VMEM((d, tt), F32), pltpu.VMEM((eb, tt), BF16)],
        compiler_params=_cparams("arbitrary", "arbitrary", vmem=PEER_VMEM_LIMIT),
        name="peer_experts",
    )(ht, u_tab, vt_tab, p1, cnt, r2, p2, x2d, g, *extra_args)


def kernel(x, c, w_ada, b_ada, norm_mix, norm_ffn, w_in, ssm_lambda_re, ssm_lambda_im, ssm_log_dt, ssm_b_re, ssm_b_im, ssm_c_re, ssm_c_im, ssm_d, w_glu, w_br_ssm, w_br_attn, w_out, peer_wq, peer_subkeys, peer_u, peer_v, norm_final):
    bsz, seq, d = x.shape
    depth = w_ada.shape[0]
    n = bsz * seq
    ssm_w = ssm_d.shape[1]
    attn_w = w_br_attn.shape[1]
    assert bsz <= SUBLANES and seq % S5_CHUNK == 0

    c_pad = jnp.pad(c, ((0, SUBLANES - bsz), (0, 0)))
    mod_all = _modulation(c_pad, w_ada, b_ada)
    x2d = x.reshape(n, d)
    ts = S5_CHUNK // SUBLANES
    w_in_all = w_in.astype(BF16)
    u_all = peer_u.astype(BF16)
    vt_all = jnp.swapaxes(peer_v, 1, 2).astype(BF16)

    for l in range(depth):
        mods = [mod_all[l, :bsz, i * d:(i + 1) * d].reshape(bsz, 1, d) for i in range(N_MOD)]
        sh_m, sc_m, g_m, sh_f, sc_f, g_f = mods

        h = _norm_mod(x2d, norm_mix[l], sc_m, sh_m, seq)
        u, qkv, gates = _in_proj(h, w_in_all, l, ssm_w, 3 * attn_w)

        a_re, a_im, bbar_re, bbar_im, at_re, at_im = _s5_prep(
            ssm_lambda_re[l], ssm_lambda_im[l], ssm_log_dt[l], ssm_b_re[l], ssm_b_im[l], ts)
        bblk, cblk = _s5_block_weights(bbar_re, bbar_im, ssm_c_re[l], ssm_c_im[l])
        y_s = _s5_branch(u, seq, a_re, a_im, at_re, at_im, bblk, cblk, ssm_d[l],
                         w_glu[l].astype(BF16))
        y_a = _dilated_attention(qkv, seq)
        merged = _merge(y_s, y_a, w_br_ssm[l].astype(BF16), w_br_attn[l].astype(BF16), gates)
        x2d, ht = _outproj_norm(merged, w_out[l].astype(BF16), x2d, g_m, norm_ffn[l], sc_f, sh_f, seq)
        qt = _matmul(peer_wq[l].T.astype(BF16), ht, out_dtype=F32, tm=1024, tn=1024,
                     name="peer_query")
        p1, cnt, r2, p2 = _router(qt, peer_subkeys[l].astype(BF16))
        x2d = _peer(ht, u_all, vt_all, l, p1, cnt, r2, p2, x2d, g_f, seq,
                    final_gain=norm_final if l == depth - 1 else None)

    return x2d.reshape(bsz, seq, d)
```

```python
import functools
import math

import jax
import jax.numpy as jnp
import numpy as np
from jax import lax
from jax.experimental import pallas as pl
from jax.experimental.pallas import tpu as pltpu

F32 = jnp.float32
BF16 = jnp.bfloat16

EPS = 1e-6
N_MOD = 6
SSM_GROUP = 16
SSM_STATE = 64
HEAD_DIM = 128
ATTN_BLOCK = 128
ATTN_GROUP = 8
DILATED_PATTERNS = ((128, 1), (512, 4), (2048, 16))
PEER_HEADS = 8
PEER_NKEYS = 128
PEER_TOPK = 16

LANES = 128
SUBLANES = 8
S5_CHUNK = 256
S5_LANE_BLOCK = 512
S5_CH_BLOCK = 128
VMEM_LIMIT = 56 * 1024 * 1024
PEER_VMEM_LIMIT = 60 * 1024 * 1024


def _cparams(*sem, vmem=VMEM_LIMIT):
    return pltpu.CompilerParams(dimension_semantics=sem, vmem_limit_bytes=vmem)


def _gelu_x2(x):
    return x * (1.0 + lax.erf(x * np.float32(math.sqrt(0.5))))


def _gelu(x):
    return 0.5 * _gelu_x2(x)


def _split3(x):
    hi = x.astype(BF16)
    r1 = x - hi.astype(F32)
    mid = r1.astype(BF16)
    lo = (r1 - mid.astype(F32)).astype(BF16)
    return hi, mid, lo


def _mod_kernel(c_ref, w_ref, b_ref, o_ref):
    c = c_ref[...]
    ca = c * jax.nn.sigmoid(c)
    w = w_ref[0]
    c_hi = ca.astype(BF16)
    c_lo = (ca - c_hi.astype(F32)).astype(BF16)
    w_hi = w.astype(BF16)
    w_lo = (w - w_hi.astype(F32)).astype(BF16)
    acc = jnp.dot(c_hi, w_hi, preferred_element_type=F32)
    acc += jnp.dot(c_lo, w_hi, preferred_element_type=F32)
    acc += jnp.dot(c_hi, w_lo, preferred_element_type=F32)
    o_ref[0] = acc + b_ref[0]


def _modulation(c_pad, w_ada, b_ada, tn=1024):
    depth, d, cols = w_ada.shape
    return pl.pallas_call(
        _mod_kernel,
        grid=(depth, cols // tn),
        in_specs=[
            pl.BlockSpec((SUBLANES, d), lambda l, j: (0, 0)),
            pl.BlockSpec((1, d, tn), lambda l, j: (l, 0, j)),
            pl.BlockSpec((1, 1, tn), lambda l, j: (l, 0, j)),
        ],
        out_specs=pl.BlockSpec((1, SUBLANES, tn), lambda l, j: (l, 0, j)),
        out_shape=jax.ShapeDtypeStruct((depth, SUBLANES, cols), F32),
        compiler_params=_cparams("arbitrary", "arbitrary"),
        name="adaln_mod",
    )(c_pad, w_ada, b_ada.reshape(depth, 1, cols))


def _rmsnorm(x, gain):
    ms = jnp.mean(x * x, axis=-1, keepdims=True)
    return x * lax.rsqrt(ms + EPS) * gain


def _norm_mod_kernel(x_ref, gain_ref, sc_ref, sh_ref, o_ref):
    h = _rmsnorm(x_ref[...], gain_ref[...]) * (1.0 + sc_ref[0]) + sh_ref[0]
    o_ref[...] = h.astype(o_ref.dtype)


def _norm_mod(x2d, gain, sc, sh, seq, tm=512):
    n, d = x2d.shape
    per_batch = seq // tm
    mod = pl.BlockSpec((1, 1, d), lambda i: (i // per_batch, 0, 0))
    return pl.pallas_call(
        _norm_mod_kernel,
        grid=(n // tm,),
        in_specs=[pl.BlockSpec((tm, d), lambda i: (i, 0)),
                  pl.BlockSpec((1, d), lambda i: (0, 0)),
                  mod, mod],
        out_specs=pl.BlockSpec((tm, d), lambda i: (i, 0)),
        out_shape=jax.ShapeDtypeStruct((n, d), BF16),
        compiler_params=_cparams("arbitrary"),
        name="norm_mod",
    )(x2d, gain.reshape(1, d), sc, sh)


def _in_proj_kernel(h_ref, w_ref, u_ref, qkv_ref, gates_ref, *, nu, nqkv):
    j = pl.program_id(1)

    def proj():
        return jnp.dot(h_ref[...], w_ref[...], preferred_element_type=F32)

    @pl.when(j < nu)
    def _():
        u_ref[...] = proj()

    @pl.when((j >= nu) & (j < nu + nqkv))
    def _():
        qkv_ref[...] = proj()

    @pl.when(j >= nu + nqkv)
    def _():
        gates_ref[...] = jax.nn.sigmoid(proj()).astype(gates_ref.dtype)


def _in_proj(h, w_all, layer, u_cols, qkv_cols, tm=1024, tn=1024):
    n, d = h.shape
    cols = w_all.shape[2]
    nu, nqkv = u_cols // tn, qkv_cols // tn
    ng = cols // tn - nu - nqkv
    return pl.pallas_call(
        functools.partial(_in_proj_kernel, nu=nu, nqkv=nqkv),
        grid=(n // tm, cols // tn),
        in_specs=[
            pl.BlockSpec((tm, d), lambda i, j: (i, 0)),
            pl.BlockSpec((None, d, tn), lambda i, j: (layer, 0, j)),
        ],
        out_specs=(
            pl.BlockSpec((tm, tn), lambda i, j: (i, jnp.minimum(j, nu - 1))),
            pl.BlockSpec((tm, tn), lambda i, j: (i, jnp.clip(j - nu, 0, nqkv - 1))),
            pl.BlockSpec((tm, tn), lambda i, j: (i, jnp.clip(j - nu - nqkv, 0, ng - 1))),
        ),
        out_shape=(jax.ShapeDtypeStruct((n, u_cols), F32),
                   jax.ShapeDtypeStruct((n, qkv_cols), F32),
                   jax.ShapeDtypeStruct((n, ng * tn), BF16)),
        compiler_params=_cparams("arbitrary", "arbitrary"),
        name="in_proj",
    )(h, w_all)


def _merge_kernel(ys_ref, ya_ref, ws_ref, wa_ref, gs_ref, ga_ref, o_ref):
    ps = jnp.dot(ys_ref[...], ws_ref[...], preferred_element_type=F32)
    pa = jnp.dot(ya_ref[...], wa_ref[...], preferred_element_type=F32)
    o_ref[...] = (gs_ref[...].astype(F32) * ps
                  + ga_ref[...].astype(F32) * pa).astype(o_ref.dtype)


def _merge(ys, ya, ws, wa, gates, tm=1024, tn=1024):
    n, ks = ys.shape
    ka = ya.shape[1]
    d = ws.shape[1]
    goff = d // tn
    return pl.pallas_call(
        _merge_kernel,
        grid=(n // tm, d // tn),
        in_specs=[
            pl.BlockSpec((tm, ks), lambda i, j: (i, 0)),
            pl.BlockSpec((tm, ka), lambda i, j: (i, 0)),
            pl.BlockSpec((ks, tn), lambda i, j: (0, j)),
            pl.BlockSpec((ka, tn), lambda i, j: (0, j)),
            pl.BlockSpec((tm, tn), lambda i, j: (i, j)),
            pl.BlockSpec((tm, tn), lambda i, j: (i, j + goff)),
        ],
        out_specs=pl.BlockSpec((tm, tn), lambda i, j: (i, j)),
        out_shape=jax.ShapeDtypeStruct((n, d), BF16),
        compiler_params=_cparams("arbitrary", "arbitrary"),
        name="branch_merge",
    )(ys, ya, ws, wa, gates, gates)


def _outproj_norm_kernel(a_ref, w_ref, x_ref, g_ref, gain_ref, sc_ref, sh_ref, xo_ref, ht_ref):
    acc = jnp.dot(a_ref[...], w_ref[...], preferred_element_type=F32)
    xn = x_ref[...] + g_ref[0] * acc
    xo_ref[...] = xn
    h = _rmsnorm(xn, gain_ref[...]) * (1.0 + sc_ref[0]) + sh_ref[0]
    ht_ref[...] = h.T.astype(ht_ref.dtype)


def _outproj_norm(a, w, x2d, g, gain, sc, sh, seq, tm=512):
    n, k = a.shape
    d = w.shape[1]
    per_batch = seq // tm
    mod = pl.BlockSpec((1, 1, d), lambda i: (i // per_batch, 0, 0))
    return pl.pallas_call(
        _outproj_norm_kernel,
        grid=(n // tm,),
        in_specs=[
            pl.BlockSpec((tm, k), lambda i: (i, 0)),
            pl.BlockSpec((k, d), lambda i: (0, 0)),
            pl.BlockSpec((tm, d), lambda i: (i, 0)),
            mod,
            pl.BlockSpec((1, d), lambda i: (0, 0)),
            mod, mod,
        ],
        out_specs=(pl.BlockSpec((tm, d), lambda i: (i, 0)),
                   pl.BlockSpec((d, tm), lambda i: (0, i))),
        out_shape=(jax.ShapeDtypeStruct((n, d), F32), jax.ShapeDtypeStruct((d, n), BF16)),
        compiler_params=_cparams("arbitrary"),
        name="out_proj_norm",
    )(a, w, x2d, g, gain.reshape(1, d), sc, sh)


def _s5_prep_kernel(lr_ref, li_ref, ldt_ref, br_ref, bi_ref,
                    are_ref, aim_ref, bbr_ref, bbi_ref, atr_ref, ati_ref, *, ts):
    lr, li = lr_ref[...], li_ref[...]
    dt = jnp.exp(ldt_ref[...])
    mag = jnp.exp(lr * dt)
    a_re = mag * jnp.cos(li * dt)
    a_im = mag * jnp.sin(li * dt)
    inv = 1.0 / (lr * lr + li * li)
    coef_re = ((a_re - 1.0) * lr + a_im * li) * inv
    coef_im = (a_im * lr - (a_re - 1.0) * li) * inv
    br, bi = br_ref[...], bi_ref[...]
    bbr_ref[...] = coef_re * br - coef_im * bi
    bbi_ref[...] = coef_re * bi + coef_im * br
    are_ref[...] = a_re
    aim_ref[...] = a_im
    pr, pi = a_re, a_im
    for _ in range(ts - 1):
        pr, pi = pr * a_re - pi * a_im, pr * a_im + pi * a_re
    atr_ref[...] = pr
    ati_ref[...] = pi


def _s5_prep(lam_re, lam_im, log_dt, b_re, b_im, ts):
    g, p, c = b_re.shape
    ns = g * p
    row = lambda t: t.reshape(1, ns)
    ldt = jnp.broadcast_to(log_dt[:, None], (g, p))
    bt = lambda t: jnp.transpose(t, (2, 0, 1)).reshape(c, ns)
    shapes = (
        jax.ShapeDtypeStruct((1, ns), F32), jax.ShapeDtypeStruct((1, ns), F32),
        jax.ShapeDtypeStruct((c, ns), F32), jax.ShapeDtypeStruct((c, ns), F32),
        jax.ShapeDtypeStruct((1, ns), F32), jax.ShapeDtypeStruct((1, ns), F32),
    )
    return pl.pallas_call(
        functools.partial(_s5_prep_kernel, ts=ts),
        out_shape=shapes,
        name="s5_prep",
    )(row(lam_re), row(lam_im), row(ldt), bt(b_re), bt(b_im))


def _s5_block_weights(bbar_re, bbar_im, c_re, c_im):
    c, ns = bbar_re.shape
    gpb = S5_CH_BLOCK // c
    nk = ns // (gpb * SSM_STATE)
    eye = jnp.eye(gpb, dtype=F32)

    def bpart(t):
        t = t.reshape(c, nk, gpb, SSM_STATE)
        t = jnp.einsum("ckjp,ij->kicjp", t, eye)
        return t.reshape(nk, gpb * c, gpb * SSM_STATE)

    def cpart(t):
        t = t.reshape(nk, gpb, c, SSM_STATE)
        t = jnp.einsum("kicp,ij->kjpic", t, eye)
        return t.reshape(nk, gpb * SSM_STATE, gpb * c)

    bblk = jnp.concatenate([bpart(bbar_re), bpart(bbar_im)], axis=2).astype(BF16)
    cblk = jnp.concatenate([cpart(c_re), cpart(-c_im)], axis=1).astype(BF16)
    return bblk, cblk


def _s5_kernel(u_ref, pm_ref, pmt_ref, bblk_ref, are_ref, aim_ref, atr_ref, ati_ref,
               cblk_ref, d_ref, wglu_ref, o_ref,
               hr, hi, car_r, car_i, cs_r, cs_i, yp, *, ts, nk):
    lw = S5_LANE_BLOCK
    cw = S5_CH_BLOCK

    @pl.when(pl.program_id(1) == 0)
    def _():
        car_r[...] = jnp.zeros_like(car_r)
        car_i[...] = jnp.zeros_like(car_i)

    u = u_ref[...]
    up = jnp.dot(pm_ref[...], u.astype(BF16), preferred_element_type=F32).astype(BF16)
    for k in range(nk):
        bu = jnp.dot(up[:, k * cw:(k + 1) * cw], bblk_ref[k], preferred_element_type=F32)
        hr[:, k * lw:(k + 1) * lw] = bu[:, :lw]
        hi[:, k * lw:(k + 1) * lw] = bu[:, lw:]

    for k in range(nk):
        sl = pl.ds(k * lw, lw)
        ar = are_ref[:, sl]
        ai = aim_ref[:, sl]

        def end_step(j, carry, sl=sl, ar=ar, ai=ai):
            cr, cim = carry
            rows = pl.ds(pl.multiple_of(j * SUBLANES, SUBLANES), SUBLANES)
            return ar * cr - ai * cim + hr[rows, sl], ar * cim + ai * cr + hi[rows, sl]

        zero = jnp.zeros((SUBLANES, lw), F32)
        end_r, end_i = lax.fori_loop(0, ts, end_step, (zero, zero), unroll=True)
        cs_r[:, sl] = end_r
        cs_i[:, sl] = end_i

    c_r = car_r[...]
    c_i = car_i[...]
    at_r = atr_ref[...]
    at_i = ati_ref[...]
    for s in range(SUBLANES):
        l_r = cs_r[s:s + 1, :]
        l_i = cs_i[s:s + 1, :]
        cs_r[s:s + 1, :] = c_r
        cs_i[s:s + 1, :] = c_i
        c_r, c_i = l_r + at_r * c_r - at_i * c_i, l_i + at_r * c_i + at_i * c_r
    car_r[...] = c_r
    car_i[...] = c_i

    for k in range(nk):
        sl = pl.ds(k * lw, lw)
        ar = are_ref[:, sl]
        ai = aim_ref[:, sl]

        def scan_step(j, carry, sl=sl, ar=ar, ai=ai):
            cr, cim = carry
            rows = pl.ds(pl.multiple_of(j * SUBLANES, SUBLANES), SUBLANES)
            nr = ar * cr - ai * cim + hr[rows, sl]
            ni = ar * cim + ai * cr + hi[rows, sl]
            hr[rows, sl] = nr
            hi[rows, sl] = ni
            return nr, ni

        lax.fori_loop(0, ts, scan_step, (cs_r[:, sl], cs_i[:, sl]), unroll=True)
        hcat = jnp.concatenate([hr[:, sl], hi[:, sl]], axis=1).astype(BF16)
        yp[:, k * cw:(k + 1) * cw] = jnp.dot(hcat, cblk_ref[k], preferred_element_type=F32)

    y1, y2, y3 = _split3(yp[...])
    pmt = pmt_ref[...]
    y = (jnp.dot(pmt, y1, preferred_element_type=F32)
         + jnp.dot(pmt, y2, preferred_element_type=F32)
         + jnp.dot(pmt, y3, preferred_element_type=F32))
    y = y + d_ref[...] * u
    y = _gelu(y)
    gate = jax.nn.sigmoid(jnp.dot(y.astype(BF16), wglu_ref[...], preferred_element_type=F32))
    o_ref[...] = (y * gate).astype(o_ref.dtype)


def _s5_permutation(t_rows, ts):
    r = np.arange(t_rows)
    src = (r % SUBLANES) * ts + r // SUBLANES
    pm = np.zeros((t_rows, t_rows), np.float32)
    pm[r, src] = 1.0
    return jnp.asarray(pm, BF16), jnp.asarray(pm.T, BF16)


def _s5_branch(u, seq, a_re, a_im, at_re, at_im, bblk, cblk, d_skip, w_glu):
    n, w = u.shape
    ns = a_re.shape[1]
    t_rows = S5_CHUNK
    ts = t_rows // SUBLANES
    nk = ns // S5_LANE_BLOCK
    nchunk = seq // t_rows
    pm, pmt = _s5_permutation(t_rows, ts)
    full = lambda shape: pl.BlockSpec(shape, lambda b, c: (0,) * len(shape))
    rep8 = lambda row: jnp.broadcast_to(row, (SUBLANES, ns))
    return pl.pallas_call(
        functools.partial(_s5_kernel, ts=ts, nk=nk),
        grid=(n // seq, nchunk),
        in_specs=[
            pl.BlockSpec((t_rows, w), lambda b, c: (b * nchunk + c, 0)),
            full((t_rows, t_rows)), full((t_rows, t_rows)),
            full(bblk.shape),
            full((SUBLANES, ns)), full((SUBLANES, ns)), full((1, ns)), full((1, ns)),
            full(cblk.shape),
            full((1, w)), full((w, w)),
        ],
        out_specs=pl.BlockSpec((t_rows, w), lambda b, c: (b * nchunk + c, 0)),
        out_shape=jax.ShapeDtypeStruct((n, w), BF16),
        scratch_shapes=[
            pltpu.VMEM((t_rows, ns), F32), pltpu.VMEM((t_rows, ns), F32),
            pltpu.VMEM((1, ns), F32), pltpu.VMEM((1, ns), F32),
            pltpu.VMEM((SUBLANES, ns), F32), pltpu.VMEM((SUBLANES, ns), F32),
            pltpu.VMEM((t_rows, w), F32),
        ],
        compiler_params=_cparams("arbitrary", "arbitrary"),
        name="s5_branch",
    )(u, pm, pmt, bblk, rep8(a_re), rep8(a_im), at_re, at_im, cblk, d_skip.reshape(1, w), w_glu)


def _attn_kernel(q_ref, k_ref, v_ref, o_ref, o_s, lse_s, *, seq, patterns, scale):
    blk = ATTN_BLOCK
    qi = lax.broadcasted_iota(jnp.int32, (blk, 2 * blk), 0)
    kj = lax.broadcasted_iota(jnp.int32, (blk, 2 * blk), 1)
    dist = blk + qi - kj
    nt = (((1,), (1,)), ((), ()))

    for pi, (window, dil) in enumerate(patterns):
        n_back = window // dil
        nblk = seq // dil // blk
        grp = min(ATTN_GROUP, nblk)
        ngrp = nblk // grp
        band = (dist >= 0) & (dist <= n_back)

        nres = max(1, ATTN_GROUP // grp)

        def group_step(idx, carry, pi=pi, dil=dil, grp=grp, ngrp=ngrp, nres=nres, band=band):
            r0 = (idx // ngrp) * nres
            n0 = (idx % ngrp) * grp

            def rows_of(r, nb):
                start = r + nb * (blk * dil)
                return pl.ds(start, blk) if dil == 1 else pl.ds(start, blk, stride=dil)

            chains = [[rows_of(r0 + ri, jnp.maximum(n0 - 1, 0))]
                      + [rows_of(r0 + ri, n0 + gi) for gi in range(grp)] for ri in range(nres)]
            units = [(ri, gi) for ri in range(nres) for gi in range(grp)]
            ks = [[k_ref[rw, :].astype(BF16) for rw in rows] for rows in chains]
            scores = []
            for ri, gi in units:
                q = q_ref[chains[ri][gi + 1], :].astype(BF16)
                scores.append(jnp.concatenate(
                    [lax.dot_general(q, ks[ri][gi], nt, preferred_element_type=F32),
                     lax.dot_general(q, ks[ri][gi + 1], nt, preferred_element_type=F32)], axis=1))
            probs, inv_l = [], []
            for (ri, gi), sc in zip(units, scores):
                rows = chains[ri][gi + 1]
                mask = band & ((kj >= blk) | (n0 > 0)) if gi == 0 else band
                s = jnp.where(mask, sc * scale, -jnp.inf)
                m = jnp.max(s, axis=1, keepdims=True)
                p = jnp.exp(s - m)
                l = jnp.sum(p, axis=1, keepdims=True)
                lse_s[pi, rows, :] = jnp.broadcast_to(m + jnp.log(l), (blk, HEAD_DIM))
                inv_l.append(1.0 / l)
                probs.append(p.astype(BF16))
            vs = [[v_ref[rw, :].astype(BF16) for rw in rows] for rows in chains]
            for (ri, gi), pb, il in zip(units, probs, inv_l):
                o_s[pi, chains[ri][gi + 1], :] = il * (
                    jnp.dot(pb[:, :blk], vs[ri][gi], preferred_element_type=F32)
                    + jnp.dot(pb[:, blk:], vs[ri][gi + 1], preferred_element_type=F32))
            return carry

        lax.fori_loop(0, (dil // nres) * ngrp, group_step, 0, unroll=2)

    npat = len(patterns)

    def merge_step(t, carry):
        rows = pl.ds(pl.multiple_of(t * blk, blk), blk)
        lses = [lse_s[pi, rows, :] for pi in range(npat)]
        mx = functools.reduce(jnp.maximum, lses)
        num = jnp.zeros((blk, HEAD_DIM), F32)
        den = jnp.zeros((blk, HEAD_DIM), F32)
        for pi in range(npat):
            w = jnp.exp(lses[pi] - mx)
            num = num + w * o_s[pi, rows, :]
            den = den + w
        o_ref[rows, :] = (num / den).astype(o_ref.dtype)
        return carry

    lax.fori_loop(0, seq // blk, merge_step, 0)


def _dilated_attention(qkv, seq):
    n = qkv.shape[0]
    nh = qkv.shape[1] // (3 * HEAD_DIM)
    for window, dil in DILATED_PATTERNS:
        assert window // dil <= ATTN_BLOCK and seq % (dil * ATTN_BLOCK) == 0
    blk = (seq, HEAD_DIM)
    return pl.pallas_call(
        functools.partial(_attn_kernel, seq=seq, patterns=DILATED_PATTERNS,
                          scale=np.float32(HEAD_DIM ** -0.5)),
        grid=(n // seq, nh),
        in_specs=[pl.BlockSpec(blk, lambda b, h: (b, h)),
                  pl.BlockSpec(blk, lambda b, h: (b, nh + h)),
                  pl.BlockSpec(blk, lambda b, h: (b, 2 * nh + h))],
        out_specs=pl.BlockSpec(blk, lambda b, h: (b, h)),
        out_shape=jax.ShapeDtypeStruct((n, nh * HEAD_DIM), BF16),
        scratch_shapes=[pltpu.VMEM((len(DILATED_PATTERNS),) + blk, F32)] * 2,
        compiler_params=_cparams("arbitrary", "arbitrary"),
        name="dilated_attn",
    )(qkv, qkv, qkv)


def _top16(s, kid, exact_ties):
    nkeys = s.shape[0]
    vals = []
    if exact_ties:
        rank = jnp.full(s.shape, float(nkeys), F32)
        for it in range(PEER_TOPK):
            m = jnp.max(s, axis=0, keepdims=True)
            idx = jnp.min(jnp.where(s == m, kid, float(nkeys)), axis=0, keepdims=True)
            sel = kid == idx
            rank = jnp.where(sel, float(it), rank)
            s = jnp.where(sel, -jnp.inf, s)
            vals.append(m)
        return rank, jnp.concatenate(vals, axis=0)
    floor = np.float32(-(2.0 ** 125))
    for it in range(PEER_TOPK):
        m = jnp.max(s, axis=0, keepdims=True)
        s = jnp.where(s == m, np.float32(-(2.0 ** 126 + it * 2.0 ** 103)), s)
        vals.append(m)
    step = (-s - np.float32(2.0 ** 126)) * np.float32(2.0 ** -103)
    rank = jnp.where(s <= floor, jnp.maximum(step, 0.0), float(nkeys))
    return rank, jnp.concatenate(vals, axis=0)


def _router_kernel(wq_ref, ht_ref, sk_ref, p1_ref, n_ref, r2_ref, p2_ref, q_s, *, ngroups):
    nk = PEER_NKEYS
    kt = PEER_TOPK
    half = kt // 2
    kid = lax.broadcasted_iota(jnp.int32, (nk, LANES), 0).astype(F32)
    ia = lax.broadcasted_iota(jnp.int32, (kt, LANES), 0).astype(F32)
    ih = ia[0:half]
    cid = jnp.concatenate([ia * kt] + [ih * kt + float(b) for b in range(1, half)] + [ih + float(half)],
                          axis=0)
    ncand = kt * kt
    tail0 = kt + (half - 1) * half

    q_s[...] = jnp.dot(wq_ref[...], ht_ref[...], preferred_element_type=F32)

    def route(lanes, exact_ties):
        q1 = q_s[0:nk, lanes].astype(BF16)
        q2 = q_s[nk:2 * nk, lanes].astype(BF16)
        s1 = jnp.dot(sk_ref[0], q1, preferred_element_type=F32)
        s2 = jnp.dot(sk_ref[1], q2, preferred_element_type=F32)
        rank1, v1 = _top16(s1, kid, exact_ties)
        rank2, v2 = _top16(s2, kid, exact_ties)
        work = jnp.concatenate([v1 + v2[0:1]] + [v1[0:half] + v2[b:b + 1] for b in range(1, half)]
                               + [v1[0:1] + v2[half:kt]], axis=0)
        m0 = v1[0:1] + v2[0:1]
        z = jnp.zeros((1, LANES), F32)
        if exact_ties:
            n_a = jnp.zeros((kt, LANES), F32)
            for _ in range(kt):
                m = jnp.max(work, axis=0, keepdims=True)
                idx = jnp.min(jnp.where(work == m, cid, float(ncand)), axis=0, keepdims=True)
                work = jnp.where(cid == idx, -jnp.inf, work)
                a_sel = jnp.floor(idx * (1.0 / kt))
                n_a = n_a + jnp.where(ia == a_sel, 1.0, 0.0)
                z = z + jnp.exp(m - m0)
        else:
            for _ in range(kt):
                m = jnp.max(work, axis=0, keepdims=True)
                work = jnp.where(work == m, -jnp.inf, work)
                z = z + jnp.exp(m - m0)
            mark = jnp.where(work == -jnp.inf, 1.0, 0.0)
            low = mark[0:half]
            for b in range(1, half):
                low = low + mark[kt + (b - 1) * half:kt + b * half]
            tail = jnp.sum(mark[tail0:], axis=0, keepdims=True)
            n_a = jnp.concatenate([low + jnp.where(ih == 0.0, tail, 0.0), mark[half:kt]], axis=0)
        nkey = jnp.zeros((nk, LANES), F32)
        for a in range(kt):
            nkey = jnp.where(rank1 == float(a), n_a[a:a + 1], nkey)
        p1_ref[0, :, lanes] = jnp.exp(s1 - v1[0:1])
        n_ref[0, :, lanes] = nkey
        r2_ref[0, :, lanes] = rank2.astype(r2_ref.dtype)
        p2_ref[0, :, lanes] = (jnp.exp(s2 - v2[0:1]) / (2.0 * z)).astype(p2_ref.dtype)
        if exact_ties:
            return None
        ranked = lambda r: jnp.sum(jnp.where(r < float(nk), 1.0, 0.0), axis=0, keepdims=True)
        ok = ((ranked(rank1) == float(kt)) & (ranked(rank2) == float(kt))
              & (jnp.sum(n_a, axis=0, keepdims=True) == float(kt)))
        return jnp.min(jnp.where(ok, 1.0, 0.0))

    groups = [slice(g * LANES, (g + 1) * LANES) for g in range(ngroups)]
    all_distinct = [route(lanes, False) for lanes in groups]
    for lanes, ok in zip(groups, all_distinct):
        @pl.when(ok < 0.5)
        def _(lanes=lanes):
            route(lanes, True)


def _router(wq_t, ht, sk, tt=512):
    d, n = ht.shape
    nh = PEER_HEADS
    nk = PEER_NKEYS
    out = jax.ShapeDtypeStruct((nh, nk, n), F32)
    out_b = jax.ShapeDtypeStruct((nh, nk, n), BF16)
    ospec = pl.BlockSpec((1, nk, tt), lambda i, h: (h, 0, i))
    return pl.pallas_call(
        functools.partial(_router_kernel, ngroups=tt // LANES),
        grid=(n // tt, nh),
        in_specs=[pl.BlockSpec((2 * nk, d), lambda i, h: (h, 0)),
                  pl.BlockSpec((d, tt), lambda i, h: (0, i)),
                  pl.BlockSpec((2, nk, nk), lambda i, h: (0, 0, 0))],
        out_specs=(ospec, ospec, ospec, ospec),
        out_shape=(out, out, out_b, out_b),
        scratch_shapes=[pltpu.VMEM((2 * nk, tt), F32)],
        compiler_params=_cparams("arbitrary", "arbitrary"),
        name="peer_router",
    )(wq_t, ht, sk)


def _peer_kernel(ht_ref, u_ref, vt_ref, p1_ref, n_ref, r2_ref, p2_ref, x_ref, g_ref, *rest,
                 ne1, nh, final_norm):
    fgain_ref = rest[0] if final_norm else None
    o_ref, acc, gbuf = rest[-3:]
    j = pl.program_id(1)
    nk = PEER_NKEYS

    @pl.when(j == 0)
    def _():
        acc[...] = jnp.zeros_like(acc)

    ht = ht_ref[...]
    for e in range(ne1):
        rows = slice(e * nk, (e + 1) * nk)
        act = _gelu_x2(jnp.dot(u_ref[rows, :], ht, preferred_element_type=F32))
        w = jnp.zeros(act.shape, BF16)
        for h in range(nh):
            p1 = jnp.broadcast_to(p1_ref[h, e:e + 1, :], act.shape).astype(BF16)
            cnt = jnp.broadcast_to(n_ref[h, e:e + 1, :], act.shape).astype(BF16)
            w = w + p1 * jnp.where(r2_ref[h] < cnt, p2_ref[h], jnp.zeros((), BF16))
        gbuf[rows, :] = act.astype(BF16) * w
    acc[...] += jnp.dot(vt_ref[...], gbuf[...], preferred_element_type=F32)

    @pl.when(j == pl.num_programs(1) - 1)
    def _():
        xn = x_ref[...] + g_ref[0] * acc[...].T
        o_ref[...] = _rmsnorm(xn, fgain_ref[...]) if final_norm else xn


def _peer(ht, u_tab, vt_tab, layer, p1, cnt, r2, p2, x2d, g, seq, final_gain=None, tt=512, eb=1024):
    d, n = ht.shape
    ne = u_tab.shape[1]
    nh, nk, _ = p1.shape
    ne1 = eb // nk
    per_batch = seq // tt
    tok = pl.BlockSpec((nh, nk, tt), lambda i, j: (0, 0, i))
    e1b = pl.BlockSpec((nh, ne1, tt), lambda i, j: (0, j, i))
    final_norm = final_gain is not None
    extra_specs = [pl.BlockSpec((1, d), lambda i, j: (0, 0))] if final_norm else []
    extra_args = [final_gain.reshape(1, d)] if final_norm else []
    return pl.pallas_call(
        functools.partial(_peer_kernel, ne1=ne1, nh=nh, final_norm=final_norm),
        grid=(n // tt, ne // eb),
        in_specs=[
            pl.BlockSpec((d, tt), lambda i, j: (0, i)),
            pl.BlockSpec((None, eb, d), lambda i, j: (layer, j, 0)),
            pl.BlockSpec((None, d, eb), lambda i, j: (layer, 0, j)),
            e1b, e1b, tok, tok,
            pl.BlockSpec((tt, d), lambda i, j: (i, 0)),
            pl.BlockSpec((1, 1, d), lambda i, j: (i // per_batch, 0, 0)),
        ] + extra_specs,
        out_specs=pl.BlockSpec((tt, d), lambda i, j: (i, 0)),
        out_shape=jax.ShapeDtypeStruct((n, d), F32),
        scratch_shapes=[pltpu.VMEM((d, tt), F32), pltpu.VMEM((eb, tt), BF16)],
        compiler_params=_cparams("arbitrary", "arbitrary", vmem=PEER_VMEM_LIMIT),
        name="peer_experts",
    )(ht, u_tab, vt_tab, p1, cnt, r2, p2, x2d, g, *extra_args)


def kernel(x, c, w_ada, b_ada, norm_mix, norm_ffn, w_in, ssm_lambda_re, ssm_lambda_im, ssm_log_dt, ssm_b_re, ssm_b_im, ssm_c_re, ssm_c_im, ssm_d, w_glu, w_br_ssm, w_br_attn, w_out, peer_wq, peer_subkeys, peer_u, peer_v, norm_final):
    bsz, seq, d = x.shape
    depth = w_ada.shape[0]
    n = bsz * seq
    ssm_w = ssm_d.shape[1]
    attn_w = w_br_attn.shape[1]
    assert bsz <= SUBLANES and seq % S5_CHUNK == 0

    c_pad = jnp.pad(c, ((0, SUBLANES - bsz), (0, 0)))
    mod_all = _modulation(c_pad, w_ada, b_ada)
    x2d = x.reshape(n, d)
    ts = S5_CHUNK // SUBLANES
    w_in_all = w_in.astype(BF16)
    u_all = peer_u.astype(BF16)
    vt_all = jnp.swapaxes(peer_v, 1, 2).astype(BF16)

    for l in range(depth):
        mods = [mod_all[l, :bsz, i * d:(i + 1) * d].reshape(bsz, 1, d) for i in range(N_MOD)]
        sh_m, sc_m, g_m, sh_f, sc_f, g_f = mods

        h = _norm_mod(x2d, norm_mix[l], sc_m, sh_m, seq)
        u, qkv, gates = _in_proj(h, w_in_all, l, ssm_w, 3 * attn_w)

        a_re, a_im, bbar_re, bbar_im, at_re, at_im = _s5_prep(
            ssm_lambda_re[l], ssm_lambda_im[l], ssm_log_dt[l], ssm_b_re[l], ssm_b_im[l], ts)
        bblk, cblk = _s5_block_weights(bbar_re, bbar_im, ssm_c_re[l], ssm_c_im[l])
        y_s = _s5_branch(u, seq, a_re, a_im, at_re, at_im, bblk, cblk, ssm_d[l],
                         w_glu[l].astype(BF16))
        y_a = _dilated_attention(qkv, seq)
        merged = _merge(y_s, y_a, w_br_ssm[l].astype(BF16), w_br_attn[l].astype(BF16), gates)
        x2d, ht = _outproj_norm(merged, w_out[l].astype(BF16), x2d, g_m, norm_ffn[l], sc_f, sh_f, seq)
        p1, cnt, r2, p2 = _router(peer_wq[l].T.astype(BF16), ht, peer_subkeys[l].astype(BF16))
        x2d = _peer(ht, u_all, vt_all, l, p1, cnt, r2, p2, x2d, g_f, seq,
                    final_gain=norm_final if l == depth - 1 else None)

    return x2d.reshape(bsz, seq, d)
```

```python
import functools
import math

import jax
import jax.numpy as jnp
import numpy as np
from jax import lax
from jax.experimental import pallas as pl
from jax.experimental.pallas import tpu as pltpu

F32 = jnp.float32
BF16 = jnp.bfloat16

EPS = 1e-6
N_MOD = 6
SSM_GROUP = 16
SSM_STATE = 64
HEAD_DIM = 128
ATTN_BLOCK = 128
ATTN_GROUP = 8
DILATED_PATTERNS = ((128, 1), (512, 4), (2048, 16))
PEER_HEADS = 8
PEER_NKEYS = 128
PEER_TOPK = 16

LANES = 128
SUBLANES = 8
S5_CHUNK = 256
S5_LANE_BLOCK = 512
S5_CH_BLOCK = 128
VMEM_LIMIT = 56 * 1024 * 1024
PEER_VMEM_LIMIT = 60 * 1024 * 1024


def _cparams(*sem, vmem=VMEM_LIMIT):
    return pltpu.CompilerParams(dimension_semantics=sem, vmem_limit_bytes=vmem)


def _gelu_x2(x):
    return x * (1.0 + lax.erf(x * np.float32(math.sqrt(0.5))))


def _gelu(x):
    return 0.5 * _gelu_x2(x)


def _split3(x):
    hi = x.astype(BF16)
    r1 = x - hi.astype(F32)
    mid = r1.astype(BF16)
    lo = (r1 - mid.astype(F32)).astype(BF16)
    return hi, mid, lo


def _mod_kernel(c_ref, w_ref, b_ref, o_ref):
    c = c_ref[...]
    ca = c * jax.nn.sigmoid(c)
    w = w_ref[0]
    c_hi = ca.astype(BF16)
    c_lo = (ca - c_hi.astype(F32)).astype(BF16)
    w_hi = w.astype(BF16)
    w_lo = (w - w_hi.astype(F32)).astype(BF16)
    acc = jnp.dot(c_hi, w_hi, preferred_element_type=F32)
    acc += jnp.dot(c_lo, w_hi, preferred_element_type=F32)
    acc += jnp.dot(c_hi, w_lo, preferred_element_type=F32)
    o_ref[0] = acc + b_ref[0]


def _modulation(c_pad, w_ada, b_ada, tn=1024):
    depth, d, cols = w_ada.shape
    return pl.pallas_call(
        _mod_kernel,
        grid=(depth, cols // tn),
        in_specs=[
            pl.BlockSpec((SUBLANES, d), lambda l, j: (0, 0)),
            pl.BlockSpec((1, d, tn), lambda l, j: (l, 0, j)),
            pl.BlockSpec((1, 1, tn), lambda l, j: (l, 0, j)),
        ],
        out_specs=pl.BlockSpec((1, SUBLANES, tn), lambda l, j: (l, 0, j)),
        out_shape=jax.ShapeDtypeStruct((depth, SUBLANES, cols), F32),
        compiler_params=_cparams("arbitrary", "arbitrary"),
        name="adaln_mod",
    )(c_pad, w_ada, b_ada.reshape(depth, 1, cols))


def _rmsnorm(x, gain):
    ms = jnp.mean(x * x, axis=-1, keepdims=True)
    return x * lax.rsqrt(ms + EPS) * gain


def _norm_mod_kernel(x_ref, gain_ref, sc_ref, sh_ref, o_ref):
    h = _rmsnorm(x_ref[...], gain_ref[...]) * (1.0 + sc_ref[0]) + sh_ref[0]
    o_ref[...] = h.astype(o_ref.dtype)


def _norm_mod(x2d, gain, sc, sh, seq, tm=512):
    n, d = x2d.shape
    per_batch = seq // tm
    mod = pl.BlockSpec((1, 1, d), lambda i: (i // per_batch, 0, 0))
    return pl.pallas_call(
        _norm_mod_kernel,
        grid=(n // tm,),
        in_specs=[pl.BlockSpec((tm, d), lambda i: (i, 0)),
                  pl.BlockSpec((1, d), lambda i: (0, 0)),
                  mod, mod],
        out_specs=pl.BlockSpec((tm, d), lambda i: (i, 0)),
        out_shape=jax.ShapeDtypeStruct((n, d), BF16),
        compiler_params=_cparams("arbitrary"),
        name="norm_mod",
    )(x2d, gain.reshape(1, d), sc, sh)


def _in_proj_kernel(h_ref, w_ref, u_ref, qkv_ref, gates_ref, *, nu, nqkv):
    j = pl.program_id(1)

    def proj():
        return jnp.dot(h_ref[...], w_ref[...], preferred_element_type=F32)

    @pl.when(j < nu)
    def _():
        u_ref[...] = proj()

    @pl.when((j >= nu) & (j < nu + nqkv))
    def _():
        qkv_ref[...] = proj()

    @pl.when(j >= nu + nqkv)
    def _():
        gates_ref[...] = jax.nn.sigmoid(proj()).astype(gates_ref.dtype)


def _in_proj(h, w_all, layer, u_cols, qkv_cols, tm=1024, tn=1024):
    n, d = h.shape
    cols = w_all.shape[2]
    nu, nqkv = u_cols // tn, qkv_cols // tn
    ng = cols // tn - nu - nqkv
    return pl.pallas_call(
        functools.partial(_in_proj_kernel, nu=nu, nqkv=nqkv),
        grid=(n // tm, cols // tn),
        in_specs=[
            pl.BlockSpec((tm, d), lambda i, j: (i, 0)),
            pl.BlockSpec((None, d, tn), lambda i, j: (layer, 0, j)),
        ],
        out_specs=(
            pl.BlockSpec((tm, tn), lambda i, j: (i, jnp.minimum(j, nu - 1))),
            pl.BlockSpec((tm, tn), lambda i, j: (i, jnp.clip(j - nu, 0, nqkv - 1))),
            pl.BlockSpec((tm, tn), lambda i, j: (i, jnp.clip(j - nu - nqkv, 0, ng - 1))),
        ),
        out_shape=(jax.ShapeDtypeStruct((n, u_cols), F32),
                   jax.ShapeDtypeStruct((n, qkv_cols), F32),
                   jax.ShapeDtypeStruct((n, ng * tn), BF16)),
        compiler_params=_cparams("arbitrary", "arbitrary"),
        name="in_proj",
    )(h, w_all)


def _merge_kernel(ys_ref, ya_ref, ws_ref, wa_ref, gs_ref, ga_ref, o_ref):
    ps = jnp.dot(ys_ref[...], ws_ref[...], preferred_element_type=F32)
    pa = jnp.dot(ya_ref[...], wa_ref[...], preferred_element_type=F32)
    o_ref[...] = (gs_ref[...].astype(F32) * ps
                  + ga_ref[...].astype(F32) * pa).astype(o_ref.dtype)


def _merge(ys, ya, ws, wa, gates, tm=1024, tn=1024):
    n, ks = ys.shape
    ka = ya.shape[1]
    d = ws.shape[1]
    goff = d // tn
    return pl.pallas_call(
        _merge_kernel,
        grid=(n // tm, d // tn),
        in_specs=[
            pl.BlockSpec((tm, ks), lambda i, j: (i, 0)),
            pl.BlockSpec((tm, ka), lambda i, j: (i, 0)),
            pl.BlockSpec((ks, tn), lambda i, j: (0, j)),
            pl.BlockSpec((ka, tn), lambda i, j: (0, j)),
            pl.BlockSpec((tm, tn), lambda i, j: (i, j)),
            pl.BlockSpec((tm, tn), lambda i, j: (i, j + goff)),
        ],
        out_specs=pl.BlockSpec((tm, tn), lambda i, j: (i, j)),
        out_shape=jax.ShapeDtypeStruct((n, d), BF16),
        compiler_params=_cparams("arbitrary", "arbitrary"),
        name="branch_merge",
    )(ys, ya, ws, wa, gates, gates)


def _outproj_norm_kernel(a_ref, w_ref, x_ref, g_ref, gain_ref, sc_ref, sh_ref, xo_ref, ht_ref):
    acc = jnp.dot(a_ref[...], w_ref[...], preferred_element_type=F32)
    xn = x_ref[...] + g_ref[0] * acc
    xo_ref[...] = xn
    h = _rmsnorm(xn, gain_ref[...]) * (1.0 + sc_ref[0]) + sh_ref[0]
    ht_ref[...] = h.T.astype(ht_ref.dtype)


def _outproj_norm(a, w, x2d, g, gain, sc, sh, seq, tm=512):
    n, k = a.shape
    d = w.shape[1]
    per_batch = seq // tm
    mod = pl.BlockSpec((1, 1, d), lambda i: (i // per_batch, 0, 0))
    return pl.pallas_call(
        _outproj_norm_kernel,
        grid=(n // tm,),
        in_specs=[
            pl.BlockSpec((tm, k), lambda i: (i, 0)),
            pl.BlockSpec((k, d), lambda i: (0, 0)),
            pl.BlockSpec((tm, d), lambda i: (i, 0)),
            mod,
            pl.BlockSpec((1, d), lambda i: (0, 0)),
            mod, mod,
        ],
        out_specs=(pl.BlockSpec((tm, d), lambda i: (i, 0)),
                   pl.BlockSpec((d, tm), lambda i: (0, i))),
        out_shape=(jax.ShapeDtypeStruct((n, d), F32), jax.ShapeDtypeStruct((d, n), BF16)),
        compiler_params=_cparams("arbitrary"),
        name="out_proj_norm",
    )(a, w, x2d, g, gain.reshape(1, d), sc, sh)


def _s5_prep_kernel(lr_ref, li_ref, ldt_ref, br_ref, bi_ref,
                    are_ref, aim_ref, bbr_ref, bbi_ref, atr_ref, ati_ref, *, ts):
    lr, li = lr_ref[...], li_ref[...]
    dt = jnp.exp(ldt_ref[...])
    mag = jnp.exp(lr * dt)
    a_re = mag * jnp.cos(li * dt)
    a_im = mag * jnp.sin(li * dt)
    inv = 1.0 / (lr * lr + li * li)
    coef_re = ((a_re - 1.0) * lr + a_im * li) * inv
    coef_im = (a_im * lr - (a_re - 1.0) * li) * inv
    br, bi = br_ref[...], bi_ref[...]
    bbr_ref[...] = coef_re * br - coef_im * bi
    bbi_ref[...] = coef_re * bi + coef_im * br
    are_ref[...] = a_re
    aim_ref[...] = a_im
    pr, pi = a_re, a_im
    for _ in range(ts - 1):
        pr, pi = pr * a_re - pi * a_im, pr * a_im + pi * a_re
    atr_ref[...] = pr
    ati_ref[...] = pi


def _s5_prep(lam_re, lam_im, log_dt, b_re, b_im, ts):
    g, p, c = b_re.shape
    ns = g * p
    row = lambda t: t.reshape(1, ns)
    ldt = jnp.broadcast_to(log_dt[:, None], (g, p))
    bt = lambda t: jnp.transpose(t, (2, 0, 1)).reshape(c, ns)
    shapes = (
        jax.ShapeDtypeStruct((1, ns), F32), jax.ShapeDtypeStruct((1, ns), F32),
        jax.ShapeDtypeStruct((c, ns), F32), jax.ShapeDtypeStruct((c, ns), F32),
        jax.ShapeDtypeStruct((1, ns), F32), jax.ShapeDtypeStruct((1, ns), F32),
    )
    return pl.pallas_call(
        functools.partial(_s5_prep_kernel, ts=ts),
        out_shape=shapes,
        name="s5_prep",
    )(row(lam_re), row(lam_im), row(ldt), bt(b_re), bt(b_im))


def _s5_block_weights(bbar_re, bbar_im, c_re, c_im):
    c, ns = bbar_re.shape
    gpb = S5_CH_BLOCK // c
    nk = ns // (gpb * SSM_STATE)
    eye = jnp.eye(gpb, dtype=F32)

    def bpart(t):
        t = t.reshape(c, nk, gpb, SSM_STATE)
        t = jnp.einsum("ckjp,ij->kicjp", t, eye)
        return t.reshape(nk, gpb * c, gpb * SSM_STATE)

    def cpart(t):
        t = t.reshape(nk, gpb, c, SSM_STATE)
        t = jnp.einsum("kicp,ij->kjpic", t, eye)
        return t.reshape(nk, gpb * SSM_STATE, gpb * c)

    bblk = jnp.concatenate([bpart(bbar_re), bpart(bbar_im)], axis=2).astype(BF16)
    cblk = jnp.concatenate([cpart(c_re), cpart(-c_im)], axis=1).astype(BF16)
    return bblk, cblk


def _s5_kernel(u_ref, pm_ref, pmt_ref, bblk_ref, are_ref, aim_ref, atr_ref, ati_ref,
               cblk_ref, d_ref, wglu_ref, o_ref,
               hr, hi, car_r, car_i, cs_r, cs_i, yp, *, ts, nk):
    lw = S5_LANE_BLOCK
    cw = S5_CH_BLOCK

    @pl.when(pl.program_id(1) == 0)
    def _():
        car_r[...] = jnp.zeros_like(car_r)
        car_i[...] = jnp.zeros_like(car_i)

    u = u_ref[...]
    up = jnp.dot(pm_ref[...], u.astype(BF16), preferred_element_type=F32).astype(BF16)
    for k in range(nk):
        bu = jnp.dot(up[:, k * cw:(k + 1) * cw], bblk_ref[k], preferred_element_type=F32)
        hr[:, k * lw:(k + 1) * lw] = bu[:, :lw]
        hi[:, k * lw:(k + 1) * lw] = bu[:, lw:]

    for k in range(nk):
        sl = pl.ds(k * lw, lw)
        ar = are_ref[:, sl]
        ai = aim_ref[:, sl]

        def end_step(j, carry, sl=sl, ar=ar, ai=ai):
            cr, cim = carry
            rows = pl.ds(pl.multiple_of(j * SUBLANES, SUBLANES), SUBLANES)
            return ar * cr - ai * cim + hr[rows, sl], ar * cim + ai * cr + hi[rows, sl]

        zero = jnp.zeros((SUBLANES, lw), F32)
        end_r, end_i = lax.fori_loop(0, ts, end_step, (zero, zero), unroll=True)
        cs_r[:, sl] = end_r
        cs_i[:, sl] = end_i

    c_r = car_r[...]
    c_i = car_i[...]
    at_r = atr_ref[...]
    at_i = ati_ref[...]
    for s in range(SUBLANES):
        l_r = cs_r[s:s + 1, :]
        l_i = cs_i[s:s + 1, :]
        cs_r[s:s + 1, :] = c_r
        cs_i[s:s + 1, :] = c_i
        c_r, c_i = l_r + at_r * c_r - at_i * c_i, l_i + at_r * c_i + at_i * c_r
    car_r[...] = c_r
    car_i[...] = c_i

    for k in range(nk):
        sl = pl.ds(k * lw, lw)
        ar = are_ref[:, sl]
        ai = aim_ref[:, sl]

        def scan_step(j, carry, sl=sl, ar=ar, ai=ai):
            cr, cim = carry
            rows = pl.ds(pl.multiple_of(j * SUBLANES, SUBLANES), SUBLANES)
            nr = ar * cr - ai * cim + hr[rows, sl]
            ni = ar * cim + ai * cr + hi[rows, sl]
            hr[rows, sl] = nr
            hi[rows, sl] = ni
            return nr, ni

        lax.fori_loop(0, ts, scan_step, (cs_r[:, sl], cs_i[:, sl]), unroll=True)
        hcat = jnp.concatenate([hr[:, sl], hi[:, sl]], axis=1).astype(BF16)
        yp[:, k * cw:(k + 1) * cw] = jnp.dot(hcat, cblk_ref[k], preferred_element_type=F32)

    y1, y2, y3 = _split3(yp[...])
    pmt = pmt_ref[...]
    y = (jnp.dot(pmt, y1, preferred_element_type=F32)
         + jnp.dot(pmt, y2, preferred_element_type=F32)
         + jnp.dot(pmt, y3, preferred_element_type=F32))
    y = y + d_ref[...] * u
    y = _gelu(y)
    gate = jax.nn.sigmoid(jnp.dot(y.astype(BF16), wglu_ref[...], preferred_element_type=F32))
    o_ref[...] = (y * gate).astype(o_ref.dtype)


def _s5_permutation(t_rows, ts):
    r = np.arange(t_rows)
    src = (r % SUBLANES) * ts + r // SUBLANES
    pm = np.zeros((t_rows, t_rows), np.float32)
    pm[r, src] = 1.0
    return jnp.asarray(pm, BF16), jnp.asarray(pm.T, BF16)


def _s5_branch(u, seq, a_re, a_im, at_re, at_im, bblk, cblk, d_skip, w_glu):
    n, w = u.shape
    ns = a_re.shape[1]
    t_rows = S5_CHUNK
    ts = t_rows // SUBLANES
    nk = ns // S5_LANE_BLOCK
    nchunk = seq // t_rows
    pm, pmt = _s5_permutation(t_rows, ts)
    full = lambda shape: pl.BlockSpec(shape, lambda b, c: (0,) * len(shape))
    rep8 = lambda row: jnp.broadcast_to(row, (SUBLANES, ns))
    return pl.pallas_call(
        functools.partial(_s5_kernel, ts=ts, nk=nk),
        grid=(n // seq, nchunk),
        in_specs=[
            pl.BlockSpec((t_rows, w), lambda b, c: (b * nchunk + c, 0)),
            full((t_rows, t_rows)), full((t_rows, t_rows)),
            full(bblk.shape),
            full((SUBLANES, ns)), full((SUBLANES, ns)), full((1, ns)), full((1, ns)),
            full(cblk.shape),
            full((1, w)), full((w, w)),
        ],
        out_specs=pl.BlockSpec((t_rows, w), lambda b, c: (b * nchunk + c, 0)),
        out_shape=jax.ShapeDtypeStruct((n, w), BF16),
        scratch_shapes=[
            pltpu.VMEM((t_rows, ns), F32), pltpu.VMEM((t_rows, ns), F32),
            pltpu.VMEM((1, ns), F32), pltpu.VMEM((1, ns), F32),
            pltpu.VMEM((SUBLANES, ns), F32), pltpu.VMEM((SUBLANES, ns), F32),
            pltpu.VMEM((t_rows, w), F32),
        ],
        compiler_params=_cparams("arbitrary", "arbitrary"),
        name="s5_branch",
    )(u, pm, pmt, bblk, rep8(a_re), rep8(a_im), at_re, at_im, cblk, d_skip.reshape(1, w), w_glu)


def _attn_kernel(q_ref, k_ref, v_ref, o_ref, o_s, lse_s, *, seq, patterns, scale):
    blk = ATTN_BLOCK
    qi = lax.broadcasted_iota(jnp.int32, (blk, 2 * blk), 0)
    kj = lax.broadcasted_iota(jnp.int32, (blk, 2 * blk), 1)
    dist = blk + qi - kj
    nt = (((1,), (1,)), ((), ()))

    for pi, (window, dil) in enumerate(patterns):
        n_back = window // dil
        nblk = seq // dil // blk
        grp = min(ATTN_GROUP, nblk)
        ngrp = nblk // grp
        band = (dist >= 0) & (dist <= n_back)

        nres = max(1, ATTN_GROUP // grp)

        def group_step(idx, carry, pi=pi, dil=dil, grp=grp, ngrp=ngrp, nres=nres, band=band):
            r0 = (idx // ngrp) * nres
            n0 = (idx % ngrp) * grp

            def rows_of(r, nb):
                start = r + nb * (blk * dil)
                return pl.ds(start, blk) if dil == 1 else pl.ds(start, blk, stride=dil)

            chains = [[rows_of(r0 + ri, jnp.maximum(n0 - 1, 0))]
                      + [rows_of(r0 + ri, n0 + gi) for gi in range(grp)] for ri in range(nres)]
            units = [(ri, gi) for ri in range(nres) for gi in range(grp)]
            ks = [[k_ref[rw, :].astype(BF16) for rw in rows] for rows in chains]
            scores = []
            for ri, gi in units:
                q = q_ref[chains[ri][gi + 1], :].astype(BF16)
                scores.append(jnp.concatenate(
                    [lax.dot_general(q, ks[ri][gi], nt, preferred_element_type=F32),
                     lax.dot_general(q, ks[ri][gi + 1], nt, preferred_element_type=F32)], axis=1))
            probs, inv_l = [], []
            for (ri, gi), sc in zip(units, scores):
                rows = chains[ri][gi + 1]
                mask = band & ((kj >= blk) | (n0 > 0)) if gi == 0 else band
                s = jnp.where(mask, sc * scale, -jnp.inf)
                m = jnp.max(s, axis=1, keepdims=True)
                p = jnp.exp(s - m)
                l = jnp.sum(p, axis=1, keepdims=True)
                lse_s[pi, rows, :] = jnp.broadcast_to(m + jnp.log(l), (blk, HEAD_DIM))
                inv_l.append(1.0 / l)
                probs.append(p.astype(BF16))
            vs = [[v_ref[rw, :].astype(BF16) for rw in rows] for rows in chains]
            for (ri, gi), pb, il in zip(units, probs, inv_l):
                o_s[pi, chains[ri][gi + 1], :] = il * (
                    jnp.dot(pb[:, :blk], vs[ri][gi], preferred_element_type=F32)
                    + jnp.dot(pb[:, blk:], vs[ri][gi + 1], preferred_element_type=F32))
            return carry

        lax.fori_loop(0, (dil // nres) * ngrp, group_step, 0, unroll=2)

    npat = len(patterns)

    def merge_step(t, carry):
        rows = pl.ds(pl.multiple_of(t * blk, blk), blk)
        lses = [lse_s[pi, rows, :] for pi in range(npat)]
        mx = functools.reduce(jnp.maximum, lses)
        num = jnp.zeros((blk, HEAD_DIM), F32)
        den = jnp.zeros((blk, HEAD_DIM), F32)
        for pi in range(npat):
            w = jnp.exp(lses[pi] - mx)
            num = num + w * o_s[pi, rows, :]
            den = den + w
        o_ref[rows, :] = (num / den).astype(o_ref.dtype)
        return carry

    lax.fori_loop(0, seq // blk, merge_step, 0)


def _dilated_attention(qkv, seq):
    n = qkv.shape[0]
    nh = qkv.shape[1] // (3 * HEAD_DIM)
    for window, dil in DILATED_PATTERNS:
        assert window // dil <= ATTN_BLOCK and seq % (dil * ATTN_BLOCK) == 0
    blk = (seq, HEAD_DIM)
    return pl.pallas_call(
        functools.partial(_attn_kernel, seq=seq, patterns=DILATED_PATTERNS,
                          scale=np.float32(HEAD_DIM ** -0.5)),
        grid=(n // seq, nh),
        in_specs=[pl.BlockSpec(blk, lambda b, h: (b, h)),
                  pl.BlockSpec(blk, lambda b, h: (b, nh + h)),
                  pl.BlockSpec(blk, lambda b, h: (b, 2 * nh + h))],
        out_specs=pl.BlockSpec(blk, lambda b, h: (b, h)),
        out_shape=jax.ShapeDtypeStruct((n, nh * HEAD_DIM), BF16),
        scratch_shapes=[pltpu.VMEM((len(DILATED_PATTERNS),) + blk, F32)] * 2,
        compiler_params=_cparams("arbitrary", "arbitrary"),
        name="dilated_attn",
    )(qkv, qkv, qkv)


def _top16(s, kid, exact_ties):
    nkeys = s.shape[0]
    vals = []
    if exact_ties:
        rank = jnp.full(s.shape, float(nkeys), F32)
        for it in range(PEER_TOPK):
            m = jnp.max(s, axis=0, keepdims=True)
            idx = jnp.min(jnp.where(s == m, kid, float(nkeys)), axis=0, keepdims=True)
            sel = kid == idx
            rank = jnp.where(sel, float(it), rank)
            s = jnp.where(sel, -jnp.inf, s)
            vals.append(m)
        return rank, jnp.concatenate(vals, axis=0)
    floor = np.float32(-(2.0 ** 125))
    for it in range(PEER_TOPK):
        m = jnp.max(s, axis=0, keepdims=True)
        s = jnp.where(s == m, np.float32(-(2.0 ** 126 + it * 2.0 ** 103)), s)
        vals.append(m)
    step = (-s - np.float32(2.0 ** 126)) * np.float32(2.0 ** -103)
    rank = jnp.where(s <= floor, jnp.maximum(step, 0.0), float(nkeys))
    return rank, jnp.concatenate(vals, axis=0)


def _router_kernel(wq_ref, ht_ref, sk_ref, p1_ref, n_ref, r2_ref, p2_ref, q_s, *, ngroups):
    nk = PEER_NKEYS
    kt = PEER_TOPK
    half = kt // 2
    kid = lax.broadcasted_iota(jnp.int32, (nk, LANES), 0).astype(F32)
    ia = lax.broadcasted_iota(jnp.int32, (kt, LANES), 0).astype(F32)
    ih = ia[0:half]
    cid = jnp.concatenate([ia * kt] + [ih * kt + float(b) for b in range(1, half)] + [ih + float(half)],
                          axis=0)
    ncand = kt * kt
    tail0 = kt + (half - 1) * half

    q_s[...] = jnp.dot(wq_ref[...], ht_ref[...], preferred_element_type=F32)

    def route(lanes, exact_ties):
        q1 = q_s[0:nk, lanes].astype(BF16)
        q2 = q_s[nk:2 * nk, lanes].astype(BF16)
        s1 = jnp.dot(sk_ref[0], q1, preferred_element_type=F32)
        s2 = jnp.dot(sk_ref[1], q2, preferred_element_type=F32)
        rank1, v1 = _top16(s1, kid, exact_ties)
        rank2, v2 = _top16(s2, kid, exact_ties)
        work = jnp.concatenate([v1 + v2[0:1]] + [v1[0:half] + v2[b:b + 1] for b in range(1, half)]
                               + [v1[0:1] + v2[half:kt]], axis=0)
        m0 = v1[0:1] + v2[0:1]
        z = jnp.zeros((1, LANES), F32)
        if exact_ties:
            n_a = jnp.zeros((kt, LANES), F32)
            for _ in range(kt):
                m = jnp.max(work, axis=0, keepdims=True)
                idx = jnp.min(jnp.where(work == m, cid, float(ncand)), axis=0, keepdims=True)
                work = jnp.where(cid == idx, -jnp.inf, work)
                a_sel = jnp.floor(idx * (1.0 / kt))
                n_a = n_a + jnp.where(ia == a_sel, 1.0, 0.0)
                z = z + jnp.exp(m - m0)
        else:
            for _ in range(kt):
                m = jnp.max(work, axis=0, keepdims=True)
                work = jnp.where(work == m, -jnp.inf, work)
                z = z + jnp.exp(m - m0)
            mark = jnp.where(work == -jnp.inf, 1.0, 0.0)
            low = mark[0:half]
            for b in range(1, half):
                low = low + mark[kt + (b - 1) * half:kt + b * half]
            tail = jnp.sum(mark[tail0:], axis=0, keepdims=True)
            n_a = jnp.concatenate([low + jnp.where(ih == 0.0, tail, 0.0), mark[half:kt]], axis=0)
        nkey = jnp.zeros((nk, LANES), F32)
        for a in range(kt):
            nkey = jnp.where(rank1 == float(a), n_a[a:a + 1], nkey)
        p1_ref[0, :, lanes] = jnp.exp(s1 - v1[0:1])
        n_ref[0, :, lanes] = nkey
        r2_ref[0, :, lanes] = rank2.astype(r2_ref.dtype)
        p2_ref[0, :, lanes] = (jnp.exp(s2 - v2[0:1]) / (2.0 * z)).astype(p2_ref.dtype)
        if exact_ties:
            return None
        ranked = lambda r: jnp.sum(jnp.where(r < float(nk), 1.0, 0.0), axis=0, keepdims=True)
        ok = ((ranked(rank1) == float(kt)) & (ranked(rank2) == float(kt))
              & (jnp.sum(n_a, axis=0, keepdims=True) == float(kt)))
        return jnp.min(jnp.where(ok, 1.0, 0.0))

    groups = [slice(g * LANES, (g + 1) * LANES) for g in range(ngroups)]
    all_distinct = [route(lanes, False) for lanes in groups]
    for lanes, ok in zip(groups, all_distinct):
        @pl.when(ok < 0.5)
        def _(lanes=lanes):
            route(lanes, True)


def _router(wq_t, ht, sk, tt=1024):
    d, n = ht.shape
    nh = PEER_HEADS
    nk = PEER_NKEYS
    out = jax.ShapeDtypeStruct((nh, nk, n), F32)
    out_b = jax.ShapeDtypeStruct((nh, nk, n), BF16)
    ospec = pl.BlockSpec((1, nk, tt), lambda i, h: (h, 0, i))
    return pl.pallas_call(
        functools.partial(_router_kernel, ngroups=tt // LANES),
        grid=(n // tt, nh),
        in_specs=[pl.BlockSpec((2 * nk, d), lambda i, h: (h, 0)),
                  pl.BlockSpec((d, tt), lambda i, h: (0, i)),
                  pl.BlockSpec((2, nk, nk), lambda i, h: (0, 0, 0))],
        out_specs=(ospec, ospec, ospec, ospec),
        out_shape=(out, out, out_b, out_b),
        scratch_shapes=[pltpu.VMEM((2 * nk, tt), F32)],
        compiler_params=_cparams("arbitrary", "arbitrary"),
        name="peer_router",
    )(wq_t, ht, sk)


def _peer_kernel(ht_ref, u_ref, vt_ref, p1_ref, n_ref, r2_ref, p2_ref, x_ref, g_ref, *rest,
                 ne1, nh, final_norm):
    fgain_ref = rest[0] if final_norm else None
    o_ref, acc, gbuf = rest[-3:]
    j = pl.program_id(1)
    nk = PEER_NKEYS

    @pl.when(j == 0)
    def _():
        acc[...] = jnp.zeros_like(acc)

    ht = ht_ref[...]
    for e in range(ne1):
        rows = slice(e * nk, (e + 1) * nk)
        act = _gelu_x2(jnp.dot(u_ref[rows, :], ht, preferred_element_type=F32))
        w = jnp.zeros(act.shape, BF16)
        for h in range(nh):
            p1 = jnp.broadcast_to(p1_ref[h, e:e + 1, :], act.shape).astype(BF16)
            cnt = jnp.broadcast_to(n_ref[h, e:e + 1, :], act.shape).astype(BF16)
            w = w + p1 * jnp.where(r2_ref[h] < cnt, p2_ref[h], jnp.zeros((), BF16))
        gbuf[rows, :] = act.astype(BF16) * w
    acc[...] += jnp.dot(vt_ref[...], gbuf[...], preferred_element_type=F32)

    @pl.when(j == pl.num_programs(1) - 1)
    def _():
        xn = x_ref[...] + g_ref[0] * acc[...].T
        o_ref[...] = _rmsnorm(xn, fgain_ref[...]) if final_norm else xn


def _peer(ht, u_tab, vt_tab, layer, p1, cnt, r2, p2, x2d, g, seq, final_gain=None, tt=512, eb=1024):
    d, n = ht.shape
    ne = u_tab.shape[1]
    nh, nk, _ = p1.shape
    ne1 = eb // nk
    per_batch = seq // tt
    tok = pl.BlockSpec((nh, nk, tt), lambda i, j: (0, 0, i))
    e1b = pl.BlockSpec((nh, ne1, tt), lambda i, j: (0, j, i))
    final_norm = final_gain is not None
    extra_specs = [pl.BlockSpec((1, d), lambda i, j: (0, 0))] if final_norm else []
    extra_args = [final_gain.reshape(1, d)] if final_norm else []
    return pl.pallas_call(
        functools.partial(_peer_kernel, ne1=ne1, nh=nh, final_norm=final_norm),
        grid=(n // tt, ne // eb),
        in_specs=[
            pl.BlockSpec((d, tt), lambda i, j: (0, i)),
            pl.BlockSpec((None, eb, d), lambda i, j: (layer, j, 0)),
            pl.BlockSpec((None, d, eb), lambda i, j: (layer, 0, j)),
            e1b, e1b, tok, tok,
            pl.BlockSpec((tt, d), lambda i, j: (i, 0)),
            pl.BlockSpec((1, 1, d), lambda i, j: (i // per_batch, 0, 0)),
        ] + extra_specs,
        out_specs=pl.BlockSpec((tt, d), lambda i, j: (i, 0)),
        out_shape=jax.ShapeDtypeStruct((n, d), F32),
        scratch_shapes=[pltpu.VMEM((d, tt), F32), pltpu.VMEM((eb, tt), BF16)],
        compiler_params=_cparams("arbitrary", "arbitrary", vmem=PEER_VMEM_LIMIT),
        name="peer_experts",
    )(ht, u_tab, vt_tab, p1, cnt, r2, p2, x2d, g, *extra_args)


def kernel(x, c, w_ada, b_ada, norm_mix, norm_ffn, w_in, ssm_lambda_re, ssm_lambda_im, ssm_log_dt, ssm_b_re, ssm_b_im, ssm_c_re, ssm_c_im, ssm_d, w_glu, w_br_ssm, w_br_attn, w_out, peer_wq, peer_subkeys, peer_u, peer_v, norm_final):
    bsz, seq, d = x.shape
    depth = w_ada.shape[0]
    n = bsz * seq
    ssm_w = ssm_d.shape[1]
    attn_w = w_br_attn.shape[1]
    assert bsz <= SUBLANES and seq % S5_CHUNK == 0

    c_pad = jnp.pad(c, ((0, SUBLANES - bsz), (0, 0)))
    mod_all = _modulation(c_pad, w_ada, b_ada)
    x2d = x.reshape(n, d)
    ts = S5_CHUNK // SUBLANES
    w_in_all = w_in.astype(BF16)
    u_all = peer_u.astype(BF16)
    vt_all = jnp.swapaxes(peer_v, 1, 2).astype(BF16)

    for l in range(depth):
        mods = [mod_all[l, :bsz, i * d:(i + 1) * d].reshape(bsz, 1, d) for i in range(N_MOD)]
        sh_m, sc_m, g_m, sh_f, sc_f, g_f = mods

        h = _norm_mod(x2d, norm_mix[l], sc_m, sh_m, seq)
        u, qkv, gates = _in_proj(h, w_in_all, l, ssm_w, 3 * attn_w)

        a_re, a_im, bbar_re, bbar_im, at_re, at_im = _s5_prep(
            ssm_lambda_re[l], ssm_lambda_im[l], ssm_log_dt[l], ssm_b_re[l], ssm_b_im[l], ts)
        bblk, cblk = _s5_block_weights(bbar_re, bbar_im, ssm_c_re[l], ssm_c_im[l])
        y_s = _s5_branch(u, seq, a_re, a_im, at_re, at_im, bblk, cblk, ssm_d[l],
                         w_glu[l].astype(BF16))
        y_a = _dilated_attention(qkv, seq)
        merged = _merge(y_s, y_a, w_br_ssm[l].astype(BF16), w_br_attn[l].astype(BF16), gates)
        x2d, ht = _outproj_norm(merged, w_out[l].astype(BF16), x2d, g_m, norm_ffn[l], sc_f, sh_f, seq)
        p1, cnt, r2, p2 = _router(peer_wq[l].T.astype(BF16), ht, peer_subkeys[l].astype(BF16))
        x2d = _peer(ht, u_all, vt_all, l, p1, cnt, r2, p2, x2d, g_f, seq,
                    final_gain=norm_final if l == depth - 1 else None)

    return x2d.reshape(bsz, seq, d)
```

```python
import functools
import math

import jax
import jax.numpy as jnp
import numpy as np
from jax import lax
from jax.experimental import pallas as pl
from jax.experimental.pallas import tpu as pltpu

F32 = jnp.float32
BF16 = jnp.bfloat16

EPS = 1e-6
N_MOD = 6
SSM_GROUP = 16
SSM_STATE = 64
HEAD_DIM = 128
ATTN_BLOCK = 128
ATTN_GROUP = 8
DILATED_PATTERNS = ((128, 1), (512, 4), (2048, 16))
PEER_HEADS = 8
PEER_NKEYS = 128
PEER_TOPK = 16
PEER_KEYS_PER_MATMUL = 2

LANES = 128
SUBLANES = 8
S5_CHUNK = 256
S5_LANE_BLOCK = 512
S5_CH_BLOCK = 128
VMEM_LIMIT = 56 * 1024 * 1024
PEER_VMEM_LIMIT = 60 * 1024 * 1024


def _cparams(*sem, vmem=VMEM_LIMIT):
    return pltpu.CompilerParams(dimension_semantics=sem, vmem_limit_bytes=vmem)


def _gelu_x2(x):
    return x * (1.0 + lax.erf(x * np.float32(math.sqrt(0.5))))


def _gelu(x):
    return 0.5 * _gelu_x2(x)


def _split3(x):
    hi = x.astype(BF16)
    r1 = x - hi.astype(F32)
    mid = r1.astype(BF16)
    lo = (r1 - mid.astype(F32)).astype(BF16)
    return hi, mid, lo


def _mod_kernel(c_ref, w_ref, b_ref, o_ref):
    c = c_ref[...]
    ca = c * jax.nn.sigmoid(c)
    w = w_ref[0]
    c_hi = ca.astype(BF16)
    c_lo = (ca - c_hi.astype(F32)).astype(BF16)
    w_hi = w.astype(BF16)
    w_lo = (w - w_hi.astype(F32)).astype(BF16)
    acc = jnp.dot(c_hi, w_hi, preferred_element_type=F32)
    acc += jnp.dot(c_lo, w_hi, preferred_element_type=F32)
    acc += jnp.dot(c_hi, w_lo, preferred_element_type=F32)
    o_ref[0] = acc + b_ref[0]


def _modulation(c_pad, w_ada, b_ada, tn=1024):
    depth, d, cols = w_ada.shape
    return pl.pallas_call(
        _mod_kernel,
        grid=(depth, cols // tn),
        in_specs=[
            pl.BlockSpec((SUBLANES, d), lambda l, j: (0, 0)),
            pl.BlockSpec((1, d, tn), lambda l, j: (l, 0, j)),
            pl.BlockSpec((1, 1, tn), lambda l, j: (l, 0, j)),
        ],
        out_specs=pl.BlockSpec((1, SUBLANES, tn), lambda l, j: (l, 0, j)),
        out_shape=jax.ShapeDtypeStruct((depth, SUBLANES, cols), F32),
        compiler_params=_cparams("arbitrary", "arbitrary"),
        name="adaln_mod",
    )(c_pad, w_ada, b_ada.reshape(depth, 1, cols))


def _rmsnorm(x, gain):
    ms = jnp.mean(x * x, axis=-1, keepdims=True)
    return x * lax.rsqrt(ms + EPS) * gain


def _norm_mod_kernel(x_ref, gain_ref, sc_ref, sh_ref, o_ref):
    h = _rmsnorm(x_ref[...], gain_ref[...]) * (1.0 + sc_ref[0]) + sh_ref[0]
    o_ref[...] = h.astype(o_ref.dtype)


def _norm_mod(x2d, gain, sc, sh, seq, tm=512):
    n, d = x2d.shape
    per_batch = seq // tm
    mod = pl.BlockSpec((1, 1, d), lambda i: (i // per_batch, 0, 0))
    return pl.pallas_call(
        _norm_mod_kernel,
        grid=(n // tm,),
        in_specs=[pl.BlockSpec((tm, d), lambda i: (i, 0)),
                  pl.BlockSpec((1, d), lambda i: (0, 0)),
                  mod, mod],
        out_specs=pl.BlockSpec((tm, d), lambda i: (i, 0)),
        out_shape=jax.ShapeDtypeStruct((n, d), BF16),
        compiler_params=_cparams("arbitrary"),
        name="norm_mod",
    )(x2d, gain.reshape(1, d), sc, sh)


def _in_proj_kernel(h_ref, w_ref, u_ref, qkv_ref, gates_ref, *, nu, nqkv):
    j = pl.program_id(1)

    def proj():
        return jnp.dot(h_ref[...], w_ref[...], preferred_element_type=F32)

    @pl.when(j < nu)
    def _():
        u_ref[...] = proj()

    @pl.when((j >= nu) & (j < nu + nqkv))
    def _():
        qkv_ref[...] = proj()

    @pl.when(j >= nu + nqkv)
    def _():
        gates_ref[...] = jax.nn.sigmoid(proj()).astype(gates_ref.dtype)


def _in_proj(h, w_all, layer, u_cols, qkv_cols, tm=1024, tn=1024):
    n, d = h.shape
    cols = w_all.shape[2]
    nu, nqkv = u_cols // tn, qkv_cols // tn
    ng = cols // tn - nu - nqkv
    return pl.pallas_call(
        functools.partial(_in_proj_kernel, nu=nu, nqkv=nqkv),
        grid=(n // tm, cols // tn),
        in_specs=[
            pl.BlockSpec((tm, d), lambda i, j: (i, 0)),
            pl.BlockSpec((None, d, tn), lambda i, j: (layer, 0, j)),
        ],
        out_specs=(
            pl.BlockSpec((tm, tn), lambda i, j: (i, jnp.minimum(j, nu - 1))),
            pl.BlockSpec((tm, tn), lambda i, j: (i, jnp.clip(j - nu, 0, nqkv - 1))),
            pl.BlockSpec((tm, tn), lambda i, j: (i, jnp.clip(j - nu - nqkv, 0, ng - 1))),
        ),
        out_shape=(jax.ShapeDtypeStruct((n, u_cols), F32),
                   jax.ShapeDtypeStruct((n, qkv_cols), F32),
                   jax.ShapeDtypeStruct((n, ng * tn), BF16)),
        compiler_params=_cparams("arbitrary", "arbitrary"),
        name="in_proj",
    )(h, w_all)


def _merge_kernel(ys_ref, ya_ref, ws_ref, wa_ref, gs_ref, ga_ref, o_ref):
    ps = jnp.dot(ys_ref[...], ws_ref[...], preferred_element_type=F32)
    pa = jnp.dot(ya_ref[...], wa_ref[...], preferred_element_type=F32)
    o_ref[...] = (gs_ref[...].astype(F32) * ps
                  + ga_ref[...].astype(F32) * pa).astype(o_ref.dtype)


def _merge(ys, ya, ws, wa, gates, tm=1024, tn=1024):
    n, ks = ys.shape
    ka = ya.shape[1]
    d = ws.shape[1]
    goff = d // tn
    return pl.pallas_call(
        _merge_kernel,
        grid=(n // tm, d // tn),
        in_specs=[
            pl.BlockSpec((tm, ks), lambda i, j: (i, 0)),
            pl.BlockSpec((tm, ka), lambda i, j: (i, 0)),
            pl.BlockSpec((ks, tn), lambda i, j: (0, j)),
            pl.BlockSpec((ka, tn), lambda i, j: (0, j)),
            pl.BlockSpec((tm, tn), lambda i, j: (i, j)),
            pl.BlockSpec((tm, tn), lambda i, j: (i, j + goff)),
        ],
        out_specs=pl.BlockSpec((tm, tn), lambda i, j: (i, j)),
        out_shape=jax.ShapeDtypeStruct((n, d), BF16),
        compiler_params=_cparams("arbitrary", "arbitrary"),
        name="branch_merge",
    )(ys, ya, ws, wa, gates, gates)


def _outproj_norm_kernel(a_ref, w_ref, x_ref, g_ref, gain_ref, sc_ref, sh_ref, xo_ref, ht_ref):
    acc = jnp.dot(a_ref[...], w_ref[...], preferred_element_type=F32)
    xn = x_ref[...] + g_ref[0] * acc
    xo_ref[...] = xn
    h = _rmsnorm(xn, gain_ref[...]) * (1.0 + sc_ref[0]) + sh_ref[0]
    ht_ref[...] = h.T.astype(ht_ref.dtype)


def _outproj_norm(a, w, x2d, g, gain, sc, sh, seq, tm=512):
    n, k = a.shape
    d = w.shape[1]
    per_batch = seq // tm
    mod = pl.BlockSpec((1, 1, d), lambda i: (i // per_batch, 0, 0))
    return pl.pallas_call(
        _outproj_norm_kernel,
        grid=(n // tm,),
        in_specs=[
            pl.BlockSpec((tm, k), lambda i: (i, 0)),
            pl.BlockSpec((k, d), lambda i: (0, 0)),
            pl.BlockSpec((tm, d), lambda i: (i, 0)),
            mod,
            pl.BlockSpec((1, d), lambda i: (0, 0)),
            mod, mod,
        ],
        out_specs=(pl.BlockSpec((tm, d), lambda i: (i, 0)),
                   pl.BlockSpec((d, tm), lambda i: (0, i))),
        out_shape=(jax.ShapeDtypeStruct((n, d), F32), jax.ShapeDtypeStruct((d, n), BF16)),
        compiler_params=_cparams("arbitrary"),
        name="out_proj_norm",
    )(a, w, x2d, g, gain.reshape(1, d), sc, sh)


def _s5_prep_kernel(lr_ref, li_ref, ldt_ref, br_ref, bi_ref,
                    are_ref, aim_ref, bbr_ref, bbi_ref, atr_ref, ati_ref, *, ts):
    lr, li = lr_ref[...], li_ref[...]
    dt = jnp.exp(ldt_ref[...])
    mag = jnp.exp(lr * dt)
    a_re = mag * jnp.cos(li * dt)
    a_im = mag * jnp.sin(li * dt)
    inv = 1.0 / (lr * lr + li * li)
    coef_re = ((a_re - 1.0) * lr + a_im * li) * inv
    coef_im = (a_im * lr - (a_re - 1.0) * li) * inv
    br, bi = br_ref[...], bi_ref[...]
    bbr_ref[...] = coef_re * br - coef_im * bi
    bbi_ref[...] = coef_re * bi + coef_im * br
    are_ref[...] = a_re
    aim_ref[...] = a_im
    pr, pi = a_re, a_im
    for _ in range(ts - 1):
        pr, pi = pr * a_re - pi * a_im, pr * a_im + pi * a_re
    atr_ref[...] = pr
    ati_ref[...] = pi


def _s5_prep(lam_re, lam_im, log_dt, b_re, b_im, ts):
    g, p, c = b_re.shape
    ns = g * p
    row = lambda t: t.reshape(1, ns)
    ldt = jnp.broadcast_to(log_dt[:, None], (g, p))
    bt = lambda t: jnp.transpose(t, (2, 0, 1)).reshape(c, ns)
    shapes = (
        jax.ShapeDtypeStruct((1, ns), F32), jax.ShapeDtypeStruct((1, ns), F32),
        jax.ShapeDtypeStruct((c, ns), F32), jax.ShapeDtypeStruct((c, ns), F32),
        jax.ShapeDtypeStruct((1, ns), F32), jax.ShapeDtypeStruct((1, ns), F32),
    )
    return pl.pallas_call(
        functools.partial(_s5_prep_kernel, ts=ts),
        out_shape=shapes,
        name="s5_prep",
    )(row(lam_re), row(lam_im), row(ldt), bt(b_re), bt(b_im))


def _s5_block_weights(bbar_re, bbar_im, c_re, c_im):
    c, ns = bbar_re.shape
    gpb = S5_CH_BLOCK // c
    nk = ns // (gpb * SSM_STATE)
    eye = jnp.eye(gpb, dtype=F32)

    def bpart(t):
        t = t.reshape(c, nk, gpb, SSM_STATE)
        t = jnp.einsum("ckjp,ij->kicjp", t, eye)
        return t.reshape(nk, gpb * c, gpb * SSM_STATE)

    def cpart(t):
        t = t.reshape(nk, gpb, c, SSM_STATE)
        t = jnp.einsum("kicp,ij->kjpic", t, eye)
        return t.reshape(nk, gpb * SSM_STATE, gpb * c)

    bblk = jnp.concatenate([bpart(bbar_re), bpart(bbar_im)], axis=2).astype(BF16)
    cblk = jnp.concatenate([cpart(c_re), cpart(-c_im)], axis=1).astype(BF16)
    return bblk, cblk


def _s5_kernel(u_ref, pm_ref, pmt_ref, bblk_ref, are_ref, aim_ref, atr_ref, ati_ref,
               cblk_ref, d_ref, wglu_ref, o_ref,
               hr, hi, car_r, car_i, cs_r, cs_i, yp, *, ts, nk):
    lw = S5_LANE_BLOCK
    cw = S5_CH_BLOCK

    @pl.when(pl.program_id(1) == 0)
    def _():
        car_r[...] = jnp.zeros_like(car_r)
        car_i[...] = jnp.zeros_like(car_i)

    u = u_ref[...]
    up = jnp.dot(pm_ref[...], u.astype(BF16), preferred_element_type=F32).astype(BF16)
    for k in range(nk):
        bu = jnp.dot(up[:, k * cw:(k + 1) * cw], bblk_ref[k], preferred_element_type=F32)
        hr[:, k * lw:(k + 1) * lw] = bu[:, :lw]
        hi[:, k * lw:(k + 1) * lw] = bu[:, lw:]

    for k in range(nk):
        sl = pl.ds(k * lw, lw)
        ar = are_ref[:, sl]
        ai = aim_ref[:, sl]

        def end_step(j, carry, sl=sl, ar=ar, ai=ai):
            cr, cim = carry
            rows = pl.ds(pl.multiple_of(j * SUBLANES, SUBLANES), SUBLANES)
            return ar * cr - ai * cim + hr[rows, sl], ar * cim + ai * cr + hi[rows, sl]

        zero = jnp.zeros((SUBLANES, lw), F32)
        end_r, end_i = lax.fori_loop(0, ts, end_step, (zero, zero), unroll=True)
        cs_r[:, sl] = end_r
        cs_i[:, sl] = end_i

    c_r = car_r[...]
    c_i = car_i[...]
    at_r = atr_ref[...]
    at_i = ati_ref[...]
    for s in range(SUBLANES):
        l_r = cs_r[s:s + 1, :]
        l_i = cs_i[s:s + 1, :]
        cs_r[s:s + 1, :] = c_r
        cs_i[s:s + 1, :] = c_i
        c_r, c_i = l_r + at_r * c_r - at_i * c_i, l_i + at_r * c_i + at_i * c_r
    car_r[...] = c_r
    car_i[...] = c_i

    for k in range(nk):
        sl = pl.ds(k * lw, lw)
        ar = are_ref[:, sl]
        ai = aim_ref[:, sl]

        def scan_step(j, carry, sl=sl, ar=ar, ai=ai):
            cr, cim = carry
            rows = pl.ds(pl.multiple_of(j * SUBLANES, SUBLANES), SUBLANES)
            nr = ar * cr - ai * cim + hr[rows, sl]
            ni = ar * cim + ai * cr + hi[rows, sl]
            hr[rows, sl] = nr
            hi[rows, sl] = ni
            return nr, ni

        lax.fori_loop(0, ts, scan_step, (cs_r[:, sl], cs_i[:, sl]), unroll=True)
        hcat = jnp.concatenate([hr[:, sl], hi[:, sl]], axis=1).astype(BF16)
        yp[:, k * cw:(k + 1) * cw] = jnp.dot(hcat, cblk_ref[k], preferred_element_type=F32)

    y1, y2, y3 = _split3(yp[...])
    pmt = pmt_ref[...]
    y = (jnp.dot(pmt, y1, preferred_element_type=F32)
         + jnp.dot(pmt, y2, preferred_element_type=F32)
         + jnp.dot(pmt, y3, preferred_element_type=F32))
    y = y + d_ref[...] * u
    y = _gelu(y)
    gate = jax.nn.sigmoid(jnp.dot(y.astype(BF16), wglu_ref[...], preferred_element_type=F32))
    o_ref[...] = (y * gate).astype(o_ref.dtype)


def _s5_permutation(t_rows, ts):
    r = np.arange(t_rows)
    src = (r % SUBLANES) * ts + r // SUBLANES
    pm = np.zeros((t_rows, t_rows), np.float32)
    pm[r, src] = 1.0
    return jnp.asarray(pm, BF16), jnp.asarray(pm.T, BF16)


def _s5_branch(u, seq, a_re, a_im, at_re, at_im, bblk, cblk, d_skip, w_glu):
    n, w = u.shape
    ns = a_re.shape[1]
    t_rows = S5_CHUNK
    ts = t_rows // SUBLANES
    nk = ns // S5_LANE_BLOCK
    nchunk = seq // t_rows
    pm, pmt = _s5_permutation(t_rows, ts)
    full = lambda shape: pl.BlockSpec(shape, lambda b, c: (0,) * len(shape))
    rep8 = lambda row: jnp.broadcast_to(row, (SUBLANES, ns))
    return pl.pallas_call(
        functools.partial(_s5_kernel, ts=ts, nk=nk),
        grid=(n // seq, nchunk),
        in_specs=[
            pl.BlockSpec((t_rows, w), lambda b, c: (b * nchunk + c, 0)),
            full((t_rows, t_rows)), full((t_rows, t_rows)),
            full(bblk.shape),
            full((SUBLANES, ns)), full((SUBLANES, ns)), full((1, ns)), full((1, ns)),
            full(cblk.shape),
            full((1, w)), full((w, w)),
        ],
        out_specs=pl.BlockSpec((t_rows, w), lambda b, c: (b * nchunk + c, 0)),
        out_shape=jax.ShapeDtypeStruct((n, w), BF16),
        scratch_shapes=[
            pltpu.VMEM((t_rows, ns), F32), pltpu.VMEM((t_rows, ns), F32),
            pltpu.VMEM((1, ns), F32), pltpu.VMEM((1, ns), F32),
            pltpu.VMEM((SUBLANES, ns), F32), pltpu.VMEM((SUBLANES, ns), F32),
            pltpu.VMEM((t_rows, w), F32),
        ],
        compiler_params=_cparams("arbitrary", "arbitrary"),
        name="s5_branch",
    )(u, pm, pmt, bblk, rep8(a_re), rep8(a_im), at_re, at_im, cblk, d_skip.reshape(1, w), w_glu)


def _attn_kernel(q_ref, k_ref, v_ref, o_ref, o_s, lse_s, *, seq, patterns, scale):
    blk = ATTN_BLOCK
    qi = lax.broadcasted_iota(jnp.int32, (blk, 2 * blk), 0)
    kj = lax.broadcasted_iota(jnp.int32, (blk, 2 * blk), 1)
    dist = blk + qi - kj
    nt = (((1,), (1,)), ((), ()))

    for pi, (window, dil) in enumerate(patterns):
        n_back = window // dil
        nblk = seq // dil // blk
        grp = min(ATTN_GROUP, nblk)
        ngrp = nblk // grp
        band = (dist >= 0) & (dist <= n_back)

        nres = max(1, ATTN_GROUP // grp)

        def group_step(idx, carry, pi=pi, dil=dil, grp=grp, ngrp=ngrp, nres=nres, band=band):
            r0 = (idx // ngrp) * nres
            n0 = (idx % ngrp) * grp

            def rows_of(r, nb):
                start = r + nb * (blk * dil)
                return pl.ds(start, blk) if dil == 1 else pl.ds(start, blk, stride=dil)

            chains = [[rows_of(r0 + ri, jnp.maximum(n0 - 1, 0))]
                      + [rows_of(r0 + ri, n0 + gi) for gi in range(grp)] for ri in range(nres)]
            units = [(ri, gi) for ri in range(nres) for gi in range(grp)]
            ks = [[k_ref[rw, :].astype(BF16) for rw in rows] for rows in chains]
            scores = []
            for ri, gi in units:
                q = q_ref[chains[ri][gi + 1], :].astype(BF16)
                scores.append(jnp.concatenate(
                    [lax.dot_general(q, ks[ri][gi], nt, preferred_element_type=F32),
                     lax.dot_general(q, ks[ri][gi + 1], nt, preferred_element_type=F32)], axis=1))
            probs, inv_l = [], []
            for (ri, gi), sc in zip(units, scores):
                rows = chains[ri][gi + 1]
                mask = band & ((kj >= blk) | (n0 > 0)) if gi == 0 else band
                s = jnp.where(mask, sc * scale, -jnp.inf)
                m = jnp.max(s, axis=1, keepdims=True)
                p = jnp.exp(s - m)
                l = jnp.sum(p, axis=1, keepdims=True)
                lse_s[pi, rows, :] = jnp.broadcast_to(m + jnp.log(l), (blk, HEAD_DIM))
                inv_l.append(1.0 / l)
                probs.append(p.astype(BF16))
            vs = [[v_ref[rw, :].astype(BF16) for rw in rows] for rows in chains]
            for (ri, gi), pb, il in zip(units, probs, inv_l):
                o_s[pi, chains[ri][gi + 1], :] = il * (
                    jnp.dot(pb[:, :blk], vs[ri][gi], preferred_element_type=F32)
                    + jnp.dot(pb[:, blk:], vs[ri][gi + 1], preferred_element_type=F32))
            return carry

        lax.fori_loop(0, (dil // nres) * ngrp, group_step, 0, unroll=2)

    npat = len(patterns)

    def merge_step(t, carry):
        rows = pl.ds(pl.multiple_of(t * blk, blk), blk)
        lses = [lse_s[pi, rows, :] for pi in range(npat)]
        mx = functools.reduce(jnp.maximum, lses)
        num = jnp.zeros((blk, HEAD_DIM), F32)
        den = jnp.zeros((blk, HEAD_DIM), F32)
        for pi in range(npat):
            w = jnp.exp(lses[pi] - mx)
            num = num + w * o_s[pi, rows, :]
            den = den + w
        o_ref[rows, :] = (num / den).astype(o_ref.dtype)
        return carry

    lax.fori_loop(0, seq // blk, merge_step, 0)


def _dilated_attention(qkv, seq):
    n = qkv.shape[0]
    nh = qkv.shape[1] // (3 * HEAD_DIM)
    for window, dil in DILATED_PATTERNS:
        assert window // dil <= ATTN_BLOCK and seq % (dil * ATTN_BLOCK) == 0
    blk = (seq, HEAD_DIM)
    return pl.pallas_call(
        functools.partial(_attn_kernel, seq=seq, patterns=DILATED_PATTERNS,
                          scale=np.float32(HEAD_DIM ** -0.5)),
        grid=(n // seq, nh),
        in_specs=[pl.BlockSpec(blk, lambda b, h: (b, h)),
                  pl.BlockSpec(blk, lambda b, h: (b, nh + h)),
                  pl.BlockSpec(blk, lambda b, h: (b, 2 * nh + h))],
        out_specs=pl.BlockSpec(blk, lambda b, h: (b, h)),
        out_shape=jax.ShapeDtypeStruct((n, nh * HEAD_DIM), BF16),
        scratch_shapes=[pltpu.VMEM((len(DILATED_PATTERNS),) + blk, F32)] * 2,
        compiler_params=_cparams("arbitrary", "arbitrary"),
        name="dilated_attn",
    )(qkv, qkv, qkv)


def _top16(s, kid, exact_ties):
    nkeys = s.shape[0]
    vals = []
    if exact_ties:
        rank = jnp.full(s.shape, float(nkeys), F32)
        for it in range(PEER_TOPK):
            m = jnp.max(s, axis=0, keepdims=True)
            idx = jnp.min(jnp.where(s == m, kid, float(nkeys)), axis=0, keepdims=True)
            sel = kid == idx
            rank = jnp.where(sel, float(it), rank)
            s = jnp.where(sel, -jnp.inf, s)
            vals.append(m)
        return rank, jnp.concatenate(vals, axis=0)
    floor = np.float32(-(2.0 ** 125))
    for it in range(PEER_TOPK):
        m = jnp.max(s, axis=0, keepdims=True)
        s = jnp.where(s == m, np.float32(-(2.0 ** 126 + it * 2.0 ** 103)), s)
        vals.append(m)
    step = (-s - np.float32(2.0 ** 126)) * np.float32(2.0 ** -103)
    rank = jnp.where(s <= floor, jnp.maximum(step, 0.0), float(nkeys))
    return rank, jnp.concatenate(vals, axis=0)


def _router_kernel(wq_ref, ht_ref, sk_ref, p1_ref, n_ref, r2_ref, p2_ref, q_s, *, ngroups):
    nk = PEER_NKEYS
    kt = PEER_TOPK
    half = kt // 2
    kid = lax.broadcasted_iota(jnp.int32, (nk, LANES), 0).astype(F32)
    ia = lax.broadcasted_iota(jnp.int32, (kt, LANES), 0).astype(F32)
    ih = ia[0:half]
    cid = jnp.concatenate([ia * kt] + [ih * kt + float(b) for b in range(1, half)] + [ih + float(half)],
                          axis=0)
    ncand = kt * kt
    tail0 = kt + (half - 1) * half

    q_s[...] = jnp.dot(wq_ref[...], ht_ref[...], preferred_element_type=F32)

    def route(lanes, exact_ties):
        q1 = q_s[0:nk, lanes].astype(BF16)
        q2 = q_s[nk:2 * nk, lanes].astype(BF16)
        s1 = jnp.dot(sk_ref[0], q1, preferred_element_type=F32)
        s2 = jnp.dot(sk_ref[1], q2, preferred_element_type=F32)
        rank1, v1 = _top16(s1, kid, exact_ties)
        rank2, v2 = _top16(s2, kid, exact_ties)
        work = jnp.concatenate([v1 + v2[0:1]] + [v1[0:half] + v2[b:b + 1] for b in range(1, half)]
                               + [v1[0:1] + v2[half:kt]], axis=0)
        m0 = v1[0:1] + v2[0:1]
        z = jnp.zeros((1, LANES), F32)
        if exact_ties:
            n_a = jnp.zeros((kt, LANES), F32)
            for _ in range(kt):
                m = jnp.max(work, axis=0, keepdims=True)
                idx = jnp.min(jnp.where(work == m, cid, float(ncand)), axis=0, keepdims=True)
                work = jnp.where(cid == idx, -jnp.inf, work)
                a_sel = jnp.floor(idx * (1.0 / kt))
                n_a = n_a + jnp.where(ia == a_sel, 1.0, 0.0)
                z = z + jnp.exp(m - m0)
        else:
            for _ in range(kt):
                m = jnp.max(work, axis=0, keepdims=True)
                work = jnp.where(work == m, -jnp.inf, work)
                z = z + jnp.exp(m - m0)
            mark = jnp.where(work == -jnp.inf, 1.0, 0.0)
            low = mark[0:half]
            for b in range(1, half):
                low = low + mark[kt + (b - 1) * half:kt + b * half]
            tail = jnp.sum(mark[tail0:], axis=0, keepdims=True)
            n_a = jnp.concatenate([low + jnp.where(ih == 0.0, tail, 0.0), mark[half:kt]], axis=0)
        nkey = jnp.zeros((nk, LANES), F32)
        for a in range(kt):
            nkey = jnp.where(rank1 == float(a), n_a[a:a + 1], nkey)
        p1_ref[0, :, lanes] = jnp.exp(s1 - v1[0:1])
        n_ref[0, :, lanes] = nkey
        r2_ref[0, :, lanes] = rank2.astype(r2_ref.dtype)
        p2_ref[0, :, lanes] = (jnp.exp(s2 - v2[0:1]) / (2.0 * z)).astype(p2_ref.dtype)
        if exact_ties:
            return None
        ranked = lambda r: jnp.sum(jnp.where(r < float(nk), 1.0, 0.0), axis=0, keepdims=True)
        ok = ((ranked(rank1) == float(kt)) & (ranked(rank2) == float(kt))
              & (jnp.sum(n_a, axis=0, keepdims=True) == float(kt)))
        return jnp.min(jnp.where(ok, 1.0, 0.0))

    groups = [slice(g * LANES, (g + 1) * LANES) for g in range(ngroups)]
    all_distinct = [route(lanes, False) for lanes in groups]
    for lanes, ok in zip(groups, all_distinct):
        @pl.when(ok < 0.5)
        def _(lanes=lanes):
            route(lanes, True)


def _router(wq_t, ht, sk, tt=1024):
    d, n = ht.shape
    nh = PEER_HEADS
    nk = PEER_NKEYS
    out = jax.ShapeDtypeStruct((nh, nk, n), F32)
    out_b = jax.ShapeDtypeStruct((nh, nk, n), BF16)
    ospec = pl.BlockSpec((1, nk, tt), lambda i, h: (h, 0, i))
    return pl.pallas_call(
        functools.partial(_router_kernel, ngroups=tt // LANES),
        grid=(n // tt, nh),
        in_specs=[pl.BlockSpec((2 * nk, d), lambda i, h: (h, 0)),
                  pl.BlockSpec((d, tt), lambda i, h: (0, i)),
                  pl.BlockSpec((2, nk, nk), lambda i, h: (0, 0, 0))],
        out_specs=(ospec, ospec, ospec, ospec),
        out_shape=(out, out, out_b, out_b),
        scratch_shapes=[pltpu.VMEM((2 * nk, tt), F32)],
        compiler_params=_cparams("arbitrary", "arbitrary"),
        name="peer_router",
    )(wq_t, ht, sk)


def _peer_kernel(ht_ref, u_ref, vt_ref, p1_ref, n_ref, r2_ref, p2_ref, x_ref, g_ref, *rest,
                 ne1, nh, final_norm):
    fgain_ref = rest[0] if final_norm else None
    o_ref, acc, gbuf = rest[-3:]
    j = pl.program_id(1)
    nk = PEER_NKEYS

    @pl.when(j == 0)
    def _():
        acc[...] = jnp.zeros_like(acc)

    ht = ht_ref[...]
    pair = PEER_KEYS_PER_MATMUL * nk
    for e in range(ne1):
        rows = slice(e * nk, (e + 1) * nk)
        if e % PEER_KEYS_PER_MATMUL == 0:
            acts = jnp.dot(u_ref[e * nk:e * nk + pair, :], ht, preferred_element_type=F32)
        off = (e % PEER_KEYS_PER_MATMUL) * nk
        act = _gelu_x2(acts[off:off + nk, :])
        w = jnp.zeros(act.shape, BF16)
        for h in range(nh):
            p1 = jnp.broadcast_to(p1_ref[h, e:e + 1, :], act.shape).astype(BF16)
            cnt = jnp.broadcast_to(n_ref[h, e:e + 1, :], act.shape).astype(BF16)
            w = w + p1 * jnp.where(r2_ref[h] < cnt, p2_ref[h], jnp.zeros((), BF16))
        gbuf[rows, :] = act.astype(BF16) * w
    acc[...] += jnp.dot(vt_ref[...], gbuf[...], preferred_element_type=F32)

    @pl.when(j == pl.num_programs(1) - 1)
    def _():
        xn = x_ref[...] + g_ref[0] * acc[...].T
        o_ref[...] = _rmsnorm(xn, fgain_ref[...]) if final_norm else xn


def _peer(ht, u_tab, vt_tab, layer, p1, cnt, r2, p2, x2d, g, seq, final_gain=None, tt=512, eb=1024):
    d, n = ht.shape
    ne = u_tab.shape[1]
    nh, nk, _ = p1.shape
    ne1 = eb // nk
    per_batch = seq // tt
    tok = pl.BlockSpec((nh, nk, tt), lambda i, j: (0, 0, i))
    e1b = pl.BlockSpec((nh, ne1, tt), lambda i, j: (0, j, i))
    final_norm = final_gain is not None
    extra_specs = [pl.BlockSpec((1, d), lambda i, j: (0, 0))] if final_norm else []
    extra_args = [final_gain.reshape(1, d)] if final_norm else []
    return pl.pallas_call(
        functools.partial(_peer_kernel, ne1=ne1, nh=nh, final_norm=final_norm),
        grid=(n // tt, ne // eb),
        in_specs=[
            pl.BlockSpec((d, tt), lambda i, j: (0, i)),
            pl.BlockSpec((None, eb, d), lambda i, j: (layer, j, 0)),
            pl.BlockSpec((None, d, eb), lambda i, j: (layer, 0, j)),
            e1b, e1b, tok, tok,
            pl.BlockSpec((tt, d), lambda i, j: (i, 0)),
            pl.BlockSpec((1, 1, d), lambda i, j: (i // per_batch, 0, 0)),
        ] + extra_specs,
        out_specs=pl.BlockSpec((tt, d), lambda i, j: (i, 0)),
        out_shape=jax.ShapeDtypeStruct((n, d), F32),
        scratch_shapes=[pltpu.VMEM((d, tt), F32), pltpu.VMEM((eb, tt), BF16)],
        compiler_params=_cparams("arbitrary", "arbitrary", vmem=PEER_VMEM_LIMIT),
        name="peer_experts",
    )(ht, u_tab, vt_tab, p1, cnt, r2, p2, x2d, g, *extra_args)


def kernel(x, c, w_ada, b_ada, norm_mix, norm_ffn, w_in, ssm_lambda_re, ssm_lambda_im, ssm_log_dt, ssm_b_re, ssm_b_im, ssm_c_re, ssm_c_im, ssm_d, w_glu, w_br_ssm, w_br_attn, w_out, peer_wq, peer_subkeys, peer_u, peer_v, norm_final):
    bsz, seq, d = x.shape
    depth = w_ada.shape[0]
    n = bsz * seq
    ssm_w = ssm_d.shape[1]
    attn_w = w_br_attn.shape[1]
    assert bsz <= SUBLANES and seq % S5_CHUNK == 0

    c_pad = jnp.pad(c, ((0, SUBLANES - bsz), (0, 0)))
    mod_all = _modulation(c_pad, w_ada, b_ada)
    x2d = x.reshape(n, d)
    ts = S5_CHUNK // SUBLANES
    w_in_all = w_in.astype(BF16)
    u_all = peer_u.astype(BF16)
    vt_all = jnp.swapaxes(peer_v, 1, 2).astype(BF16)

    for l in range(depth):
        mods = [mod_all[l, :bsz, i * d:(i + 1) * d].reshape(bsz, 1, d) for i in range(N_MOD)]
        sh_m, sc_m, g_m, sh_f, sc_f, g_f = mods

        h = _norm_mod(x2d, norm_mix[l], sc_m, sh_m, seq)
        u, qkv, gates = _in_proj(h, w_in_all, l, ssm_w, 3 * attn_w)

        a_re, a_im, bbar_re, bbar_im, at_re, at_im = _s5_prep(
            ssm_lambda_re[l], ssm_lambda_im[l], ssm_log_dt[l], ssm_b_re[l], ssm_b_im[l], ts)
        bblk, cblk = _s5_block_weights(bbar_re, bbar_im, ssm_c_re[l], ssm_c_im[l])
        y_s = _s5_branch(u, seq, a_re, a_im, at_re, at_im, bblk, cblk, ssm_d[l],
                         w_glu[l].astype(BF16))
        y_a = _dilated_attention(qkv, seq)
        merged = _merge(y_s, y_a, w_br_ssm[l].astype(BF16), w_br_attn[l].astype(BF16), gates)
        x2d, ht = _outproj_norm(merged, w_out[l].astype(BF16), x2d, g_m, norm_ffn[l], sc_f, sh_f, seq)
        p1, cnt, r2, p2 = _router(peer_wq[l].T.astype(BF16), ht, peer_subkeys[l].astype(BF16))
        x2d = _peer(ht, u_all, vt_all, l, p1, cnt, r2, p2, x2d, g_f, seq,
                    final_gain=norm_final if l == depth - 1 else None)

    return x2d.reshape(bsz, seq, d)
```

```python
import functools
import math

import jax
import jax.numpy as jnp
import numpy as np
from jax import lax
from jax.experimental import pallas as pl
from jax.experimental.pallas import tpu as pltpu

F32 = jnp.float32
BF16 = jnp.bfloat16

EPS = 1e-6
N_MOD = 6
SSM_GROUP = 16
SSM_STATE = 64
HEAD_DIM = 128
ATTN_BLOCK = 128
ATTN_GROUP = 8
DILATED_PATTERNS = ((128, 1), (512, 4), (2048, 16))
PEER_HEADS = 8
PEER_NKEYS = 128
PEER_TOPK = 16

LANES = 128
SUBLANES = 8
S5_CHUNK = 256
S5_LANE_BLOCK = 512
S5_CH_BLOCK = 128
VMEM_LIMIT = 56 * 1024 * 1024
PEER_VMEM_LIMIT = 60 * 1024 * 1024


def _cparams(*sem, vmem=VMEM_LIMIT):
    return pltpu.CompilerParams(dimension_semantics=sem, vmem_limit_bytes=vmem)


def _gelu_x2(x):
    return x * (1.0 + lax.erf(x * np.float32(math.sqrt(0.5))))


def _gelu(x):
    return 0.5 * _gelu_x2(x)


def _split3(x):
    hi = x.astype(BF16)
    r1 = x - hi.astype(F32)
    mid = r1.astype(BF16)
    lo = (r1 - mid.astype(F32)).astype(BF16)
    return hi, mid, lo


def _mod_kernel(c_ref, w_ref, b_ref, o_ref):
    c = c_ref[...]
    ca = c * jax.nn.sigmoid(c)
    w = w_ref[0]
    c_hi = ca.astype(BF16)
    c_lo = (ca - c_hi.astype(F32)).astype(BF16)
    w_hi = w.astype(BF16)
    w_lo = (w - w_hi.astype(F32)).astype(BF16)
    acc = jnp.dot(c_hi, w_hi, preferred_element_type=F32)
    acc += jnp.dot(c_lo, w_hi, preferred_element_type=F32)
    acc += jnp.dot(c_hi, w_lo, preferred_element_type=F32)
    o_ref[0] = acc + b_ref[0]


def _modulation(c_pad, w_ada, b_ada, tn=1024):
    depth, d, cols = w_ada.shape
    return pl.pallas_call(
        _mod_kernel,
        grid=(depth, cols // tn),
        in_specs=[
            pl.BlockSpec((SUBLANES, d), lambda l, j: (0, 0)),
            pl.BlockSpec((1, d, tn), lambda l, j: (l, 0, j)),
            pl.BlockSpec((1, 1, tn), lambda l, j: (l, 0, j)),
        ],
        out_specs=pl.BlockSpec((1, SUBLANES, tn), lambda l, j: (l, 0, j)),
        out_shape=jax.ShapeDtypeStruct((depth, SUBLANES, cols), F32),
        compiler_params=_cparams("arbitrary", "arbitrary"),
        name="adaln_mod",
    )(c_pad, w_ada, b_ada.reshape(depth, 1, cols))


def _rmsnorm(x, gain):
    ms = jnp.mean(x * x, axis=-1, keepdims=True)
    return x * lax.rsqrt(ms + EPS) * gain


def _norm_mod_kernel(x_ref, gain_ref, sc_ref, sh_ref, o_ref):
    h = _rmsnorm(x_ref[...], gain_ref[...]) * (1.0 + sc_ref[0]) + sh_ref[0]
    o_ref[...] = h.astype(o_ref.dtype)


def _norm_mod(x2d, gain, sc, sh, seq, tm=512):
    n, d = x2d.shape
    per_batch = seq // tm
    mod = pl.BlockSpec((1, 1, d), lambda i: (i // per_batch, 0, 0))
    return pl.pallas_call(
        _norm_mod_kernel,
        grid=(n // tm,),
        in_specs=[pl.BlockSpec((tm, d), lambda i: (i, 0)),
                  pl.BlockSpec((1, d), lambda i: (0, 0)),
                  mod, mod],
        out_specs=pl.BlockSpec((tm, d), lambda i: (i, 0)),
        out_shape=jax.ShapeDtypeStruct((n, d), BF16),
        compiler_params=_cparams("arbitrary"),
        name="norm_mod",
    )(x2d, gain.reshape(1, d), sc, sh)


def _in_proj_kernel(h_ref, w_ref, u_ref, qkv_ref, gates_ref, *, nu, nqkv):
    j = pl.program_id(1)

    def proj():
        return jnp.dot(h_ref[...], w_ref[...], preferred_element_type=F32)

    @pl.when(j < nu)
    def _():
        u_ref[...] = proj()

    @pl.when((j >= nu) & (j < nu + nqkv))
    def _():
        qkv_ref[...] = proj()

    @pl.when(j >= nu + nqkv)
    def _():
        gates_ref[...] = jax.nn.sigmoid(proj()).astype(gates_ref.dtype)


def _in_proj(h, w_all, layer, u_cols, qkv_cols, tm=1024, tn=1024):
    n, d = h.shape
    cols = w_all.shape[2]
    nu, nqkv = u_cols // tn, qkv_cols // tn
    ng = cols // tn - nu - nqkv
    return pl.pallas_call(
        functools.partial(_in_proj_kernel, nu=nu, nqkv=nqkv),
        grid=(n // tm, cols // tn),
        in_specs=[
            pl.BlockSpec((tm, d), lambda i, j: (i, 0)),
            pl.BlockSpec((None, d, tn), lambda i, j: (layer, 0, j)),
        ],
        out_specs=(
            pl.BlockSpec((tm, tn), lambda i, j: (i, jnp.minimum(j, nu - 1))),
            pl.BlockSpec((tm, tn), lambda i, j: (i, jnp.clip(j - nu, 0, nqkv - 1))),
            pl.BlockSpec((tm, tn), lambda i, j: (i, jnp.clip(j - nu - nqkv, 0, ng - 1))),
        ),
        out_shape=(jax.ShapeDtypeStruct((n, u_cols), F32),
                   jax.ShapeDtypeStruct((n, qkv_cols), F32),
                   jax.ShapeDtypeStruct((n, ng * tn), BF16)),
        compiler_params=_cparams("arbitrary", "arbitrary"),
        name="in_proj",
    )(h, w_all)


def _merge_kernel(ys_ref, ya_ref, ws_ref, wa_ref, gs_ref, ga_ref, o_ref):
    ps = jnp.dot(ys_ref[...], ws_ref[...], preferred_element_type=F32)
    pa = jnp.dot(ya_ref[...], wa_ref[...], preferred_element_type=F32)
    o_ref[...] = (gs_ref[...].astype(F32) * ps
                  + ga_ref[...].astype(F32) * pa).astype(o_ref.dtype)


def _merge(ys, ya, ws, wa, gates, tm=1024, tn=1024):
    n, ks = ys.shape
    ka = ya.shape[1]
    d = ws.shape[1]
    goff = d // tn
    return pl.pallas_call(
        _merge_kernel,
        grid=(n // tm, d // tn),
        in_specs=[
            pl.BlockSpec((tm, ks), lambda i, j: (i, 0)),
            pl.BlockSpec((tm, ka), lambda i, j: (i, 0)),
            pl.BlockSpec((ks, tn), lambda i, j: (0, j)),
            pl.BlockSpec((ka, tn), lambda i, j: (0, j)),
            pl.BlockSpec((tm, tn), lambda i, j: (i, j)),
            pl.BlockSpec((tm, tn), lambda i, j: (i, j + goff)),
        ],
        out_specs=pl.BlockSpec((tm, tn), lambda i, j: (i, j)),
        out_shape=jax.ShapeDtypeStruct((n, d), BF16),
        compiler_params=_cparams("arbitrary", "arbitrary"),
        name="branch_merge",
    )(ys, ya, ws, wa, gates, gates)


def _outproj_norm_kernel(a_ref, w_ref, x_ref, g_ref, gain_ref, sc_ref, sh_ref, xo_ref, ht_ref):
    acc = jnp.dot(a_ref[...], w_ref[...], preferred_element_type=F32)
    xn = x_ref[...] + g_ref[0] * acc
    xo_ref[...] = xn
    h = _rmsnorm(xn, gain_ref[...]) * (1.0 + sc_ref[0]) + sh_ref[0]
    ht_ref[...] = h.T.astype(ht_ref.dtype)


def _outproj_norm(a, w, x2d, g, gain, sc, sh, seq, tm=512):
    n, k = a.shape
    d = w.shape[1]
    per_batch = seq // tm
    mod = pl.BlockSpec((1, 1, d), lambda i: (i // per_batch, 0, 0))
    return pl.pallas_call(
        _outproj_norm_kernel,
        grid=(n // tm,),
        in_specs=[
            pl.BlockSpec((tm, k), lambda i: (i, 0)),
            pl.BlockSpec((k, d), lambda i: (0, 0)),
            pl.BlockSpec((tm, d), lambda i: (i, 0)),
            mod,
            pl.BlockSpec((1, d), lambda i: (0, 0)),
            mod, mod,
        ],
        out_specs=(pl.BlockSpec((tm, d), lambda i: (i, 0)),
                   pl.BlockSpec((d, tm), lambda i: (0, i))),
        out_shape=(jax.ShapeDtypeStruct((n, d), F32), jax.ShapeDtypeStruct((d, n), BF16)),
        compiler_params=_cparams("arbitrary"),
        name="out_proj_norm",
    )(a, w, x2d, g, gain.reshape(1, d), sc, sh)


def _s5_prep_kernel(lr_ref, li_ref, ldt_ref, br_ref, bi_ref,
                    are_ref, aim_ref, bbr_ref, bbi_ref, atr_ref, ati_ref, *, ts):
    lr, li = lr_ref[...], li_ref[...]
    dt = jnp.exp(ldt_ref[...])
    mag = jnp.exp(lr * dt)
    a_re = mag * jnp.cos(li * dt)
    a_im = mag * jnp.sin(li * dt)
    inv = 1.0 / (lr * lr + li * li)
    coef_re = ((a_re - 1.0) * lr + a_im * li) * inv
    coef_im = (a_im * lr - (a_re - 1.0) * li) * inv
    br, bi = br_ref[...], bi_ref[...]
    bbr_ref[...] = coef_re * br - coef_im * bi
    bbi_ref[...] = coef_re * bi + coef_im * br
    are_ref[...] = a_re
    aim_ref[...] = a_im
    pr, pi = a_re, a_im
    for _ in range(ts - 1):
        pr, pi = pr * a_re - pi * a_im, pr * a_im + pi * a_re
    atr_ref[...] = pr
    ati_ref[...] = pi


def _s5_prep(lam_re, lam_im, log_dt, b_re, b_im, ts):
    g, p, c = b_re.shape
    ns = g * p
    row = lambda t: t.reshape(1, ns)
    ldt = jnp.broadcast_to(log_dt[:, None], (g, p))
    bt = lambda t: jnp.transpose(t, (2, 0, 1)).reshape(c, ns)
    shapes = (
        jax.ShapeDtypeStruct((1, ns), F32), jax.ShapeDtypeStruct((1, ns), F32),
        jax.ShapeDtypeStruct((c, ns), F32), jax.ShapeDtypeStruct((c, ns), F32),
        jax.ShapeDtypeStruct((1, ns), F32), jax.ShapeDtypeStruct((1, ns), F32),
    )
    return pl.pallas_call(
        functools.partial(_s5_prep_kernel, ts=ts),
        out_shape=shapes,
        name="s5_prep",
    )(row(lam_re), row(lam_im), row(ldt), bt(b_re), bt(b_im))


def _s5_block_weights(bbar_re, bbar_im, c_re, c_im):
    c, ns = bbar_re.shape
    gpb = S5_CH_BLOCK // c
    nk = ns // (gpb * SSM_STATE)
    eye = jnp.eye(gpb, dtype=F32)

    def bpart(t):
        t = t.reshape(c, nk, gpb, SSM_STATE)
        t = jnp.einsum("ckjp,ij->kicjp", t, eye)
        return t.reshape(nk, gpb * c, gpb * SSM_STATE)

    def cpart(t):
        t = t.reshape(nk, gpb, c, SSM_STATE)
        t = jnp.einsum("kicp,ij->kjpic", t, eye)
        return t.reshape(nk, gpb * SSM_STATE, gpb * c)

    bblk = jnp.concatenate([bpart(bbar_re), bpart(bbar_im)], axis=2).astype(BF16)
    cblk = jnp.concatenate([cpart(c_re), cpart(-c_im)], axis=1).astype(BF16)
    return bblk, cblk


def _s5_kernel(u_ref, pm_ref, pmt_ref, bblk_ref, are_ref, aim_ref, atr_ref, ati_ref,
               cblk_ref, d_ref, wglu_ref, o_ref,
               hr, hi, car_r, car_i, cs_r, cs_i, yp, *, ts, nk):
    lw = S5_LANE_BLOCK
    cw = S5_CH_BLOCK

    @pl.when(pl.program_id(1) == 0)
    def _():
        car_r[...] = jnp.zeros_like(car_r)
        car_i[...] = jnp.zeros_like(car_i)

    u = u_ref[...]
    up = jnp.dot(pm_ref[...], u.astype(BF16), preferred_element_type=F32).astype(BF16)
    for k in range(nk):
        bu = jnp.dot(up[:, k * cw:(k + 1) * cw], bblk_ref[k], preferred_element_type=F32)
        hr[:, k * lw:(k + 1) * lw] = bu[:, :lw]
        hi[:, k * lw:(k + 1) * lw] = bu[:, lw:]

    for k in range(nk):
        sl = pl.ds(k * lw, lw)
        ar = are_ref[:, sl]
        ai = aim_ref[:, sl]

        def end_step(j, carry, sl=sl, ar=ar, ai=ai):
            cr, cim = carry
            rows = pl.ds(pl.multiple_of(j * SUBLANES, SUBLANES), SUBLANES)
            return ar * cr - ai * cim + hr[rows, sl], ar * cim + ai * cr + hi[rows, sl]

        zero = jnp.zeros((SUBLANES, lw), F32)
        end_r, end_i = lax.fori_loop(0, ts, end_step, (zero, zero), unroll=True)
        cs_r[:, sl] = end_r
        cs_i[:, sl] = end_i

    c_r = car_r[...]
    c_i = car_i[...]
    at_r = atr_ref[...]
    at_i = ati_ref[...]
    for s in range(SUBLANES):
        l_r = cs_r[s:s + 1, :]
        l_i = cs_i[s:s + 1, :]
        cs_r[s:s + 1, :] = c_r
        cs_i[s:s + 1, :] = c_i
        c_r, c_i = l_r + at_r * c_r - at_i * c_i, l_i + at_r * c_i + at_i * c_r
    car_r[...] = c_r
    car_i[...] = c_i

    for k in range(nk):
        sl = pl.ds(k * lw, lw)
        ar = are_ref[:, sl]
        ai = aim_ref[:, sl]

        def scan_step(j, carry, sl=sl, ar=ar, ai=ai):
            cr, cim = carry
            rows = pl.ds(pl.multiple_of(j * SUBLANES, SUBLANES), SUBLANES)
            nr = ar * cr - ai * cim + hr[rows, sl]
            ni = ar * cim + ai * cr + hi[rows, sl]
            hr[rows, sl] = nr
            hi[rows, sl] = ni
            return nr, ni

        lax.fori_loop(0, ts, scan_step, (cs_r[:, sl], cs_i[:, sl]), unroll=True)
        hcat = jnp.concatenate([hr[:, sl], hi[:, sl]], axis=1).astype(BF16)
        yp[:, k * cw:(k + 1) * cw] = jnp.dot(hcat, cblk_ref[k], preferred_element_type=F32)

    y1, y2, y3 = _split3(yp[...])
    pmt = pmt_ref[...]
    y = (jnp.dot(pmt, y1, preferred_element_type=F32)
         + jnp.dot(pmt, y2, preferred_element_type=F32)
         + jnp.dot(pmt, y3, preferred_element_type=F32))
    y = y + d_ref[...] * u
    y = _gelu(y)
    gate = jax.nn.sigmoid(jnp.dot(y.astype(BF16), wglu_ref[...], preferred_element_type=F32))
    o_ref[...] = (y * gate).astype(o_ref.dtype)


def _s5_permutation(t_rows, ts):
    r = np.arange(t_rows)
    src = (r % SUBLANES) * ts + r // SUBLANES
    pm = np.zeros((t_rows, t_rows), np.float32)
    pm[r, src] = 1.0
    return jnp.asarray(pm, BF16), jnp.asarray(pm.T, BF16)


def _s5_branch(u, seq, a_re, a_im, at_re, at_im, bblk, cblk, d_skip, w_glu):
    n, w = u.shape
    ns = a_re.shape[1]
    t_rows = S5_CHUNK
    ts = t_rows // SUBLANES
    nk = ns // S5_LANE_BLOCK
    nchunk = seq // t_rows
    pm, pmt = _s5_permutation(t_rows, ts)
    full = lambda shape: pl.BlockSpec(shape, lambda b, c: (0,) * len(shape))
    rep8 = lambda row: jnp.broadcast_to(row, (SUBLANES, ns))
    return pl.pallas_call(
        functools.partial(_s5_kernel, ts=ts, nk=nk),
        grid=(n // seq, nchunk),
        in_specs=[
            pl.BlockSpec((t_rows, w), lambda b, c: (b * nchunk + c, 0)),
            full((t_rows, t_rows)), full((t_rows, t_rows)),
            full(bblk.shape),
            full((SUBLANES, ns)), full((SUBLANES, ns)), full((1, ns)), full((1, ns)),
            full(cblk.shape),
            full((1, w)), full((w, w)),
        ],
        out_specs=pl.BlockSpec((t_rows, w), lambda b, c: (b * nchunk + c, 0)),
        out_shape=jax.ShapeDtypeStruct((n, w), BF16),
        scratch_shapes=[
            pltpu.VMEM((t_rows, ns), F32), pltpu.VMEM((t_rows, ns), F32),
            pltpu.VMEM((1, ns), F32), pltpu.VMEM((1, ns), F32),
            pltpu.VMEM((SUBLANES, ns), F32), pltpu.VMEM((SUBLANES, ns), F32),
            pltpu.VMEM((t_rows, w), F32),
        ],
        compiler_params=_cparams("arbitrary", "arbitrary"),
        name="s5_branch",
    )(u, pm, pmt, bblk, rep8(a_re), rep8(a_im), at_re, at_im, cblk, d_skip.reshape(1, w), w_glu)


def _attn_kernel(q_ref, k_ref, v_ref, o_ref, o_s, lse_s, *, seq, patterns, scale):
    blk = ATTN_BLOCK
    qi = lax.broadcasted_iota(jnp.int32, (blk, 2 * blk), 0)
    kj = lax.broadcasted_iota(jnp.int32, (blk, 2 * blk), 1)
    dist = blk + qi - kj
    nt = (((1,), (1,)), ((), ()))

    for pi, (window, dil) in enumerate(patterns):
        n_back = window // dil
        nblk = seq // dil // blk
        grp = min(ATTN_GROUP, nblk)
        ngrp = nblk // grp
        band = (dist >= 0) & (dist <= n_back)

        nres = max(1, ATTN_GROUP // grp)

        def group_step(idx, carry, pi=pi, dil=dil, grp=grp, ngrp=ngrp, nres=nres, band=band):
            r0 = (idx // ngrp) * nres
            n0 = (idx % ngrp) * grp

            def rows_of(r, nb):
                start = r + nb * (blk * dil)
                return pl.ds(start, blk) if dil == 1 else pl.ds(start, blk, stride=dil)

            chains = [[rows_of(r0 + ri, jnp.maximum(n0 - 1, 0))]
                      + [rows_of(r0 + ri, n0 + gi) for gi in range(grp)] for ri in range(nres)]
            units = [(ri, gi) for ri in range(nres) for gi in range(grp)]
            ks = [[k_ref[rw, :].astype(BF16) for rw in rows] for rows in chains]
            scores = []
            for ri, gi in units:
                q = q_ref[chains[ri][gi + 1], :].astype(BF16)
                scores.append(jnp.concatenate(
                    [lax.dot_general(q, ks[ri][gi], nt, preferred_element_type=F32),
                     lax.dot_general(q, ks[ri][gi + 1], nt, preferred_element_type=F32)], axis=1))
            probs, inv_l = [], []
            for (ri, gi), sc in zip(units, scores):
                rows = chains[ri][gi + 1]
                mask = band & ((kj >= blk) | (n0 > 0)) if gi == 0 else band
                s = jnp.where(mask, sc * scale, -jnp.inf)
                m = jnp.max(s, axis=1, keepdims=True)
                p = jnp.exp(s - m)
                l = jnp.sum(p, axis=1, keepdims=True)
                lse_s[pi, rows, :] = jnp.broadcast_to(m + jnp.log(l), (blk, HEAD_DIM))
                inv_l.append(1.0 / l)
                probs.append(p.astype(BF16))
            vs = [[v_ref[rw, :].astype(BF16) for rw in rows] for rows in chains]
            for (ri, gi), pb, il in zip(units, probs, inv_l):
                o_s[pi, chains[ri][gi + 1], :] = il * (
                    jnp.dot(pb[:, :blk], vs[ri][gi], preferred_element_type=F32)
                    + jnp.dot(pb[:, blk:], vs[ri][gi + 1], preferred_element_type=F32))
            return carry

        lax.fori_loop(0, (dil // nres) * ngrp, group_step, 0, unroll=2)

    npat = len(patterns)

    def merge_step(t, carry):
        rows = pl.ds(pl.multiple_of(t * blk, blk), blk)
        lses = [lse_s[pi, rows, :] for pi in range(npat)]
        mx = functools.reduce(jnp.maximum, lses)
        num = jnp.zeros((blk, HEAD_DIM), F32)
        den = jnp.zeros((blk, HEAD_DIM), F32)
        for pi in range(npat):
            w = jnp.exp(lses[pi] - mx)
            num = num + w * o_s[pi, rows, :]
            den = den + w
        o_ref[rows, :] = (num / den).astype(o_ref.dtype)
        return carry

    lax.fori_loop(0, seq // blk, merge_step, 0)


def _dilated_attention(qkv, seq):
    n = qkv.shape[0]
    nh = qkv.shape[1] // (3 * HEAD_DIM)
    for window, dil in DILATED_PATTERNS:
        assert window // dil <= ATTN_BLOCK and seq % (dil * ATTN_BLOCK) == 0
    blk = (seq, HEAD_DIM)
    return pl.pallas_call(
        functools.partial(_attn_kernel, seq=seq, patterns=DILATED_PATTERNS,
                          scale=np.float32(HEAD_DIM ** -0.5)),
        grid=(n // seq, nh),
        in_specs=[pl.BlockSpec(blk, lambda b, h: (b, h)),
                  pl.BlockSpec(blk, lambda b, h: (b, nh + h)),
                  pl.BlockSpec(blk, lambda b, h: (b, 2 * nh + h))],
        out_specs=pl.BlockSpec(blk, lambda b, h: (b, h)),
        out_shape=jax.ShapeDtypeStruct((n, nh * HEAD_DIM), BF16),
        scratch_shapes=[pltpu.VMEM((len(DILATED_PATTERNS),) + blk, F32)] * 2,
        compiler_params=_cparams("arbitrary", "arbitrary"),
        name="dilated_attn",
    )(qkv, qkv, qkv)


def _top16(s, kid, exact_ties):
    nkeys = s.shape[0]
    vals = []
    if exact_ties:
        rank = jnp.full(s.shape, float(nkeys), F32)
        for it in range(PEER_TOPK):
            m = jnp.max(s, axis=0, keepdims=True)
            idx = jnp.min(jnp.where(s == m, kid, float(nkeys)), axis=0, keepdims=True)
            sel = kid == idx
            rank = jnp.where(sel, float(it), rank)
            s = jnp.where(sel, -jnp.inf, s)
            vals.append(m)
        return rank, jnp.concatenate(vals, axis=0)
    floor = np.float32(-(2.0 ** 125))
    for it in range(PEER_TOPK):
        m = jnp.max(s, axis=0, keepdims=True)
        s = jnp.where(s == m, np.float32(-(2.0 ** 126 + it * 2.0 ** 103)), s)
        vals.append(m)
    step = (-s - np.float32(2.0 ** 126)) * np.float32(2.0 ** -103)
    rank = jnp.where(s <= floor, jnp.maximum(step, 0.0), float(nkeys))
    return rank, jnp.concatenate(vals, axis=0)


def _router_kernel(wq_ref, ht_ref, sk_ref, p1_ref, n_ref, r2_ref, p2_ref, q_s, *, ngroups):
    nk = PEER_NKEYS
    kt = PEER_TOPK
    half = kt // 2
    kid = lax.broadcasted_iota(jnp.int32, (nk, LANES), 0).astype(F32)
    ia = lax.broadcasted_iota(jnp.int32, (kt, LANES), 0).astype(F32)
    ih = ia[0:half]
    cid = jnp.concatenate([ia * kt] + [ih * kt + float(b) for b in range(1, half)] + [ih + float(half)],
                          axis=0)
    ncand = kt * kt
    tail0 = kt + (half - 1) * half

    q_s[...] = jnp.dot(wq_ref[...], ht_ref[...], preferred_element_type=F32)

    def route(lanes, exact_ties):
        q1 = q_s[0:nk, lanes].astype(BF16)
        q2 = q_s[nk:2 * nk, lanes].astype(BF16)
        s1 = jnp.dot(sk_ref[0], q1, preferred_element_type=F32)
        s2 = jnp.dot(sk_ref[1], q2, preferred_element_type=F32)
        rank1, v1 = _top16(s1, kid, exact_ties)
        rank2, v2 = _top16(s2, kid, exact_ties)
        work = jnp.concatenate([v1 + v2[0:1]] + [v1[0:half] + v2[b:b + 1] for b in range(1, half)]
                               + [v1[0:1] + v2[half:kt]], axis=0)
        m0 = v1[0:1] + v2[0:1]
        z = jnp.zeros((1, LANES), F32)
        if exact_ties:
            n_a = jnp.zeros((kt, LANES), F32)
            for _ in range(kt):
                m = jnp.max(work, axis=0, keepdims=True)
                idx = jnp.min(jnp.where(work == m, cid, float(ncand)), axis=0, keepdims=True)
                work = jnp.where(cid == idx, -jnp.inf, work)
                a_sel = jnp.floor(idx * (1.0 / kt))
                n_a = n_a + jnp.where(ia == a_sel, 1.0, 0.0)
                z = z + jnp.exp(m - m0)
        else:
            for _ in range(kt):
                m = jnp.max(work, axis=0, keepdims=True)
                work = jnp.where(work == m, -jnp.inf, work)
                z = z + jnp.exp(m - m0)
            mark = jnp.where(work == -jnp.inf, 1.0, 0.0)
            low = mark[0:half]
            for b in range(1, half):
                low = low + mark[kt + (b - 1) * half:kt + b * half]
            tail = jnp.sum(mark[tail0:], axis=0, keepdims=True)
            n_a = jnp.concatenate([low + jnp.where(ih == 0.0, tail, 0.0), mark[half:kt]], axis=0)
        nkey = jnp.zeros((nk, LANES), F32)
        for a in range(kt):
            nkey = jnp.where(rank1 == float(a), n_a[a:a + 1], nkey)
        p1_ref[0, :, lanes] = jnp.exp(s1 - v1[0:1])
        n_ref[0, :, lanes] = nkey
        r2_ref[0, :, lanes] = rank2.astype(r2_ref.dtype)
        p2_ref[0, :, lanes] = (jnp.exp(s2 - v2[0:1]) / (2.0 * z)).astype(p2_ref.dtype)
        if exact_ties:
            return None
        ranked = lambda r: jnp.sum(jnp.where(r < float(nk), 1.0, 0.0), axis=0, keepdims=True)
        ok = ((ranked(rank1) == float(kt)) & (ranked(rank2) == float(kt))
              & (jnp.sum(n_a, axis=0, keepdims=True) == float(kt)))
        return jnp.min(jnp.where(ok, 1.0, 0.0))

    groups = [slice(g * LANES, (g + 1) * LANES) for g in range(ngroups)]
    all_distinct = [route(lanes, False) for lanes in groups]
    for lanes, ok in zip(groups, all_distinct):
        @pl.when(ok < 0.5)
        def _(lanes=lanes):
            route(lanes, True)


def _router(wq_t, ht, sk, tt=2048):
    d, n = ht.shape
    nh = PEER_HEADS
    nk = PEER_NKEYS
    out = jax.ShapeDtypeStruct((nh, nk, n), F32)
    out_b = jax.ShapeDtypeStruct((nh, nk, n), BF16)
    ospec = pl.BlockSpec((1, nk, tt), lambda i, h: (h, 0, i))
    return pl.pallas_call(
        functools.partial(_router_kernel, ngroups=tt // LANES),
        grid=(n // tt, nh),
        in_specs=[pl.BlockSpec((2 * nk, d), lambda i, h: (h, 0)),
                  pl.BlockSpec((d, tt), lambda i, h: (0, i)),
                  pl.BlockSpec((2, nk, nk), lambda i, h: (0, 0, 0))],
        out_specs=(ospec, ospec, ospec, ospec),
        out_shape=(out, out, out_b, out_b),
        scratch_shapes=[pltpu.VMEM((2 * nk, tt), F32)],
        compiler_params=_cparams("arbitrary", "arbitrary"),
        name="peer_router",
    )(wq_t, ht, sk)


def _peer_kernel(ht_ref, u_ref, vt_ref, p1_ref, n_ref, r2_ref, p2_ref, x_ref, g_ref, *rest,
                 ne1, nh, final_norm):
    fgain_ref = rest[0] if final_norm else None
    o_ref, acc, gbuf = rest[-3:]
    j = pl.program_id(1)
    nk = PEER_NKEYS

    @pl.when(j == 0)
    def _():
        acc[...] = jnp.zeros_like(acc)

    ht = ht_ref[...]
    for e in range(ne1):
        rows = slice(e * nk, (e + 1) * nk)
        act = _gelu_x2(jnp.dot(u_ref[rows, :], ht, preferred_element_type=F32))
        w = jnp.zeros(act.shape, BF16)
        for h in range(nh):
            p1 = jnp.broadcast_to(p1_ref[h, e:e + 1, :], act.shape).astype(BF16)
            cnt = jnp.broadcast_to(n_ref[h, e:e + 1, :], act.shape).astype(BF16)
            w = w + p1 * jnp.where(r2_ref[h] < cnt, p2_ref[h], jnp.zeros((), BF16))
        gbuf[rows, :] = act.astype(BF16) * w
    acc[...] += jnp.dot(vt_ref[...], gbuf[...], preferred_element_type=F32)

    @pl.when(j == pl.num_programs(1) - 1)
    def _():
        xn = x_ref[...] + g_ref[0] * acc[...].T
        o_ref[...] = _rmsnorm(xn, fgain_ref[...]) if final_norm else xn


def _peer(ht, u_tab, vt_tab, layer, p1, cnt, r2, p2, x2d, g, seq, final_gain=None, tt=512, eb=1024):
    d, n = ht.shape
    ne = u_tab.shape[1]
    nh, nk, _ = p1.shape
    ne1 = eb // nk
    per_batch = seq // tt
    tok = pl.BlockSpec((nh, nk, tt), lambda i, j: (0, 0, i))
    e1b = pl.BlockSpec((nh, ne1, tt), lambda i, j: (0, j, i))
    final_norm = final_gain is not None
    extra_specs = [pl.BlockSpec((1, d), lambda i, j: (0, 0))] if final_norm else []
    extra_args = [final_gain.reshape(1, d)] if final_norm else []
    return pl.pallas_call(
        functools.partial(_peer_kernel, ne1=ne1, nh=nh, final_norm=final_norm),
        grid=(n // tt, ne // eb),
        in_specs=[
            pl.BlockSpec((d, tt), lambda i, j: (0, i)),
            pl.BlockSpec((None, eb, d), lambda i, j: (layer, j, 0)),
            pl.BlockSpec((None, d, eb), lambda i, j: (layer, 0, j)),
            e1b, e1b, tok, tok,
            pl.BlockSpec((tt, d), lambda i, j: (i, 0)),
            pl.BlockSpec((1, 1, d), lambda i, j: (i // per_batch, 0, 0)),
        ] + extra_specs,
        out_specs=pl.BlockSpec((tt, d), lambda i, j: (i, 0)),
        out_shape=jax.ShapeDtypeStruct((n, d), F32),
        scratch_shapes=[pltpu.VMEM((d, tt), F32), pltpu.VMEM((eb, tt), BF16)],
        compiler_params=_cparams("arbitrary", "arbitrary", vmem=PEER_VMEM_LIMIT),
        name="peer_experts",
    )(ht, u_tab, vt_tab, p1, cnt, r2, p2, x2d, g, *extra_args)


def kernel(x, c, w_ada, b_ada, norm_mix, norm_ffn, w_in, ssm_lambda_re, ssm_lambda_im, ssm_log_dt, ssm_b_re, ssm_b_im, ssm_c_re, ssm_c_im, ssm_d, w_glu, w_br_ssm, w_br_attn, w_out, peer_wq, peer_subkeys, peer_u, peer_v, norm_final):
    bsz, seq, d = x.shape
    depth = w_ada.shape[0]
    n = bsz * seq
    ssm_w = ssm_d.shape[1]
    attn_w = w_br_attn.shape[1]
    assert bsz <= SUBLANES and seq % S5_CHUNK == 0

    c_pad = jnp.pad(c, ((0, SUBLANES - bsz), (0, 0)))
    mod_all = _modulation(c_pad, w_ada, b_ada)
    x2d = x.reshape(n, d)
    ts = S5_CHUNK // SUBLANES
    w_in_all = w_in.astype(BF16)
    u_all = peer_u.astype(BF16)
    vt_all = jnp.swapaxes(peer_v, 1, 2).astype(BF16)

    for l in range(depth):
        mods = [mod_all[l, :bsz, i * d:(i + 1) * d].reshape(bsz, 1, d) for i in range(N_MOD)]
        sh_m, sc_m, g_m, sh_f, sc_f, g_f = mods

        h = _norm_mod(x2d, norm_mix[l], sc_m, sh_m, seq)
        u, qkv, gates = _in_proj(h, w_in_all, l, ssm_w, 3 * attn_w)

        a_re, a_im, bbar_re, bbar_im, at_re, at_im = _s5_prep(
            ssm_lambda_re[l], ssm_lambda_im[l], ssm_log_dt[l], ssm_b_re[l], ssm_b_im[l], ts)
        bblk, cblk = _s5_block_weights(bbar_re, bbar_im, ssm_c_re[l], ssm_c_im[l])
        y_s = _s5_branch(u, seq, a_re, a_im, at_re, at_im, bblk, cblk, ssm_d[l],
                         w_glu[l].astype(BF16))
        y_a = _dilated_attention(qkv, seq)
        merged = _merge(y_s, y_a, w_br_ssm[l].astype(BF16), w_br_attn[l].astype(BF16), gates)
        x2d, ht = _outproj_norm(merged, w_out[l].astype(BF16), x2d, g_m, norm_ffn[l], sc_f, sh_f, seq)
        p1, cnt, r2, p2 = _router(peer_wq[l].T.astype(BF16), ht, peer_subkeys[l].astype(BF16))
        x2d = _peer(ht, u_all, vt_all, l, p1, cnt, r2, p2, x2d, g_f, seq,
                    final_gain=norm_final if l == depth - 1 else None)

    return x2d.reshape(bsz, seq, d)
```

```python
import functools
import math

import jax
import jax.numpy as jnp
import numpy as np
from jax import lax
from jax.experimental import pallas as pl
from jax.experimental.pallas import tpu as pltpu

F32 = jnp.float32
BF16 = jnp.bfloat16

EPS = 1e-6
N_MOD = 6
SSM_GROUP = 16
SSM_STATE = 64
HEAD_DIM = 128
ATTN_BLOCK = 128
ATTN_GROUP = 8
DILATED_PATTERNS = ((128, 1), (512, 4), (2048, 16))
PEER_HEADS = 8
PEER_NKEYS = 128
PEER_TOPK = 16
OUTPROJ_PIECES = 4

LANES = 128
SUBLANES = 8
S5_CHUNK = 256
S5_LANE_BLOCK = 512
S5_CH_BLOCK = 128
VMEM_LIMIT = 56 * 1024 * 1024
PEER_VMEM_LIMIT = 60 * 1024 * 1024


def _cparams(*sem, vmem=VMEM_LIMIT):
    return pltpu.CompilerParams(dimension_semantics=sem, vmem_limit_bytes=vmem)


def _gelu_x2(x):
    return x * (1.0 + lax.erf(x * np.float32(math.sqrt(0.5))))


def _gelu(x):
    return 0.5 * _gelu_x2(x)


def _split3(x):
    hi = x.astype(BF16)
    r1 = x - hi.astype(F32)
    mid = r1.astype(BF16)
    lo = (r1 - mid.astype(F32)).astype(BF16)
    return hi, mid, lo


def _mod_kernel(c_ref, w_ref, b_ref, o_ref):
    c = c_ref[...]
    ca = c * jax.nn.sigmoid(c)
    w = w_ref[0]
    c_hi = ca.astype(BF16)
    c_lo = (ca - c_hi.astype(F32)).astype(BF16)
    w_hi = w.astype(BF16)
    w_lo = (w - w_hi.astype(F32)).astype(BF16)
    acc = jnp.dot(c_hi, w_hi, preferred_element_type=F32)
    acc += jnp.dot(c_lo, w_hi, preferred_element_type=F32)
    acc += jnp.dot(c_hi, w_lo, preferred_element_type=F32)
    o_ref[0] = acc + b_ref[0]


def _modulation(c_pad, w_ada, b_ada, tn=1024):
    depth, d, cols = w_ada.shape
    return pl.pallas_call(
        _mod_kernel,
        grid=(depth, cols // tn),
        in_specs=[
            pl.BlockSpec((SUBLANES, d), lambda l, j: (0, 0)),
            pl.BlockSpec((1, d, tn), lambda l, j: (l, 0, j)),
            pl.BlockSpec((1, 1, tn), lambda l, j: (l, 0, j)),
        ],
        out_specs=pl.BlockSpec((1, SUBLANES, tn), lambda l, j: (l, 0, j)),
        out_shape=jax.ShapeDtypeStruct((depth, SUBLANES, cols), F32),
        compiler_params=_cparams("arbitrary", "arbitrary"),
        name="adaln_mod",
    )(c_pad, w_ada, b_ada.reshape(depth, 1, cols))


def _rmsnorm(x, gain):
    ms = jnp.mean(x * x, axis=-1, keepdims=True)
    return x * lax.rsqrt(ms + EPS) * gain


def _norm_mod_kernel(x_ref, gain_ref, sc_ref, sh_ref, o_ref):
    h = _rmsnorm(x_ref[...], gain_ref[...]) * (1.0 + sc_ref[0]) + sh_ref[0]
    o_ref[...] = h.astype(o_ref.dtype)


def _norm_mod(x2d, gain, sc, sh, seq, tm=512):
    n, d = x2d.shape
    per_batch = seq // tm
    mod = pl.BlockSpec((1, 1, d), lambda i: (i // per_batch, 0, 0))
    return pl.pallas_call(
        _norm_mod_kernel,
        grid=(n // tm,),
        in_specs=[pl.BlockSpec((tm, d), lambda i: (i, 0)),
                  pl.BlockSpec((1, d), lambda i: (0, 0)),
                  mod, mod],
        out_specs=pl.BlockSpec((tm, d), lambda i: (i, 0)),
        out_shape=jax.ShapeDtypeStruct((n, d), BF16),
        compiler_params=_cparams("arbitrary"),
        name="norm_mod",
    )(x2d, gain.reshape(1, d), sc, sh)


def _in_proj_kernel(h_ref, w_ref, u_ref, qkv_ref, gates_ref, *, nu, nqkv):
    j = pl.program_id(1)

    def proj():
        return jnp.dot(h_ref[...], w_ref[...], preferred_element_type=F32)

    @pl.when(j < nu)
    def _():
        u_ref[...] = proj()

    @pl.when((j >= nu) & (j < nu + nqkv))
    def _():
        qkv_ref[...] = proj()

    @pl.when(j >= nu + nqkv)
    def _():
        gates_ref[...] = jax.nn.sigmoid(proj()).astype(gates_ref.dtype)


def _in_proj(h, w_all, layer, u_cols, qkv_cols, tm=1024, tn=1024):
    n, d = h.shape
    cols = w_all.shape[2]
    nu, nqkv = u_cols // tn, qkv_cols // tn
    ng = cols // tn - nu - nqkv
    return pl.pallas_call(
        functools.partial(_in_proj_kernel, nu=nu, nqkv=nqkv),
        grid=(n // tm, cols // tn),
        in_specs=[
            pl.BlockSpec((tm, d), lambda i, j: (i, 0)),
            pl.BlockSpec((None, d, tn), lambda i, j: (layer, 0, j)),
        ],
        out_specs=(
            pl.BlockSpec((tm, tn), lambda i, j: (i, jnp.minimum(j, nu - 1))),
            pl.BlockSpec((tm, tn), lambda i, j: (i, jnp.clip(j - nu, 0, nqkv - 1))),
            pl.BlockSpec((tm, tn), lambda i, j: (i, jnp.clip(j - nu - nqkv, 0, ng - 1))),
        ),
        out_shape=(jax.ShapeDtypeStruct((n, u_cols), F32),
                   jax.ShapeDtypeStruct((n, qkv_cols), F32),
                   jax.ShapeDtypeStruct((n, ng * tn), BF16)),
        compiler_params=_cparams("arbitrary", "arbitrary"),
        name="in_proj",
    )(h, w_all)


def _merge_kernel(ys_ref, ya_ref, ws_ref, wa_ref, gs_ref, ga_ref, o_ref):
    ps = jnp.dot(ys_ref[...], ws_ref[...], preferred_element_type=F32)
    pa = jnp.dot(ya_ref[...], wa_ref[...], preferred_element_type=F32)
    o_ref[...] = (gs_ref[...].astype(F32) * ps
                  + ga_ref[...].astype(F32) * pa).astype(o_ref.dtype)


def _merge(ys, ya, ws, wa, gates, tm=1024, tn=1024):
    n, ks = ys.shape
    ka = ya.shape[1]
    d = ws.shape[1]
    goff = d // tn
    return pl.pallas_call(
        _merge_kernel,
        grid=(n // tm, d // tn),
        in_specs=[
            pl.BlockSpec((tm, ks), lambda i, j: (i, 0)),
            pl.BlockSpec((tm, ka), lambda i, j: (i, 0)),
            pl.BlockSpec((ks, tn), lambda i, j: (0, j)),
            pl.BlockSpec((ka, tn), lambda i, j: (0, j)),
            pl.BlockSpec((tm, tn), lambda i, j: (i, j)),
            pl.BlockSpec((tm, tn), lambda i, j: (i, j + goff)),
        ],
        out_specs=pl.BlockSpec((tm, tn), lambda i, j: (i, j)),
        out_shape=jax.ShapeDtypeStruct((n, d), BF16),
        compiler_params=_cparams("arbitrary", "arbitrary"),
        name="branch_merge",
    )(ys, ya, ws, wa, gates, gates)


def _outproj_norm_kernel(a_ref, w_ref, x_ref, g_ref, gain_ref, sc_ref, sh_ref, xo_ref, ht_ref):
    w = w_ref[...]
    piece = a_ref.shape[0] // OUTPROJ_PIECES
    for c in range(OUTPROJ_PIECES):
        rows = slice(c * piece, (c + 1) * piece)
        acc = jnp.dot(a_ref[rows, :], w, preferred_element_type=F32)
        xn = x_ref[rows, :] + g_ref[0] * acc
        xo_ref[rows, :] = xn
        h = _rmsnorm(xn, gain_ref[...]) * (1.0 + sc_ref[0]) + sh_ref[0]
        ht_ref[:, rows] = h.T.astype(ht_ref.dtype)


def _outproj_norm(a, w, x2d, g, gain, sc, sh, seq, tm=512):
    n, k = a.shape
    d = w.shape[1]
    per_batch = seq // tm
    mod = pl.BlockSpec((1, 1, d), lambda i: (i // per_batch, 0, 0))
    return pl.pallas_call(
        _outproj_norm_kernel,
        grid=(n // tm,),
        in_specs=[
            pl.BlockSpec((tm, k), lambda i: (i, 0)),
            pl.BlockSpec((k, d), lambda i: (0, 0)),
            pl.BlockSpec((tm, d), lambda i: (i, 0)),
            mod,
            pl.BlockSpec((1, d), lambda i: (0, 0)),
            mod, mod,
        ],
        out_specs=(pl.BlockSpec((tm, d), lambda i: (i, 0)),
                   pl.BlockSpec((d, tm), lambda i: (0, i))),
        out_shape=(jax.ShapeDtypeStruct((n, d), F32), jax.ShapeDtypeStruct((d, n), BF16)),
        compiler_params=_cparams("arbitrary"),
        name="out_proj_norm",
    )(a, w, x2d, g, gain.reshape(1, d), sc, sh)


def _s5_prep_kernel(lr_ref, li_ref, ldt_ref, br_ref, bi_ref,
                    are_ref, aim_ref, bbr_ref, bbi_ref, atr_ref, ati_ref, *, ts):
    lr, li = lr_ref[...], li_ref[...]
    dt = jnp.exp(ldt_ref[...])
    mag = jnp.exp(lr * dt)
    a_re = mag * jnp.cos(li * dt)
    a_im = mag * jnp.sin(li * dt)
    inv = 1.0 / (lr * lr + li * li)
    coef_re = ((a_re - 1.0) * lr + a_im * li) * inv
    coef_im = (a_im * lr - (a_re - 1.0) * li) * inv
    br, bi = br_ref[...], bi_ref[...]
    bbr_ref[...] = coef_re * br - coef_im * bi
    bbi_ref[...] = coef_re * bi + coef_im * br
    are_ref[...] = a_re
    aim_ref[...] = a_im
    pr, pi = a_re, a_im
    for _ in range(ts - 1):
        pr, pi = pr * a_re - pi * a_im, pr * a_im + pi * a_re
    atr_ref[...] = pr
    ati_ref[...] = pi


def _s5_prep(lam_re, lam_im, log_dt, b_re, b_im, ts):
    g, p, c = b_re.shape
    ns = g * p
    row = lambda t: t.reshape(1, ns)
    ldt = jnp.broadcast_to(log_dt[:, None], (g, p))
    bt = lambda t: jnp.transpose(t, (2, 0, 1)).reshape(c, ns)
    shapes = (
        jax.ShapeDtypeStruct((1, ns), F32), jax.ShapeDtypeStruct((1, ns), F32),
        jax.ShapeDtypeStruct((c, ns), F32), jax.ShapeDtypeStruct((c, ns), F32),
        jax.ShapeDtypeStruct((1, ns), F32), jax.ShapeDtypeStruct((1, ns), F32),
    )
    return pl.pallas_call(
        functools.partial(_s5_prep_kernel, ts=ts),
        out_shape=shapes,
        name="s5_prep",
    )(row(lam_re), row(lam_im), row(ldt), bt(b_re), bt(b_im))


def _s5_block_weights(bbar_re, bbar_im, c_re, c_im):
    c, ns = bbar_re.shape
    gpb = S5_CH_BLOCK // c
    nk = ns // (gpb * SSM_STATE)
    eye = jnp.eye(gpb, dtype=F32)

    def bpart(t):
        t = t.reshape(c, nk, gpb, SSM_STATE)
        t = jnp.einsum("ckjp,ij->kicjp", t, eye)
        return t.reshape(nk, gpb * c, gpb * SSM_STATE)

    def cpart(t):
        t = t.reshape(nk, gpb, c, SSM_STATE)
        t = jnp.einsum("kicp,ij->kjpic", t, eye)
        return t.reshape(nk, gpb * SSM_STATE, gpb * c)

    bblk = jnp.concatenate([bpart(bbar_re), bpart(bbar_im)], axis=2).astype(BF16)
    cblk = jnp.concatenate([cpart(c_re), cpart(-c_im)], axis=1).astype(BF16)
    return bblk, cblk


def _s5_kernel(u_ref, pm_ref, pmt_ref, bblk_ref, are_ref, aim_ref, atr_ref, ati_ref,
               cblk_ref, d_ref, wglu_ref, o_ref,
               hr, hi, car_r, car_i, cs_r, cs_i, yp, *, ts, nk):
    lw = S5_LANE_BLOCK
    cw = S5_CH_BLOCK

    @pl.when(pl.program_id(1) == 0)
    def _():
        car_r[...] = jnp.zeros_like(car_r)
        car_i[...] = jnp.zeros_like(car_i)

    u = u_ref[...]
    up = jnp.dot(pm_ref[...], u.astype(BF16), preferred_element_type=F32).astype(BF16)
    for k in range(nk):
        bu = jnp.dot(up[:, k * cw:(k + 1) * cw], bblk_ref[k], preferred_element_type=F32)
        hr[:, k * lw:(k + 1) * lw] = bu[:, :lw]
        hi[:, k * lw:(k + 1) * lw] = bu[:, lw:]

    for k in range(nk):
        sl = pl.ds(k * lw, lw)
        ar = are_ref[:, sl]
        ai = aim_ref[:, sl]

        def end_step(j, carry, sl=sl, ar=ar, ai=ai):
            cr, cim = carry
            rows = pl.ds(pl.multiple_of(j * SUBLANES, SUBLANES), SUBLANES)
            return ar * cr - ai * cim + hr[rows, sl], ar * cim + ai * cr + hi[rows, sl]

        zero = jnp.zeros((SUBLANES, lw), F32)
        end_r, end_i = lax.fori_loop(0, ts, end_step, (zero, zero), unroll=True)
        cs_r[:, sl] = end_r
        cs_i[:, sl] = end_i

    c_r = car_r[...]
    c_i = car_i[...]
    at_r = atr_ref[...]
    at_i = ati_ref[...]
    for s in range(SUBLANES):
        l_r = cs_r[s:s + 1, :]
        l_i = cs_i[s:s + 1, :]
        cs_r[s:s + 1, :] = c_r
        cs_i[s:s + 1, :] = c_i
        c_r, c_i = l_r + at_r * c_r - at_i * c_i, l_i + at_r * c_i + at_i * c_r
    car_r[...] = c_r
    car_i[...] = c_i

    for k in range(nk):
        sl = pl.ds(k * lw, lw)
        ar = are_ref[:, sl]
        ai = aim_ref[:, sl]

        def scan_step(j, carry, sl=sl, ar=ar, ai=ai):
            cr, cim = carry
            rows = pl.ds(pl.multiple_of(j * SUBLANES, SUBLANES), SUBLANES)
            nr = ar * cr - ai * cim + hr[rows, sl]
            ni = ar * cim + ai * cr + hi[rows, sl]
            hr[rows, sl] = nr
            hi[rows, sl] = ni
            return nr, ni

        lax.fori_loop(0, ts, scan_step, (cs_r[:, sl], cs_i[:, sl]), unroll=True)
        hcat = jnp.concatenate([hr[:, sl], hi[:, sl]], axis=1).astype(BF16)
        yp[:, k * cw:(k + 1) * cw] = jnp.dot(hcat, cblk_ref[k], preferred_element_type=F32)

    y1, y2, y3 = _split3(yp[...])
    pmt = pmt_ref[...]
    y = (jnp.dot(pmt, y1, preferred_element_type=F32)
         + jnp.dot(pmt, y2, preferred_element_type=F32)
         + jnp.dot(pmt, y3, preferred_element_type=F32))
    y = y + d_ref[...] * u
    y = _gelu(y)
    gate = jax.nn.sigmoid(jnp.dot(y.astype(BF16), wglu_ref[...], preferred_element_type=F32))
    o_ref[...] = (y * gate).astype(o_ref.dtype)


def _s5_permutation(t_rows, ts):
    r = np.arange(t_rows)
    src = (r % SUBLANES) * ts + r // SUBLANES
    pm = np.zeros((t_rows, t_rows), np.float32)
    pm[r, src] = 1.0
    return jnp.asarray(pm, BF16), jnp.asarray(pm.T, BF16)


def _s5_branch(u, seq, a_re, a_im, at_re, at_im, bblk, cblk, d_skip, w_glu):
    n, w = u.shape
    ns = a_re.shape[1]
    t_rows = S5_CHUNK
    ts = t_rows // SUBLANES
    nk = ns // S5_LANE_BLOCK
    nchunk = seq // t_rows
    pm, pmt = _s5_permutation(t_rows, ts)
    full = lambda shape: pl.BlockSpec(shape, lambda b, c: (0,) * len(shape))
    rep8 = lambda row: jnp.broadcast_to(row, (SUBLANES, ns))
    return pl.pallas_call(
        functools.partial(_s5_kernel, ts=ts, nk=nk),
        grid=(n // seq, nchunk),
        in_specs=[
            pl.BlockSpec((t_rows, w), lambda b, c: (b * nchunk + c, 0)),
            full((t_rows, t_rows)), full((t_rows, t_rows)),
            full(bblk.shape),
            full((SUBLANES, ns)), full((SUBLANES, ns)), full((1, ns)), full((1, ns)),
            full(cblk.shape),
            full((1, w)), full((w, w)),
        ],
        out_specs=pl.BlockSpec((t_rows, w), lambda b, c: (b * nchunk + c, 0)),
        out_shape=jax.ShapeDtypeStruct((n, w), BF16),
        scratch_shapes=[
            pltpu.VMEM((t_rows, ns), F32), pltpu.VMEM((t_rows, ns), F32),
            pltpu.VMEM((1, ns), F32), pltpu.VMEM((1, ns), F32),
            pltpu.VMEM((SUBLANES, ns), F32), pltpu.VMEM((SUBLANES, ns), F32),
            pltpu.VMEM((t_rows, w), F32),
        ],
        compiler_params=_cparams("arbitrary", "arbitrary"),
        name="s5_branch",
    )(u, pm, pmt, bblk, rep8(a_re), rep8(a_im), at_re, at_im, cblk, d_skip.reshape(1, w), w_glu)


def _attn_kernel(q_ref, k_ref, v_ref, o_ref, o_s, lse_s, *, seq, patterns, scale):
    blk = ATTN_BLOCK
    qi = lax.broadcasted_iota(jnp.int32, (blk, 2 * blk), 0)
    kj = lax.broadcasted_iota(jnp.int32, (blk, 2 * blk), 1)
    dist = blk + qi - kj
    nt = (((1,), (1,)), ((), ()))

    for pi, (window, dil) in enumerate(patterns):
        n_back = window // dil
        nblk = seq // dil // blk
        grp = min(ATTN_GROUP, nblk)
        ngrp = nblk // grp
        band = (dist >= 0) & (dist <= n_back)

        nres = max(1, ATTN_GROUP // grp)

        def group_step(idx, carry, pi=pi, dil=dil, grp=grp, ngrp=ngrp, nres=nres, band=band):
            r0 = (idx // ngrp) * nres
            n0 = (idx % ngrp) * grp

            def rows_of(r, nb):
                start = r + nb * (blk * dil)
                return pl.ds(start, blk) if dil == 1 else pl.ds(start, blk, stride=dil)

            chains = [[rows_of(r0 + ri, jnp.maximum(n0 - 1, 0))]
                      + [rows_of(r0 + ri, n0 + gi) for gi in range(grp)] for ri in range(nres)]
            units = [(ri, gi) for ri in range(nres) for gi in range(grp)]
            ks = [[k_ref[rw, :].astype(BF16) for rw in rows] for rows in chains]
            scores = []
            for ri, gi in units:
                q = q_ref[chains[ri][gi + 1], :].astype(BF16)
                scores.append(jnp.concatenate(
                    [lax.dot_general(q, ks[ri][gi], nt, preferred_element_type=F32),
                     lax.dot_general(q, ks[ri][gi + 1], nt, preferred_element_type=F32)], axis=1))
            probs, inv_l = [], []
            for (ri, gi), sc in zip(units, scores):
                rows = chains[ri][gi + 1]
                mask = band & ((kj >= blk) | (n0 > 0)) if gi == 0 else band
                s = jnp.where(mask, sc * scale, -jnp.inf)
                m = jnp.max(s, axis=1, keepdims=True)
                p = jnp.exp(s - m)
                l = jnp.sum(p, axis=1, keepdims=True)
                lse_s[pi, rows, :] = jnp.broadcast_to(m + jnp.log(l), (blk, HEAD_DIM))
                inv_l.append(1.0 / l)
                probs.append(p.astype(BF16))
            vs = [[v_ref[rw, :].astype(BF16) for rw in rows] for rows in chains]
            for (ri, gi), pb, il in zip(units, probs, inv_l):
                o_s[pi, chains[ri][gi + 1], :] = il * (
                    jnp.dot(pb[:, :blk], vs[ri][gi], preferred_element_type=F32)
                    + jnp.dot(pb[:, blk:], vs[ri][gi + 1], preferred_element_type=F32))
            return carry

        lax.fori_loop(0, (dil // nres) * ngrp, group_step, 0, unroll=2)

    npat = len(patterns)

    def merge_step(t, carry):
        rows = pl.ds(pl.multiple_of(t * blk, blk), blk)
        lses = [lse_s[pi, rows, :] for pi in range(npat)]
        mx = functools.reduce(jnp.maximum, lses)
        num = jnp.zeros((blk, HEAD_DIM), F32)
        den = jnp.zeros((blk, HEAD_DIM), F32)
        for pi in range(npat):
            w = jnp.exp(lses[pi] - mx)
            num = num + w * o_s[pi, rows, :]
            den = den + w
        o_ref[rows, :] = (num / den).astype(o_ref.dtype)
        return carry

    lax.fori_loop(0, seq // blk, merge_step, 0)


def _dilated_attention(qkv, seq):
    n = qkv.shape[0]
    nh = qkv.shape[1] // (3 * HEAD_DIM)
    for window, dil in DILATED_PATTERNS:
        assert window // dil <= ATTN_BLOCK and seq % (dil * ATTN_BLOCK) == 0
    blk = (seq, HEAD_DIM)
    return pl.pallas_call(
        functools.partial(_attn_kernel, seq=seq, patterns=DILATED_PATTERNS,
                          scale=np.float32(HEAD_DIM ** -0.5)),
        grid=(n // seq, nh),
        in_specs=[pl.BlockSpec(blk, lambda b, h: (b, h)),
                  pl.BlockSpec(blk, lambda b, h: (b, nh + h)),
                  pl.BlockSpec(blk, lambda b, h: (b, 2 * nh + h))],
        out_specs=pl.BlockSpec(blk, lambda b, h: (b, h)),
        out_shape=jax.ShapeDtypeStruct((n, nh * HEAD_DIM), BF16),
        scratch_shapes=[pltpu.VMEM((len(DILATED_PATTERNS),) + blk, F32)] * 2,
        compiler_params=_cparams("arbitrary", "arbitrary"),
        name="dilated_attn",
    )(qkv, qkv, qkv)


def _top16(s, kid, exact_ties):
    nkeys = s.shape[0]
    vals = []
    if exact_ties:
        rank = jnp.full(s.shape, float(nkeys), F32)
        for it in range(PEER_TOPK):
            m = jnp.max(s, axis=0, keepdims=True)
            idx = jnp.min(jnp.where(s == m, kid, float(nkeys)), axis=0, keepdims=True)
            sel = kid == idx
            rank = jnp.where(sel, float(it), rank)
            s = jnp.where(sel, -jnp.inf, s)
            vals.append(m)
        return rank, jnp.concatenate(vals, axis=0)
    floor = np.float32(-(2.0 ** 125))
    for it in range(PEER_TOPK):
        m = jnp.max(s, axis=0, keepdims=True)
        s = jnp.where(s == m, np.float32(-(2.0 ** 126 + it * 2.0 ** 103)), s)
        vals.append(m)
    step = (-s - np.float32(2.0 ** 126)) * np.float32(2.0 ** -103)
    rank = jnp.where(s <= floor, jnp.maximum(step, 0.0), float(nkeys))
    return rank, jnp.concatenate(vals, axis=0)


def _router_kernel(wq_ref, ht_ref, sk_ref, p1_ref, n_ref, r2_ref, p2_ref, q_s, *, ngroups):
    nk = PEER_NKEYS
    kt = PEER_TOPK
    half = kt // 2
    kid = lax.broadcasted_iota(jnp.int32, (nk, LANES), 0).astype(F32)
    ia = lax.broadcasted_iota(jnp.int32, (kt, LANES), 0).astype(F32)
    ih = ia[0:half]
    cid = jnp.concatenate([ia * kt] + [ih * kt + float(b) for b in range(1, half)] + [ih + float(half)],
                          axis=0)
    ncand = kt * kt
    tail0 = kt + (half - 1) * half

    q_s[...] = jnp.dot(wq_ref[...], ht_ref[...], preferred_element_type=F32)

    def route(lanes, exact_ties):
        q1 = q_s[0:nk, lanes].astype(BF16)
        q2 = q_s[nk:2 * nk, lanes].astype(BF16)
        s1 = jnp.dot(sk_ref[0], q1, preferred_element_type=F32)
        s2 = jnp.dot(sk_ref[1], q2, preferred_element_type=F32)
        rank1, v1 = _top16(s1, kid, exact_ties)
        rank2, v2 = _top16(s2, kid, exact_ties)
        work = jnp.concatenate([v1 + v2[0:1]] + [v1[0:half] + v2[b:b + 1] for b in range(1, half)]
                               + [v1[0:1] + v2[half:kt]], axis=0)
        m0 = v1[0:1] + v2[0:1]
        z = jnp.zeros((1, LANES), F32)
        if exact_ties:
            n_a = jnp.zeros((kt, LANES), F32)
            for _ in range(kt):
                m = jnp.max(work, axis=0, keepdims=True)
                idx = jnp.min(jnp.where(work == m, cid, float(ncand)), axis=0, keepdims=True)
                work = jnp.where(cid == idx, -jnp.inf, work)
                a_sel = jnp.floor(idx * (1.0 / kt))
                n_a = n_a + jnp.where(ia == a_sel, 1.0, 0.0)
                z = z + jnp.exp(m - m0)
        else:
            for _ in range(kt):
                m = jnp.max(work, axis=0, keepdims=True)
                work = jnp.where(work == m, -jnp.inf, work)
                z = z + jnp.exp(m - m0)
            mark = jnp.where(work == -jnp.inf, 1.0, 0.0)
            low = mark[0:half]
            for b in range(1, half):
                low = low + mark[kt + (b - 1) * half:kt + b * half]
            tail = jnp.sum(mark[tail0:], axis=0, keepdims=True)
            n_a = jnp.concatenate([low + jnp.where(ih == 0.0, tail, 0.0), mark[half:kt]], axis=0)
        nkey = jnp.zeros((nk, LANES), F32)
        for a in range(kt):
            nkey = jnp.where(rank1 == float(a), n_a[a:a + 1], nkey)
        p1_ref[0, :, lanes] = jnp.exp(s1 - v1[0:1])
        n_ref[0, :, lanes] = nkey
        r2_ref[0, :, lanes] = rank2.astype(r2_ref.dtype)
        p2_ref[0, :, lanes] = (jnp.exp(s2 - v2[0:1]) / (2.0 * z)).astype(p2_ref.dtype)
        if exact_ties:
            return None
        ranked = lambda r: jnp.sum(jnp.where(r < float(nk), 1.0, 0.0), axis=0, keepdims=True)
        ok = ((ranked(rank1) == float(kt)) & (ranked(rank2) == float(kt))
              & (jnp.sum(n_a, axis=0, keepdims=True) == float(kt)))
        return jnp.min(jnp.where(ok, 1.0, 0.0))

    groups = [slice(g * LANES, (g + 1) * LANES) for g in range(ngroups)]
    all_distinct = [route(lanes, False) for lanes in groups]
    for lanes, ok in zip(groups, all_distinct):
        @pl.when(ok < 0.5)
        def _(lanes=lanes):
            route(lanes, True)


def _router(wq_t, ht, sk, tt=2048):
    d, n = ht.shape
    nh = PEER_HEADS
    nk = PEER_NKEYS
    out = jax.ShapeDtypeStruct((nh, nk, n), F32)
    out_b = jax.ShapeDtypeStruct((nh, nk, n), BF16)
    ospec = pl.BlockSpec((1, nk, tt), lambda i, h: (h, 0, i))
    return pl.pallas_call(
        functools.partial(_router_kernel, ngroups=tt // LANES),
        grid=(n // tt, nh),
        in_specs=[pl.BlockSpec((2 * nk, d), lambda i, h: (h, 0)),
                  pl.BlockSpec((d, tt), lambda i, h: (0, i)),
                  pl.BlockSpec((2, nk, nk), lambda i, h: (0, 0, 0))],
        out_specs=(ospec, ospec, ospec, ospec),
        out_shape=(out, out, out_b, out_b),
        scratch_shapes=[pltpu.VMEM((2 * nk, tt), F32)],
        compiler_params=_cparams("arbitrary", "arbitrary"),
        name="peer_router",
    )(wq_t, ht, sk)


def _peer_kernel(ht_ref, u_ref, vt_ref, p1_ref, n_ref, r2_ref, p2_ref, x_ref, g_ref, *rest,
                 ne1, nh, final_norm):
    fgain_ref = rest[0] if final_norm else None
    o_ref, acc, gbuf = rest[-3:]
    j = pl.program_id(1)
    nk = PEER_NKEYS

    @pl.when(j == 0)
    def _():
        acc[...] = jnp.zeros_like(acc)

    ht = ht_ref[...]
    for e in range(ne1):
        rows = slice(e * nk, (e + 1) * nk)
        act = _gelu_x2(jnp.dot(u_ref[rows, :], ht, preferred_element_type=F32))
        w = jnp.zeros(act.shape, BF16)
        for h in range(nh):
            p1 = jnp.broadcast_to(p1_ref[h, e:e + 1, :], act.shape).astype(BF16)
            cnt = jnp.broadcast_to(n_ref[h, e:e + 1, :], act.shape).astype(BF16)
            w = w + p1 * jnp.where(r2_ref[h] < cnt, p2_ref[h], jnp.zeros((), BF16))
        gbuf[rows, :] = act.astype(BF16) * w
    acc[...] += jnp.dot(vt_ref[...], gbuf[...], preferred_element_type=F32)

    @pl.when(j == pl.num_programs(1) - 1)
    def _():
        xn = x_ref[...] + g_ref[0] * acc[...].T
        o_ref[...] = _rmsnorm(xn, fgain_ref[...]) if final_norm else xn


def _peer(ht, u_tab, vt_tab, layer, p1, cnt, r2, p2, x2d, g, seq, final_gain=None, tt=512, eb=1024):
    d, n = ht.shape
    ne = u_tab.shape[1]
    nh, nk, _ = p1.shape
    ne1 = eb // nk
    per_batch = seq // tt
    tok = pl.BlockSpec((nh, nk, tt), lambda i, j: (0, 0, i))
    e1b = pl.BlockSpec((nh, ne1, tt), lambda i, j: (0, j, i))
    final_norm = final_gain is not None
    extra_specs = [pl.BlockSpec((1, d), lambda i, j: (0, 0))] if final_norm else []
    extra_args = [final_gain.reshape(1, d)] if final_norm else []
    return pl.pallas_call(
        functools.partial(_peer_kernel, ne1=ne1, nh=nh, final_norm=final_norm),
        grid=(n // tt, ne // eb),
        in_specs=[
            pl.BlockSpec((d, tt), lambda i, j: (0, i)),
            pl.BlockSpec((None, eb, d), lambda i, j: (layer, j, 0)),
            pl.BlockSpec((None, d, eb), lambda i, j: (layer, 0, j)),
            e1b, e1b, tok, tok,
            pl.BlockSpec((tt, d), lambda i, j: (i, 0)),
            pl.BlockSpec((1, 1, d), lambda i, j: (i // per_batch, 0, 0)),
        ] + extra_specs,
        out_specs=pl.BlockSpec((tt, d), lambda i, j: (i, 0)),
        out_shape=jax.ShapeDtypeStruct((n, d), F32),
        scratch_shapes=[pltpu.VMEM((d, tt), F32), pltpu.VMEM((eb, tt), BF16)],
        compiler_params=_cparams("arbitrary", "arbitrary", vmem=PEER_VMEM_LIMIT),
        name="peer_experts",
    )(ht, u_tab, vt_tab, p1, cnt, r2, p2, x2d, g, *extra_args)


def kernel(x, c, w_ada, b_ada, norm_mix, norm_ffn, w_in, ssm_lambda_re, ssm_lambda_im, ssm_log_dt, ssm_b_re, ssm_b_im, ssm_c_re, ssm_c_im, ssm_d, w_glu, w_br_ssm, w_br_attn, w_out, peer_wq, peer_subkeys, peer_u, peer_v, norm_final):
    bsz, seq, d = x.shape
    depth = w_ada.shape[0]
    n = bsz * seq
    ssm_w = ssm_d.shape[1]
    attn_w = w_br_attn.shape[1]
    assert bsz <= SUBLANES and seq % S5_CHUNK == 0

    c_pad = jnp.pad(c, ((0, SUBLANES - bsz), (0, 0)))
    mod_all = _modulation(c_pad, w_ada, b_ada)
    x2d = x.reshape(n, d)
    ts = S5_CHUNK // SUBLANES
    w_in_all = w_in.astype(BF16)
    u_all = peer_u.astype(BF16)
    vt_all = jnp.swapaxes(peer_v, 1, 2).astype(BF16)

    for l in range(depth):
        mods = [mod_all[l, :bsz, i * d:(i + 1) * d].reshape(bsz, 1, d) for i in range(N_MOD)]
        sh_m, sc_m, g_m, sh_f, sc_f, g_f = mods

        h = _norm_mod(x2d, norm_mix[l], sc_m, sh_m, seq)
        u, qkv, gates = _in_proj(h, w_in_all, l, ssm_w, 3 * attn_w)

        a_re, a_im, bbar_re, bbar_im, at_re, at_im = _s5_prep(
            ssm_lambda_re[l], ssm_lambda_im[l], ssm_log_dt[l], ssm_b_re[l], ssm_b_im[l], ts)
        bblk, cblk = _s5_block_weights(bbar_re, bbar_im, ssm_c_re[l], ssm_c_im[l])
        y_s = _s5_branch(u, seq, a_re, a_im, at_re, at_im, bblk, cblk, ssm_d[l],
                         w_glu[l].astype(BF16))
        y_a = _dilated_attention(qkv, seq)
        merged = _merge(y_s, y_a, w_br_ssm[l].astype(BF16), w_br_attn[l].astype(BF16), gates)
        x2d, ht = _outproj_norm(merged, w_out[l].astype(BF16), x2d, g_m, norm_ffn[l], sc_f, sh_f, seq)
        p1, cnt, r2, p2 = _router(peer_wq[l].T.astype(BF16), ht, peer_subkeys[l].astype(BF16))
        x2d = _peer(ht, u_all, vt_all, l, p1, cnt, r2, p2, x2d, g_f, seq,
                    final_gain=norm_final if l == depth - 1 else None)

    return x2d.reshape(bsz, seq, d)
```
